```python
import math
import jax
import jax.numpy as jnp
from jax import lax
import numpy as np

D_MODEL = 1024
BATCH = 8
SEQ = 4096
DEPTH = 2

GRID_W = 64
CTX_LEN = 256
NORM_EPS = 1e-6

HYENA_W = 512
HYENA_ORDER = 2
SHORT_CONV = 3
N_BANDS = 16
POS_EMB_DIM = 1 + 2 * N_BANDS
FILTER_HIDDEN = 64
HYENA_TARGET = 1e-2
HYENA_FAST_DECAY = 0.3
HYENA_SLOW_DECAY = 1.5
HYENA_MAX_DECAY = math.log(HYENA_TARGET) / HYENA_FAST_DECAY
HYENA_MIN_DECAY = math.log(HYENA_TARGET) / HYENA_SLOW_DECAY

MLA_HEADS = 16
QK_NOPE = 64
QK_ROPE = 32
V_HEAD = 64
Q_LORA = 384
KV_LORA = 256
ROPE_BASE = 10000.0
AXIS_ROPE = QK_ROPE // 2
ATTN_SCALE = (QK_NOPE + QK_ROPE) ** -0.5
Q_BLOCK = 128

D_FF = -(-8 * D_MODEL // (3 * 256)) * 256

HY_COLS = (HYENA_ORDER + 1) * HYENA_W
OFF_Q = HY_COLS
OFF_KV = OFF_Q + Q_LORA
OFF_G = OFF_KV + KV_LORA + QK_ROPE
IN_COLS = OFF_G + 2 * D_MODEL

kernel_name = "hybrid_hyena_mla_dit_block"


def rmsnorm(x, g):
    xf = x.astype(jnp.float32)
    y = xf * lax.rsqrt(jnp.mean(xf * xf, axis=-1, keepdims=True) + NORM_EPS)
    return (y * g.astype(jnp.float32)).astype(x.dtype)


def modulate(h, shift, scale):
    return h * (1 + scale) + shift


def axial_rope_tables(n_tokens):
    rows = n_tokens // GRID_W
    row = jnp.repeat(jnp.arange(rows, dtype=jnp.float32), GRID_W)
    col = jnp.tile(jnp.arange(GRID_W, dtype=jnp.float32), rows)
    inv = ROPE_BASE ** (-jnp.arange(0, AXIS_ROPE, 2, dtype=jnp.float32) / AXIS_ROPE)
    ang = jnp.concatenate([row[:, None] * inv, col[:, None] * inv], axis=-1)
    return jnp.cos(ang), jnp.sin(ang)


def apply_rope(x, cos, sin):
    half = QK_ROPE // 2
    x1, x2 = x[..., :half], x[..., half:]
    return jnp.concatenate([x1 * cos - x2 * sin, x1 * sin + x2 * cos], axis=-1).astype(x.dtype)


def short_conv(u, w, b):
    L = u.shape[1]
    pad = SHORT_CONV // 2
    up = jnp.pad(u, ((0, 0), (pad, pad), (0, 0)))
    return sum(up[:, j:j + L] * w[j] for j in range(SHORT_CONV)) + b


def hyena_filter_spectrum(L, w1, b1, w2, b2, w3, freq):
    f32 = jnp.float32
    t = jnp.linspace(0.0, 1.0, L, dtype=f32)[:, None]
    bands = jnp.arange(1, N_BANDS + 1, dtype=f32)
    z = jnp.concatenate([t, jnp.cos(2 * math.pi * bands * t), jnp.sin(2 * math.pi * bands * t)], axis=-1)
    h = jnp.sin(freq[0].astype(f32) * (z @ w1.astype(f32) + b1.astype(f32)))
    h = jnp.sin(freq[1].astype(f32) * (h @ w2.astype(f32) + b2.astype(f32)))
    h = (h @ w3.astype(f32)).reshape(L, 2, HYENA_ORDER, HYENA_W)
    deltas = jnp.linspace(HYENA_MIN_DECAY, HYENA_MAX_DECAY, HYENA_W, dtype=f32)
    h = h * jnp.exp(-t * jnp.abs(deltas))[:, None, None, :]
    fwd, bwd = h[:, 0], h[:, 1]
    full = jnp.concatenate([fwd, jnp.zeros_like(fwd[:1]), bwd[:0:-1]], axis=0)
    full = full / jnp.sum(jnp.abs(full), axis=0, keepdims=True)
    return jnp.fft.rfft(full, axis=0)


def long_conv(v, hf):
    L = v.shape[1]
    vf = jnp.fft.rfft(v.astype(jnp.float32), n=2 * L, axis=1)
    y = jnp.fft.irfft(vf * hf[None], n=2 * L, axis=1)[:, :L]
    return y.astype(v.dtype)


def hyena(u, conv_w, conv_b, w1, b1, w2, b2, w3, freq, bias):
    L = u.shape[1]
    u = short_conv(u, conv_w, conv_b)
    parts = jnp.split(u, HYENA_ORDER + 1, axis=-1)
    gates, v = parts[:HYENA_ORDER], parts[HYENA_ORDER]
    hf = hyena_filter_spectrum(L, w1, b1, w2, b2, w3, freq)
    for o in range(HYENA_ORDER):
        v = gates[o] * (long_conv(v, hf[:, o]) + bias[o] * v)
    return v


def mla_queries(cq, q_g, w_uq, cos, sin):
    B, L = cq.shape[:2]
    q = (rmsnorm(cq, q_g) @ w_uq).reshape(B, L, MLA_HEADS, QK_NOPE + QK_ROPE)
    q_nope, q_rope = q[..., :QK_NOPE], q[..., QK_NOPE:]
    if cos is not None:
        q_rope = apply_rope(q_rope, cos[:, None, :], sin[:, None, :])
    return q_nope, q_rope


def mla_keys(ckv_kr, kv_g, w_ukv, cos, sin):
    B, L = ckv_kr.shape[:2]
    ckv, k_rope = ckv_kr[..., :KV_LORA], ckv_kr[..., KV_LORA:]
    kv = (rmsnorm(ckv, kv_g) @ w_ukv).reshape(B, L, MLA_HEADS, QK_NOPE + V_HEAD)
    k_nope, v = kv[..., :QK_NOPE], kv[..., QK_NOPE:]
    if cos is not None:
        k_rope = apply_rope(k_rope, cos, sin)
    return k_nope, k_rope, v


def attend(q_nope, q_rope, k_nope, k_rope, v):
    s = jnp.einsum('bqhd,bkhd->bhqk', q_nope, k_nope) + jnp.einsum('bqhr,bkr->bhqk', q_rope, k_rope)
    p = jax.nn.softmax(s.astype(jnp.float32) * ATTN_SCALE, axis=-1).astype(v.dtype)
    return jnp.einsum('bhqk,bkhd->bqhd', p, v)


def attend_blocked(q_nope, q_rope, k_nope, k_rope, v):
    B, L = q_nope.shape[:2]
    nb = L // Q_BLOCK
    to_blocks = lambda t: jnp.moveaxis(t.reshape(B, nb, Q_BLOCK, *t.shape[2:]), 1, 0)
    out = lax.map(lambda qs: attend(qs[0], qs[1], k_nope, k_rope, v), (to_blocks(q_nope), to_blocks(q_rope)))
    return jnp.moveaxis(out, 0, 1).reshape(B, L, MLA_HEADS * V_HEAD)


def merge(gate_pre, gate_b, y_hy, y_att, hy_proj, mla_proj, w_out):
    g_hy, g_att = jnp.split(jax.nn.sigmoid(gate_pre + gate_b), 2, axis=-1)
    return (g_hy * (y_hy @ hy_proj) + g_att * (y_att @ mla_proj)) @ w_out


def swiglu(h, w_in, w_out):
    g, u = jnp.split(h @ w_in, 2, axis=-1)
    return (jax.nn.silu(g) * u) @ w_out


def setup_inputs(seed: int = 0) -> dict:
    key = jax.random.key(seed)
    keys = iter(jax.random.split(key, 40))

    def normal(shape, scale):
        return scale * jax.random.normal(next(keys), shape, jnp.float32)

    L = DEPTH
    return {
        "x": normal((BATCH, SEQ, D_MODEL), 1.0),
        "c": normal((BATCH, D_MODEL), 1.0),
        "ctx": normal((BATCH, CTX_LEN, D_MODEL), 1.0),
        "c_ctx": normal((D_MODEL,), 1.0),
        "ada_w": normal((L, D_MODEL, 6 * D_MODEL), 0.3 * D_MODEL ** -0.5),
        "ada_b": normal((L, 6 * D_MODEL), 0.02),
        "norm1_g": 1.0 + normal((L, D_MODEL), 0.05),
        "norm2_g": 1.0 + normal((L, D_MODEL), 0.05),
        "w_in": normal((L, D_MODEL, IN_COLS), D_MODEL ** -0.5),
        "gate_b": normal((L, 2 * D_MODEL), 0.02),
        "hy_conv_w": normal((L, SHORT_CONV, HY_COLS), SHORT_CONV ** -0.5),
        "hy_conv_b": normal((L, HY_COLS), 0.02),
        "hy_filt_w1": normal((L, POS_EMB_DIM, FILTER_HIDDEN), POS_EMB_DIM ** -0.5),
        "hy_filt_b1": normal((L, FILTER_HIDDEN), 0.02),
        "hy_filt_w2": normal((L, FILTER_HIDDEN, FILTER_HIDDEN), FILTER_HIDDEN ** -0.5),
        "hy_filt_b2": normal((L, FILTER_HIDDEN), 0.02),
        "hy_filt_w3": normal((L, FILTER_HIDDEN, 2 * HYENA_ORDER * HYENA_W), FILTER_HIDDEN ** -0.5),
        "hy_filt_freq": 1.0 + normal((L, 2, FILTER_HIDDEN), 0.1),
        "hy_bias": normal((L, HYENA_ORDER, HYENA_W), 0.5),
        "hy_proj": normal((L, HYENA_W, D_MODEL), HYENA_W ** -0.5),
        "q_norm_g": 1.0 + normal((L, Q_LORA), 0.05),
        "kv_norm_g": 1.0 + normal((L, KV_LORA), 0.05),
        "w_uq": normal((L, Q_LORA, MLA_HEADS * (QK_NOPE + QK_ROPE)), Q_LORA ** -0.5),
        "w_ukv": normal((L, KV_LORA, MLA_HEADS * (QK_NOPE + V_HEAD)), KV_LORA ** -0.5),
        "mla_proj": normal((L, MLA_HEADS * V_HEAD, D_MODEL), (MLA_HEADS * V_HEAD) ** -0.5),
        "w_out": normal((L, D_MODEL, D_MODEL), D_MODEL ** -0.5),
        "ffn_w_in": normal((L, D_MODEL, 2 * D_FF), D_MODEL ** -0.5),
        "ffn_w_out": normal((L, D_FF, D_MODEL), D_FF ** -0.5),
        "final_norm_g": 1.0 + normal((D_MODEL,), 0.05),
    }


def reference(x, c, ctx, c_ctx, ada_w, ada_b, norm1_g, norm2_g, w_in, gate_b,
              hy_conv_w, hy_conv_b, hy_filt_w1, hy_filt_b1, hy_filt_w2, hy_filt_b2,
              hy_filt_w3, hy_filt_freq, hy_bias, hy_proj, q_norm_g, kv_norm_g,
              w_uq, w_ukv, mla_proj, w_out, ffn_w_in, ffn_w_out, final_norm_g):
    cos_l, sin_l = axial_rope_tables(x.shape[1])
    h_lat, h_ctx = x, ctx
    for i in range(DEPTH):
        last = i == DEPTH - 1
        sh1, sc1, g1, sh2, sc2, g2 = [m[:, None, :] for m in
                                      jnp.split(jax.nn.silu(c) @ ada_w[i] + ada_b[i], 6, axis=-1)]
        csh1, csc1, cg1, csh2, csc2, cg2 = jnp.split(jax.nn.silu(c_ctx) @ ada_w[i] + ada_b[i], 6, axis=-1)
        hy_params = (hy_conv_w[i], hy_conv_b[i], hy_filt_w1[i], hy_filt_b1[i], hy_filt_w2[i],
                     hy_filt_b2[i], hy_filt_w3[i], hy_filt_freq[i], hy_bias[i])

        a_l = modulate(rmsnorm(h_lat, norm1_g[i]), sh1, sc1)
        a_c = modulate(rmsnorm(h_ctx, norm1_g[i]), csh1, csc1)
        u_l = a_l @ w_in[i]
        if last:
            u_kv_c = a_c @ w_in[i][:, OFF_KV:OFF_G]
        else:
            u_c = a_c @ w_in[i]
            u_kv_c = u_c[..., OFF_KV:OFF_G]

        kn_c, kr_c, v_c = mla_keys(u_kv_c, kv_norm_g[i], w_ukv[i], None, None)
        kn_l, kr_l, v_l = mla_keys(u_l[..., OFF_KV:OFF_G], kv_norm_g[i], w_ukv[i], cos_l, sin_l)
        qn_l, qr_l = mla_queries(u_l[..., OFF_Q:OFF_KV], q_norm_g[i], w_uq[i], cos_l, sin_l)
        kn_all = jnp.concatenate([kn_l, kn_c], axis=1)
        kr_all = jnp.concatenate([kr_l, kr_c], axis=1)
        v_all = jnp.concatenate([v_l, v_c], axis=1)
        att_l = attend_blocked(qn_l, qr_l, kn_all, kr_all, v_all)
        hy_l = hyena(u_l[..., :OFF_Q], *hy_params)
        mix_l = merge(u_l[..., OFF_G:], gate_b[i], hy_l, att_l, hy_proj[i], mla_proj[i], w_out[i])
        h_lat = h_lat + g1 * mix_l

        f_l = modulate(rmsnorm(h_lat, norm2_g[i]), sh2, sc2)
        h_lat = h_lat + g2 * swiglu(f_l, ffn_w_in[i], ffn_w_out[i])

        if not last:
            qn_c, qr_c = mla_queries(u_c[..., OFF_Q:OFF_KV], q_norm_g[i], w_uq[i], None, None)
            att_c = attend(qn_c, qr_c, kn_c, kr_c, v_c).reshape(h_ctx.shape[0], h_ctx.shape[1], MLA_HEADS * V_HEAD)
            hy_c = hyena(u_c[..., :OFF_Q], *hy_params)
            mix_c = merge(u_c[..., OFF_G:], gate_b[i], hy_c, att_c, hy_proj[i], mla_proj[i], w_out[i])
            h_ctx = h_ctx + cg1 * mix_c
            f_c = modulate(rmsnorm(h_ctx, norm2_g[i]), csh2, csc2)
            h_ctx = h_ctx + cg2 * swiglu(f_c, ffn_w_in[i], ffn_w_out[i])

    return rmsnorm(h_lat, final_norm_g)
```

```python
import functools
import math

import numpy as np
import jax
import jax.numpy as jnp
from jax import lax
from jax.experimental import pallas as pl
from jax.experimental.pallas import tpu as pltpu

F32 = jnp.float32
BF16 = jnp.bfloat16
HIGHEST = lax.Precision.HIGHEST

D_MODEL = 1024
GRID_W = 64
NORM_EPS = 1e-6

HYENA_W = 512
HYENA_ORDER = 2
N_BANDS = 16
FILTER_HIDDEN = 64
HYENA_TARGET = 1e-2
HYENA_MAX_DECAY = math.log(HYENA_TARGET) / 0.3
HYENA_MIN_DECAY = math.log(HYENA_TARGET) / 1.5

MLA_HEADS = 16
QK_NOPE = 64
QK_ROPE = 32
V_HEAD = 64
Q_LORA = 384
KV_LORA = 256
ROPE_BASE = 10000.0
AXIS_ROPE = QK_ROPE // 2
ATTN_SCALE = (QK_NOPE + QK_ROPE) ** -0.5
D_FF = 2816

HY_COLS = (HYENA_ORDER + 1) * HYENA_W
OFF_Q = HY_COLS
OFF_KV = OFF_Q + Q_LORA
OFF_KR = OFF_KV + KV_LORA
OFF_G = OFF_KR + QK_ROPE

HEAD_PAD = 128
ROPE_LANE = QK_NOPE
QKV_COLS = Q_LORA + KV_LORA + 2 * HEAD_PAD
HEADS_PER_STEP = 4
NMOD = 16

FFT_N1 = 64
FFT_N2 = 128
VMEM_LIMIT = 56 * 1024 * 1024


def _cparams(*sem):
    return pltpu.CompilerParams(dimension_semantics=sem, vmem_limit_bytes=VMEM_LIMIT)


def _rms(x, g):
    return x * lax.rsqrt(jnp.mean(x * x, axis=-1, keepdims=True) + NORM_EPS) * g


def _silu(x):
    return x * (1.0 / (1.0 + jnp.exp(-x)))


def _sigmoid(x):
    return 1.0 / (1.0 + jnp.exp(-x))


def _mod_row(mods_ref, row, k):
    return mods_ref[pl.ds(row, 1), k * D_MODEL:(k + 1) * D_MODEL]


def _ada_kernel(c_ref, w_ref, b_ref, o_ref):
    s = _silu(c_ref[...])
    o_ref[...] = jnp.dot(s, w_ref[...], preferred_element_type=F32, precision=HIGHEST) + b_ref[...]


def _ada(cc, ada_w, ada_b):
    depth = ada_w.shape[0]
    tn = D_MODEL
    return pl.pallas_call(
        _ada_kernel,
        grid=(depth, 6 * D_MODEL // tn),
        in_specs=[pl.BlockSpec((NMOD, D_MODEL), lambda l, j: (0, 0)),
                  pl.BlockSpec((None, D_MODEL, tn), lambda l, j: (l, 0, j)),
                  pl.BlockSpec((None, 1, tn), lambda l, j: (l, 0, j))],
        out_specs=pl.BlockSpec((None, NMOD, tn), lambda l, j: (l, 0, j)),
        out_shape=jax.ShapeDtypeStruct((depth, NMOD, 6 * D_MODEL), F32),
        compiler_params=_cparams("arbitrary", "arbitrary"),
        name="ada",
    )(cc, ada_w, ada_b.reshape(depth, 1, 6 * D_MODEL))


def _norm_mod_kernel(h_ref, g_ref, mods_ref, o_ref, *, tpb, base, k_shift, k_scale):
    row = base + pl.program_id(0) // tpb
    y = _rms(h_ref[...], g_ref[...])
    o_ref[...] = (y * (1.0 + _mod_row(mods_ref, row, k_scale)) + _mod_row(mods_ref, row, k_shift)).astype(BF16)


def _norm_mod(h, g, mods, *, tpb, base, k_shift, k_scale, tm=512):
    t = h.shape[0]
    return pl.pallas_call(
        functools.partial(_norm_mod_kernel, tpb=tpb, base=base, k_shift=k_shift, k_scale=k_scale),
        grid=(t // tm,),
        in_specs=[pl.BlockSpec((tm, D_MODEL), lambda i: (i, 0)),
                  pl.BlockSpec((1, D_MODEL), lambda i: (0, 0)),
                  pl.BlockSpec((NMOD, 6 * D_MODEL), lambda i: (0, 0))],
        out_specs=pl.BlockSpec((tm, D_MODEL), lambda i: (i, 0)),
        out_shape=jax.ShapeDtypeStruct((t, D_MODEL), BF16),
        compiler_params=_cparams("parallel"),
        name="norm_mod",
    )(h, g.reshape(1, D_MODEL), mods)


def _mm_kernel(a_ref, w_ref, o_ref):
    o_ref[...] = jnp.dot(a_ref[...], w_ref[...], preferred_element_type=F32).astype(o_ref.dtype)


def _matmul(a, w, out_dtype, tm=512):
    m, k = a.shape
    n = w.shape[1]
    return pl.pallas_call(
        _mm_kernel,
        grid=(m // tm,),
        in_specs=[pl.BlockSpec((tm, k), lambda i: (i, 0)),
                  pl.BlockSpec((k, n), lambda i: (0, 0))],
        out_specs=pl.BlockSpec((tm, n), lambda i: (i, 0)),
        out_shape=jax.ShapeDtypeStruct((m, n), out_dtype),
        compiler_params=_cparams("parallel"),
        name="matmul",
    )(a, w)


def _mm_nt_kernel(w_ref, a_ref, o_ref):
    o_ref[...] = lax.dot_general(w_ref[...], a_ref[...], (((1,), (1,)), ((), ())),
                                 preferred_element_type=F32).astype(o_ref.dtype)


def _matmul_nt(wt, a, out_dtype, tm=512):
    n, k = wt.shape
    m = a.shape[0]
    return pl.pallas_call(
        _mm_nt_kernel,
        grid=(m // tm,),
        in_specs=[pl.BlockSpec((n, k), lambda i: (0, 0)),
                  pl.BlockSpec((tm, k), lambda i: (i, 0))],
        out_specs=pl.BlockSpec((n, tm), lambda i: (0, i)),
        out_shape=jax.ShapeDtypeStruct((n, m), out_dtype),
        compiler_params=_cparams("parallel"),
        name="matmul_nt",
    )(wt, a)


def _mm_hi_kernel(a_ref, w_ref, o_ref):
    o_ref[...] = jnp.dot(a_ref[...], w_ref[...], preferred_element_type=F32, precision=HIGHEST)


def _matmul_hi(a, w, tm):
    m, k = a.shape
    n = w.shape[1]
    return pl.pallas_call(
        _mm_hi_kernel,
        grid=(m // tm,),
        in_specs=[pl.BlockSpec((tm, k), lambda i: (i, 0)),
                  pl.BlockSpec((k, n), lambda i: (0, 0))],
        out_specs=pl.BlockSpec((tm, n), lambda i: (i, 0)),
        out_shape=jax.ShapeDtypeStruct((m, n), F32),
        compiler_params=_cparams("parallel"),
        name="matmul_hi",
    )(a, w)


def _qkv_kernel(u_ref, qg_ref, kvg_ref, wq_ref, wk_ref, wv_ref, ct_ref, st_ref, q_ref, k_ref, v_ref):
    u = u_ref[...]
    nq = _rms(u[:, :Q_LORA], qg_ref[...]).astype(BF16)
    nkv = _rms(u[:, Q_LORA:Q_LORA + KV_LORA], kvg_ref[...]).astype(BF16)
    ka = u[:, Q_LORA + KV_LORA:Q_LORA + KV_LORA + HEAD_PAD]
    kb = u[:, Q_LORA + KV_LORA + HEAD_PAD:]
    ct = ct_ref[...]
    st = st_ref[...]
    kr = ka * ct + kb * st
    qq = jnp.dot(nq, wq_ref[...], preferred_element_type=F32)
    kn = jnp.dot(nkv, wk_ref[...], preferred_element_type=F32)
    nh = MLA_HEADS * HEAD_PAD
    for h in range(MLA_HEADS):
        lo, hi = h * HEAD_PAD, (h + 1) * HEAD_PAD
        qh = qq[:, lo:hi] * ct + qq[:, nh + lo:nh + hi] * st
        q_ref[:, lo:hi] = (qh * ATTN_SCALE).astype(BF16)
        k_ref[:, lo:hi] = (kn[:, lo:hi] + kr).astype(BF16)
    v_ref[...] = jnp.dot(nkv, wv_ref[...], preferred_element_type=F32).astype(BF16)


def _qkv_up(u, qg, kvg, wq, wk, wv, ctab, stab, *, rope, tm=512):
    t = u.shape[0]
    ntab = ctab.shape[0] // tm
    tab_map = (lambda i: (i % ntab, 0)) if rope else (lambda i: (0, 0))
    nh = MLA_HEADS * HEAD_PAD
    nv = MLA_HEADS * V_HEAD
    const = lambda i: (0, 0)
    return pl.pallas_call(
        _qkv_kernel,
        grid=(t // tm,),
        in_specs=[pl.BlockSpec((tm, QKV_COLS), lambda i: (i, 0)),
                  pl.BlockSpec((1, Q_LORA), const),
                  pl.BlockSpec((1, KV_LORA), const),
                  pl.BlockSpec((Q_LORA, 2 * nh), const),
                  pl.BlockSpec((KV_LORA, nh), const),
                  pl.BlockSpec((KV_LORA, nv), const),
                  pl.BlockSpec((tm, HEAD_PAD), tab_map),
                  pl.BlockSpec((tm, HEAD_PAD), tab_map)],
        out_specs=[pl.BlockSpec((tm, nh), lambda i: (i, 0)),
                   pl.BlockSpec((tm, nh), lambda i: (i, 0)),
                   pl.BlockSpec((tm, nv), lambda i: (i, 0))],
        out_shape=[jax.ShapeDtypeStruct((t, nh), BF16),
                   jax.ShapeDtypeStruct((t, nh), BF16),
                   jax.ShapeDtypeStruct((t, nv), BF16)],
        compiler_params=_cparams("parallel"),
        name="qkv_up",
    )(u, qg, kvg, wq, wk, wv, ctab, stab)


def _attn_kernel(*refs, n_kv):
    q_ref = refs[0]
    k_refs = refs[1:1 + n_kv]
    v_refs = refs[1 + n_kv:1 + 2 * n_kv]
    o_ref = refs[1 + 2 * n_kv]
    nt = (((1,), (1,)), ((), ()))
    vw = HEADS_PER_STEP * V_HEAD
    lane_head = lax.broadcasted_iota(jnp.int32, (q_ref.shape[0], vw), 1) // V_HEAD
    res = jnp.zeros((q_ref.shape[0], vw), F32)
    for h in range(HEADS_PER_STEP):
        lo, hi = h * HEAD_PAD, (h + 1) * HEAD_PAD
        q = q_ref[:, lo:hi]
        s = [lax.dot_general(q, k_ref[:, lo:hi], nt, preferred_element_type=F32) for k_ref in k_refs]
        m = s[0].max(axis=-1, keepdims=True)
        for si in s[1:]:
            m = jnp.maximum(m, si.max(axis=-1, keepdims=True))
        p = [jnp.exp(si - m) for si in s]
        l = p[0].sum(axis=-1, keepdims=True)
        for pi in p[1:]:
            l = l + pi.sum(axis=-1, keepdims=True)
        o = jnp.dot(p[0].astype(BF16), v_refs[0][...], preferred_element_type=F32)
        for pi, v_ref in zip(p[1:], v_refs[1:]):
            o = o + jnp.dot(pi.astype(BF16), v_ref[...], preferred_element_type=F32)
        res = jnp.where(lane_head == h, o / l, res)
    o_ref[...] = res.astype(BF16)


def _attention(q, kv_sets, *, nb, tq=256):
    t = q.shape[0]
    lq = t // nb
    nq = lq // tq
    groups = MLA_HEADS // HEADS_PER_STEP
    qw = HEADS_PER_STEP * HEAD_PAD
    vw = HEADS_PER_STEP * V_HEAD
    in_specs = [pl.BlockSpec((tq, qw), lambda b, g, i: (b * nq + i, g))]
    k_specs, v_specs, ks, vs = [], [], [], []
    for k, v, lk in kv_sets:
        k_specs.append(pl.BlockSpec((lk, qw), lambda b, g, i: (b, g)))
        v_specs.append(pl.BlockSpec((lk, vw), lambda b, g, i: (b, g)))
        ks.append(k)
        vs.append(v)
    return pl.pallas_call(
        functools.partial(_attn_kernel, n_kv=len(kv_sets)),
        grid=(nb, groups, nq),
        in_specs=in_specs + k_specs + v_specs,
        out_specs=pl.BlockSpec((tq, vw), lambda b, g, i: (b * nq + i, g)),
        out_shape=jax.ShapeDtypeStruct((t, MLA_HEADS * V_HEAD), BF16),
        compiler_params=_cparams("parallel", "parallel", "arbitrary"),
        name="attention",
    )(q, *ks, *vs)


def _filt_mlp_kernel(z_ref, w1_ref, b1_ref, w2_ref, b2_ref, f0_ref, f1_ref, o_ref):
    h = jnp.dot(w1_ref[...], z_ref[...], preferred_element_type=F32, precision=HIGHEST) + b1_ref[...]
    h = jnp.sin(f0_ref[...] * h)
    h = jnp.dot(w2_ref[...], h, preferred_element_type=F32, precision=HIGHEST) + b2_ref[...]
    o_ref[...] = jnp.sin(f1_ref[...] * h)


def _filt_mlp(zt, w1, b1, w2, b2, freq):
    kz, n = zt.shape
    w1t = jnp.zeros((FILTER_HIDDEN, kz), F32).at[:, :w1.shape[0]].set(w1.T)
    col = lambda a: a.reshape(FILTER_HIDDEN, 1)
    args = (zt, w1t, col(b1), w2.T, col(b2), col(freq[0]), col(freq[1]))
    return pl.pallas_call(
        _filt_mlp_kernel,
        grid=(1,),
        in_specs=[pl.BlockSpec(a.shape, lambda i: (0, 0)) for a in args],
        out_specs=pl.BlockSpec((FILTER_HIDDEN, n), lambda i: (0, 0)),
        out_shape=jax.ShapeDtypeStruct((FILTER_HIDDEN, n), F32),
        compiler_params=_cparams("arbitrary"),
        name="filt_mlp",
    )(*args)


def _filt_taps_kernel(h_ref, w3_ref, d_ref, t_ref, o_ref, *, seq):
    n = h_ref.shape[1]
    pos = lax.broadcasted_iota(jnp.int32, (1, n), 1)
    win = jnp.exp(-t_ref[...] * d_ref[...])
    h = h_ref[...]
    for o in range(HYENA_ORDER):
        fwd = jnp.dot(w3_ref[0, o], h, preferred_element_type=F32, precision=HIGHEST)
        bwd = jnp.dot(w3_ref[1, o], h, preferred_element_type=F32, precision=HIGHEST)
        full = jnp.where(pos < seq, fwd, jnp.where(pos > seq, bwd, 0.0)) * win
        o_ref[o] = full / jnp.sum(jnp.abs(full), axis=1, keepdims=True)


def _filt_taps(h2t, w3, dabs, tvec, seq, cf=64):
    n = h2t.shape[1]
    w3t = w3.T.reshape(2, HYENA_ORDER, HYENA_W, FILTER_HIDDEN)
    return pl.pallas_call(
        functools.partial(_filt_taps_kernel, seq=seq),
        grid=(HYENA_W // cf,),
        in_specs=[pl.BlockSpec((FILTER_HIDDEN, n), lambda c: (0, 0)),
                  pl.BlockSpec((2, HYENA_ORDER, cf, FILTER_HIDDEN), lambda c: (0, 0, c, 0)),
                  pl.BlockSpec((cf, 1), lambda c: (c, 0)),
                  pl.BlockSpec((1, n), lambda c: (0, 0))],
        out_specs=pl.BlockSpec((HYENA_ORDER, cf, n), lambda c: (0, c, 0)),
        out_shape=jax.ShapeDtypeStruct((HYENA_ORDER, HYENA_W, n), F32),
        compiler_params=_cparams("parallel"),
        name="filt_taps",
    )(h2t, w3t, dabs, tvec)


def _filt_spec_kernel(x_ref, f1_ref, tc_ref, ts_ref, f2_ref, o_ref):
    ns = x_ref.shape[0]
    x = jnp.concatenate([x_ref[s] for s in range(ns)], axis=1)
    a = jnp.dot(f1_ref[...], x, preferred_element_type=F32, precision=HIGHEST)
    tc = tc_ref[...]
    ts = ts_ref[...]
    rows = []
    for s in range(ns):
        ar = a[:FFT_N1, s * FFT_N2:(s + 1) * FFT_N2]
        ai = a[FFT_N1:, s * FFT_N2:(s + 1) * FFT_N2]
        rows.append(jnp.concatenate([ar * tc + ai * ts, ai * tc - ar * ts], axis=1))
    b = jnp.concatenate(rows, axis=0)
    z = jnp.dot(b, f2_ref[...], preferred_element_type=F32, precision=HIGHEST)
    o_ref[...] = z.reshape(ns, FFT_N1, 2 * FFT_N2)


def _filt_spec(hfull, consts, ns=16):
    r = hfull.shape[0]
    x = hfull.reshape(r, FFT_N1, FFT_N2)
    f1h, tc, ts, f2 = consts["f1h"], consts["tc"], consts["ts"], consts["f2_f32"]
    const = lambda i: (0, 0)
    return pl.pallas_call(
        _filt_spec_kernel,
        grid=(r // ns,),
        in_specs=[pl.BlockSpec((ns, FFT_N1, FFT_N2), lambda i: (i, 0, 0)),
                  pl.BlockSpec(f1h.shape, const), pl.BlockSpec(tc.shape, const),
                  pl.BlockSpec(ts.shape, const), pl.BlockSpec(f2.shape, const)],
        out_specs=pl.BlockSpec((ns, FFT_N1, 2 * FFT_N2), lambda i: (i, 0, 0)),
        out_shape=jax.ShapeDtypeStruct((r, FFT_N1, 2 * FFT_N2), F32),
        compiler_params=_cparams("parallel"),
        name="filt_spec",
    )(x, f1h, tc, ts, f2)


def _short_conv_tile(x, w, b):
    nrow = x.shape[1]
    lane = lax.broadcasted_iota(jnp.int32, x.shape, 2)
    row = lax.broadcasted_iota(jnp.int32, x.shape, 1)
    r1 = pltpu.roll(x, 1, 2)
    r2 = pltpu.roll(r1, 1, 1)
    prev = jnp.where(lane == 0, jnp.where(row == 0, 0.0, r2), r1)
    l1 = pltpu.roll(x, FFT_N2 - 1, 2)
    l2 = pltpu.roll(l1, nrow - 1, 1)
    nxt = jnp.where(lane == FFT_N2 - 1, jnp.where(row == nrow - 1, 0.0, l2), l1)
    return prev * w[0] + x * w[1] + nxt * w[2] + b


def _long_conv_pair(vr, vi, h, f1, f1i, f2, f2i, tc, ts):
    ns = vr.shape[0]
    half = FFT_N2
    x = jnp.concatenate([jnp.concatenate([vr[s] for s in range(ns)], axis=1),
                         jnp.concatenate([vi[s] for s in range(ns)], axis=1)], axis=0).astype(BF16)
    a = jnp.dot(f1, x, preferred_element_type=F32)
    rows = []
    for s in range(ns):
        ar = a[:FFT_N1, s * half:(s + 1) * half]
        ai = a[FFT_N1:, s * half:(s + 1) * half]
        rows.append(jnp.concatenate([ar * tc + ai * ts, ai * tc - ar * ts], axis=1))
    b = jnp.concatenate(rows, axis=0).astype(BF16)
    z = jnp.dot(b, f2, preferred_element_type=F32).reshape(ns, FFT_N1, 2 * half)
    zr, zi = z[:, :, :half], z[:, :, half:]
    hr, hi = h[:, :, :half], h[:, :, half:]
    w = jnp.concatenate([zr * hr - zi * hi, zr * hi + zi * hr], axis=2)
    w = w.reshape(ns * FFT_N1, 2 * half).astype(BF16)
    c = jnp.dot(w, f2i, preferred_element_type=F32).reshape(ns, FFT_N1, 2 * half)
    cr, ci = c[:, :, :half], c[:, :, half:]
    dr = cr * tc - ci * ts
    di = cr * ts + ci * tc
    d = jnp.concatenate([jnp.concatenate([dr[s] for s in range(ns)], axis=1),
                         jnp.concatenate([di[s] for s in range(ns)], axis=1)], axis=0).astype(BF16)
    y = jnp.dot(f1i, d, preferred_element_type=F32)
    nrow = y.shape[0] // 2
    yr = jnp.stack([y[:nrow, s * half:(s + 1) * half] for s in range(ns)], axis=0)
    yi = jnp.stack([y[nrow:, s * half:(s + 1) * half] for s in range(ns)], axis=0)
    return yr, yi


def _hyena_kernel(x0_ref, x1_ref, v_ref, cw_ref, cb_ref, hb_ref, h_ref,
                  f1_ref, f1i_ref, f2_ref, f2i_ref, tc_ref, ts_ref, o_ref):
    consts = (f1_ref[...], f1i_ref[...], f2_ref[...], f2i_ref[...], tc_ref[...], ts_ref[...])
    g0, g1, v = [], [], []
    for half in range(2):
        g0.append(_short_conv_tile(x0_ref[:, half], cw_ref[0], cb_ref[0]))
        g1.append(_short_conv_tile(x1_ref[:, half], cw_ref[1], cb_ref[1]))
        v.append(_short_conv_tile(v_ref[:, half], cw_ref[2], cb_ref[2]))
    gates = (g0, g1)
    for o in range(HYENA_ORDER):
        yr, yi = _long_conv_pair(v[0], v[1], h_ref[o], *consts)
        v = [gates[o][0] * (yr + hb_ref[o] * v[0]), gates[o][1] * (yi + hb_ref[o] * v[1])]
    o_ref[:, 0] = v[0].astype(BF16)
    o_ref[:, 1] = v[1].astype(BF16)


def _hyena_latent(ut, cw, cb, hb, hspec, consts, *, nb, ct=32):
    seq = ut.shape[1] // nb
    nrow = seq // FFT_N2
    npair = nb // 2
    nct = HYENA_W // ct
    u5 = ut.reshape(3 * HYENA_W, 2, npair, nrow, FFT_N2)
    part = lambda p: pl.BlockSpec((ct, 2, None, nrow, FFT_N2), lambda c, j: (p * nct + c, 0, j, 0, 0))
    const2 = lambda c, j: (0, 0)
    cnames = ("f1", "f1i", "f2", "f2i", "tc", "ts")
    out = pl.pallas_call(
        _hyena_kernel,
        grid=(nct, npair),
        in_specs=[part(0), part(1), part(2),
                  pl.BlockSpec((3, 3, ct, 1, 1), lambda c, j: (0, 0, c, 0, 0)),
                  pl.BlockSpec((3, ct, 1, 1), lambda c, j: (0, c, 0, 0)),
                  pl.BlockSpec((HYENA_ORDER, ct, 1, 1), lambda c, j: (0, c, 0, 0)),
                  pl.BlockSpec((HYENA_ORDER, ct, FFT_N1, 2 * FFT_N2), lambda c, j: (0, c, 0, 0))]
                 + [pl.BlockSpec(consts[k].shape, const2) for k in cnames],
        out_specs=pl.BlockSpec((ct, 2, None, nrow, FFT_N2), lambda c, j: (c, 0, j, 0, 0)),
        out_shape=jax.ShapeDtypeStruct((HYENA_W, 2, npair, nrow, FFT_N2), BF16),
        compiler_params=_cparams("parallel", "arbitrary"),
        name="hyena",
    )(u5, u5, u5, cw, cb, hb, hspec, *[consts[k] for k in cnames])
    return out.reshape(HYENA_W, nb * seq)


def _hyena_ctx_kernel(x0_ref, x1_ref, v_ref, cw_ref, cb_ref, hb_ref, h_ref, cf_ref, ci_ref, o_ref, *, nb):
    seq = x0_ref.shape[1] // nb
    cc = x0_ref.shape[0]
    half = h_ref.shape[2] // 2

    def rows(ref, p):
        x = jnp.concatenate([ref[:, b * seq:(b + 1) * seq] for b in range(nb)], axis=0)
        lane = lax.broadcasted_iota(jnp.int32, x.shape, 1)
        prev = jnp.where(lane == 0, 0.0, pltpu.roll(x, 1, 1))
        nxt = jnp.where(lane == seq - 1, 0.0, pltpu.roll(x, seq - 1, 1))
        w = [jnp.concatenate([cw_ref[p, j]] * nb, axis=0) for j in range(3)]
        return prev * w[0] + x * w[1] + nxt * w[2] + jnp.concatenate([cb_ref[p]] * nb, axis=0)

    g = (rows(x0_ref, 0), rows(x1_ref, 1))
    v = rows(v_ref, 2)
    for o in range(HYENA_ORDER):
        spec = jnp.dot(v.astype(BF16), cf_ref[...], preferred_element_type=F32)
        xr, xi = spec[:, :half], spec[:, half:]
        hh = jnp.concatenate([h_ref[o]] * nb, axis=0)
        hr, hi = hh[:, :half], hh[:, half:]
        y = jnp.concatenate([xr * hr - xi * hi, xr * hi + xi * hr], axis=1).astype(BF16)
        conv = jnp.dot(y, ci_ref[...], preferred_element_type=F32)
        v = g[o] * (conv + jnp.concatenate([hb_ref[o]] * nb, axis=0) * v)
    for b in range(nb):
        o_ref[:, b * seq:(b + 1) * seq] = v[b * cc:(b + 1) * cc].astype(BF16)


def _hyena_ctx(ut, cw, cb, hb, hspec, cfwd, cinv, *, nb, cc=64):
    n = ut.shape[1]
    ncc = HYENA_W // cc
    part = lambda p: pl.BlockSpec((cc, n), lambda c: (p * ncc + c, 0))
    return pl.pallas_call(
        functools.partial(_hyena_ctx_kernel, nb=nb),
        grid=(ncc,),
        in_specs=[part(0), part(1), part(2),
                  pl.BlockSpec((3, 3, cc, 1), lambda c: (0, 0, c, 0)),
                  pl.BlockSpec((3, cc, 1), lambda c: (0, c, 0)),
                  pl.BlockSpec((HYENA_ORDER, cc, 1), lambda c: (0, c, 0)),
                  pl.BlockSpec((HYENA_ORDER, cc, hspec.shape[2]), lambda c: (0, c, 0)),
                  pl.BlockSpec(cfwd.shape, lambda c: (0, 0)),
                  pl.BlockSpec(cinv.shape, lambda c: (0, 0))],
        out_specs=pl.BlockSpec((cc, n), lambda c: (c, 0)),
        out_shape=jax.ShapeDtypeStruct((HYENA_W, n), BF16),
        compiler_params=_cparams("parallel"),
        name="hyena_ctx",
    )(ut, ut, ut, cw, cb, hb, hspec, cfwd, cinv)


def _merge_kernel(h_ref, ug_ref, gb_ref, yt_ref, att_ref, hp_ref, mp_ref, wo_ref, mods_ref, o_ref, *, tpb, base):
    row = base + pl.program_id(0) // tpb
    g = _sigmoid(ug_ref[...] + gb_ref[...])
    yh = lax.dot_general(yt_ref[...], hp_ref[...], (((0,), (0,)), ((), ())), preferred_element_type=F32)
    ya = jnp.dot(att_ref[...], mp_ref[...], preferred_element_type=F32)
    mix = (g[:, :D_MODEL] * yh + g[:, D_MODEL:] * ya).astype(BF16)
    out = jnp.dot(mix, wo_ref[...], preferred_element_type=F32)
    o_ref[...] = h_ref[...] + _mod_row(mods_ref, row, 2) * out


def _merge(h, ug, gate_b, yt, att, hy_proj, mla_proj, w_out, mods, *, tpb, base, tm=512):
    t = h.shape[0]
    const = lambda i: (0, 0)
    return pl.pallas_call(
        functools.partial(_merge_kernel, tpb=tpb, base=base),
        grid=(t // tm,),
        in_specs=[pl.BlockSpec((tm, D_MODEL), lambda i: (i, 0)),
                  pl.BlockSpec((tm, 2 * D_MODEL), lambda i: (i, 0)),
                  pl.BlockSpec((1, 2 * D_MODEL), const),
                  pl.BlockSpec((HYENA_W, tm), lambda i: (0, i)),
                  pl.BlockSpec((tm, MLA_HEADS * V_HEAD), lambda i: (i, 0)),
                  pl.BlockSpec(hy_proj.shape, const),
                  pl.BlockSpec(mla_proj.shape, const),
                  pl.BlockSpec(w_out.shape, const),
                  pl.BlockSpec((NMOD, 6 * D_MODEL), const)],
        out_specs=pl.BlockSpec((tm, D_MODEL), lambda i: (i, 0)),
        out_shape=jax.ShapeDtypeStruct((t, D_MODEL), F32),
        compiler_params=_cparams("parallel"),
        name="merge",
    )(h, ug, gate_b.reshape(1, -1), yt, att, hy_proj, mla_proj, w_out, mods)


def _ffn_kernel(h_ref, g_ref, mods_ref, wg_ref, wu_ref, wo_ref, fg_ref, o_ref, f_scr, *, tpb, base, final):
    j = pl.program_id(1)
    row = base + pl.program_id(0) // tpb

    @pl.when(j == 0)
    def _():
        h = h_ref[...]
        f = _rms(h, g_ref[...]) * (1.0 + _mod_row(mods_ref, row, 4)) + _mod_row(mods_ref, row, 3)
        f_scr[...] = f.astype(BF16)
        o_ref[...] = h

    f = f_scr[...]
    act = _silu(jnp.dot(f, wg_ref[...], preferred_element_type=F32)) * jnp.dot(f, wu_ref[...], preferred_element_type=F32)
    y = jnp.dot(act.astype(BF16), wo_ref[...], preferred_element_type=F32)
    o_ref[...] += _mod_row(mods_ref, row, 5) * y

    if final:
        @pl.when(j == pl.num_programs(1) - 1)
        def _():
            o_ref[...] = _rms(o_ref[...], fg_ref[...])


def _ffn(h, g, mods, w_gate, w_up, w_down, final_g, *, tpb, base, final, tm=512, nchunk=2):
    t = h.shape[0]
    fc = D_FF // nchunk
    const = lambda i, j: (0, 0)
    return pl.pallas_call(
        functools.partial(_ffn_kernel, tpb=tpb, base=base, final=final),
        grid=(t // tm, nchunk),
        in_specs=[pl.BlockSpec((tm, D_MODEL), lambda i, j: (i, 0)),
                  pl.BlockSpec((1, D_MODEL), const),
                  pl.BlockSpec((NMOD, 6 * D_MODEL), const),
                  pl.BlockSpec((D_MODEL, fc), lambda i, j: (0, j)),
                  pl.BlockSpec((D_MODEL, fc), lambda i, j: (0, j)),
                  pl.BlockSpec((fc, D_MODEL), lambda i, j: (j, 0)),
                  pl.BlockSpec((1, D_MODEL), const)],
        out_specs=pl.BlockSpec((tm, D_MODEL), lambda i, j: (i, 0)),
        out_shape=jax.ShapeDtypeStruct((t, D_MODEL), F32),
        scratch_shapes=[pltpu.VMEM((tm, D_MODEL), BF16)],
        compiler_params=_cparams("parallel", "arbitrary"),
        name="ffn",
    )(h, g.reshape(1, -1), mods, w_gate, w_up, w_down, final_g.reshape(1, -1))


def _dft_consts():
    n = FFT_N1 * FFT_N2
    k1 = np.arange(FFT_N1)
    th = 2 * np.pi * np.outer(k1, np.arange(FFT_N1)) / FFT_N1
    c, s = np.cos(th), np.sin(th)
    hn = FFT_N1 // 2
    f1 = np.block([[c[:, :hn], s[:, :hn]], [-s[:, :hn], c[:, :hn]]])
    f1h = np.concatenate([c, -s], axis=0)
    f1i = np.block([[c[:, :hn].T, -s[:, :hn].T], [s[:, :hn].T, c[:, :hn].T]]) / n
    ph = 2 * np.pi * np.outer(k1, np.arange(FFT_N2)) / n
    th2 = 2 * np.pi * np.outer(np.arange(FFT_N2), np.arange(FFT_N2)) / FFT_N2
    c2, s2 = np.cos(th2), np.sin(th2)
    f2 = np.block([[c2, -s2], [s2, c2]])
    f2i = np.block([[c2, s2], [-s2, c2]])
    return {"f1": jnp.asarray(f1, BF16), "f1h": jnp.asarray(f1h, F32), "f1i": jnp.asarray(f1i, BF16),
            "f2": jnp.asarray(f2, BF16), "f2i": jnp.asarray(f2i, BF16), "f2_f32": jnp.asarray(f2, F32),
            "tc": jnp.asarray(np.cos(ph), F32), "ts": jnp.asarray(np.sin(ph), F32)}


def _dense_dft(seq):
    n = 2 * seq
    th = 2 * np.pi * np.outer(np.arange(n), np.arange(n)) / n
    c, s = np.cos(th), np.sin(th)
    full = np.concatenate([c, -s], axis=1)
    inv = np.concatenate([c[:, :seq], -s[:, :seq]], axis=0) / n
    return jnp.asarray(full, F32), jnp.asarray(full[:seq], BF16), jnp.asarray(inv, BF16)


def _filter_positions(seq):
    n = np.arange(2 * seq)
    lag = np.minimum(np.where(n < seq, n, 2 * seq - n), seq - 1)
    return jnp.linspace(0.0, 1.0, seq, dtype=F32)[lag]


def _rope_tables(seq):
    rows = seq // GRID_W
    row = jnp.repeat(jnp.arange(rows, dtype=F32), GRID_W)
    col = jnp.tile(jnp.arange(GRID_W, dtype=F32), rows)
    inv = ROPE_BASE ** (-jnp.arange(0, AXIS_ROPE, 2, dtype=F32) / AXIS_ROPE)
    ang = jnp.concatenate([row[:, None] * inv, col[:, None] * inv], axis=-1)
    cos, sin = jnp.cos(ang), jnp.sin(ang)
    ones = jnp.ones((seq, QK_NOPE), F32)
    pad = HEAD_PAD - QK_NOPE - QK_ROPE
    ctab = jnp.concatenate([ones, cos, cos, jnp.ones((seq, pad), F32)], axis=1)
    stab = jnp.concatenate([0 * ones, sin, sin, jnp.zeros((seq, pad), F32)], axis=1)
    return ctab, stab


def _rot_cols(w):
    half = QK_ROPE // 2
    return jnp.concatenate([-w[..., half:], w[..., :half]], axis=-1)


def _pad_heads(nope, rope):
    k = nope.shape[0]
    z = jnp.zeros((k, MLA_HEADS, HEAD_PAD - QK_NOPE - QK_ROPE), nope.dtype)
    return jnp.concatenate([nope, rope, z], axis=-1).reshape(k, MLA_HEADS * HEAD_PAD)


def kernel(x, c, ctx, c_ctx, ada_w, ada_b, norm1_g, norm2_g, w_in, gate_b, hy_conv_w, hy_conv_b,
           hy_filt_w1, hy_filt_b1, hy_filt_w2, hy_filt_b2, hy_filt_w3, hy_filt_freq, hy_bias, hy_proj,
           q_norm_g, kv_norm_g, w_uq, w_ukv, mla_proj, w_out, ffn_w_in, ffn_w_out, final_norm_g):
    nb, seq, d = x.shape
    lc = ctx.shape[1]
    depth = ada_w.shape[0]
    assert d == D_MODEL and seq == FFT_N1 * FFT_N2 // 2 and nb % 2 == 0 and nb < NMOD
    tm = 512
    tpb_lat = seq // tm
    tpb_ctx = 1 << 30

    cc = jnp.zeros((NMOD, d), F32).at[:nb].set(c).at[nb].set(c_ctx)
    mods_all = _ada(cc, ada_w, ada_b)

    consts = _dft_consts()
    cfull_c, cfwd_c, cinv_c = _dense_dft(lc)
    ctab, stab = _rope_tables(seq)
    one_tab = jnp.ones((tm, HEAD_PAD), F32)
    zero_tab = jnp.zeros((tm, HEAD_PAD), F32)
    dabs = jnp.abs(jnp.linspace(HYENA_MIN_DECAY, HYENA_MAX_DECAY, HYENA_W, dtype=F32)).reshape(HYENA_W, 1)
    bands = jnp.arange(1, N_BANDS + 1, dtype=F32)

    def pos_features(length):
        t = _filter_positions(length)
        ang = 2 * math.pi * bands[:, None] * t[None, :]
        zt = jnp.concatenate([t[None, :], jnp.cos(ang), jnp.sin(ang),
                              jnp.zeros((7, 2 * length), F32)], axis=0)
        return zt, t.reshape(1, -1)

    zt_lat, tv_lat = pos_features(seq)
    zt_ctx, tv_ctx = pos_features(lc)

    h_lat = x.reshape(nb * seq, d)
    h_ctx = ctx.reshape(nb * lc, d)

    for i in range(depth):
        last = i == depth - 1
        mods = mods_all[i]

        wi = w_in[i]
        w_hy_t = wi[:, :OFF_Q].T.astype(BF16)
        kr_w = wi[:, OFF_KR:OFF_G]
        zl = jnp.zeros((d, ROPE_LANE), F32)
        zr = jnp.zeros((d, HEAD_PAD - ROPE_LANE - QK_ROPE), F32)
        w_qkv = jnp.concatenate([wi[:, OFF_Q:OFF_KR], zl, kr_w, zr, zl, _rot_cols(kr_w), zr], axis=1).astype(BF16)
        w_gate_in = wi[:, OFF_G:].astype(BF16)
        uq = w_uq[i].reshape(Q_LORA, MLA_HEADS, QK_NOPE + QK_ROPE)
        wq = jnp.concatenate([_pad_heads(uq[..., :QK_NOPE], uq[..., QK_NOPE:]),
                              _pad_heads(0 * uq[..., :QK_NOPE], _rot_cols(uq[..., QK_NOPE:]))], axis=1).astype(BF16)
        ukv = w_ukv[i].reshape(KV_LORA, MLA_HEADS, QK_NOPE + V_HEAD)
        wk = _pad_heads(ukv[..., :QK_NOPE], jnp.zeros((KV_LORA, MLA_HEADS, QK_ROPE), F32)).astype(BF16)
        wv = ukv[..., QK_NOPE:].reshape(KV_LORA, MLA_HEADS * V_HEAD).astype(BF16)
        qg = q_norm_g[i].reshape(1, -1)
        kvg = kv_norm_g[i].reshape(1, -1)
        hp, mp, wo = hy_proj[i].astype(BF16), mla_proj[i].astype(BF16), w_out[i].astype(BF16)
        w_ffg = ffn_w_in[i][:, :D_FF].astype(BF16)
        w_ffu = ffn_w_in[i][:, D_FF:].astype(BF16)
        w_ffo = ffn_w_out[i].astype(BF16)
        cw = jnp.transpose(hy_conv_w[i].reshape(3, 3, HYENA_W), (1, 0, 2))
        cb = hy_conv_b[i].reshape(3, HYENA_W)
        hb = hy_bias[i]

        filt = (hy_filt_w1[i], hy_filt_b1[i], hy_filt_w2[i], hy_filt_b2[i], hy_filt_freq[i])
        hfull = _filt_taps(_filt_mlp(zt_lat, *filt), hy_filt_w3[i], dabs, tv_lat, seq)
        hspec = _filt_spec(hfull.reshape(HYENA_ORDER * HYENA_W, 2 * seq), consts)
        hspec = hspec.reshape(HYENA_ORDER, HYENA_W, FFT_N1, 2 * FFT_N2)

        a_l = _norm_mod(h_lat, norm1_g[i], mods, tpb=tpb_lat, base=0, k_shift=0, k_scale=1)
        a_c = _norm_mod(h_ctx, norm1_g[i], mods, tpb=tpb_ctx, base=nb, k_shift=0, k_scale=1)
        q_l, k_l, v_l = _qkv_up(_matmul(a_l, w_qkv, F32), qg, kvg, wq, wk, wv, ctab, stab, rope=True)
        q_c, k_c, v_c = _qkv_up(_matmul(a_c, w_qkv, F32), qg, kvg, wq, wk, wv, one_tab, zero_tab, rope=False)
        att_l = _attention(q_l, [(k_l, v_l, seq), (k_c, v_c, lc)], nb=nb)
        ut_l = _matmul_nt(w_hy_t, a_l, F32)
        yt_l = _hyena_latent(ut_l, cw.reshape(3, 3, HYENA_W, 1, 1), cb.reshape(3, HYENA_W, 1, 1),
                             hb.reshape(HYENA_ORDER, HYENA_W, 1, 1), hspec, consts, nb=nb)
        ug_l = _matmul(a_l, w_gate_in, F32)
        h_lat = _merge(h_lat, ug_l, gate_b[i], yt_l, att_l, hp, mp, wo, mods, tpb=tpb_lat, base=0)

        h_lat = _ffn(h_lat, norm2_g[i], mods, w_ffg, w_ffu, w_ffo, final_norm_g,
                     tpb=tpb_lat, base=0, final=last)

        if not last:
            hfull_c = _filt_taps(_filt_mlp(zt_ctx, *filt), hy_filt_w3[i], dabs, tv_ctx, lc)
            hspec_c = _matmul_hi(hfull_c.reshape(HYENA_ORDER * HYENA_W, 2 * lc), cfull_c, tm=256)
            hspec_c = hspec_c.reshape(HYENA_ORDER, HYENA_W, 4 * lc)
            att_c = _attention(q_c, [(k_c, v_c, lc)], nb=nb)
            ut_c = _matmul_nt(w_hy_t, a_c, F32)
            yt_c = _hyena_ctx(ut_c, cw.reshape(3, 3, HYENA_W, 1), cb.reshape(3, HYENA_W, 1),
                              hb.reshape(HYENA_ORDER, HYENA_W, 1), hspec_c, cfwd_c, cinv_c, nb=nb)
            ug_c = _matmul(a_c, w_gate_in, F32)
            h_ctx = _merge(h_ctx, ug_c, gate_b[i], yt_c, att_c, hp, mp, wo, mods, tpb=tpb_ctx, base=nb)
            h_ctx = _ffn(h_ctx, norm2_g[i], mods, w_ffg, w_ffu, w_ffo, final_norm_g,
                         tpb=tpb_ctx, base=nb, final=False)

    return h_lat.reshape(nb, seq, d)
```

```python
import functools
import math

import numpy as np
import jax
import jax.numpy as jnp
from jax import lax
from jax.experimental import pallas as pl
from jax.experimental.pallas import tpu as pltpu

F32 = jnp.float32
BF16 = jnp.bfloat16
HIGHEST = lax.Precision.HIGHEST

D_MODEL = 1024
GRID_W = 64
NORM_EPS = 1e-6

HYENA_W = 512
HYENA_ORDER = 2
N_BANDS = 16
FILTER_HIDDEN = 64
HYENA_TARGET = 1e-2
HYENA_MAX_DECAY = math.log(HYENA_TARGET) / 0.3
HYENA_MIN_DECAY = math.log(HYENA_TARGET) / 1.5

MLA_HEADS = 16
QK_NOPE = 64
QK_ROPE = 32
V_HEAD = 64
Q_LORA = 384
KV_LORA = 256
ROPE_BASE = 10000.0
AXIS_ROPE = QK_ROPE // 2
ATTN_SCALE = (QK_NOPE + QK_ROPE) ** -0.5
Q_SCALE = ATTN_SCALE * math.log2(math.e)
D_FF = 2816

HY_COLS = (HYENA_ORDER + 1) * HYENA_W
OFF_Q = HY_COLS
OFF_KV = OFF_Q + Q_LORA
OFF_KR = OFF_KV + KV_LORA
OFF_G = OFF_KR + QK_ROPE

HEAD_PAD = 128
ROPE_LANE = QK_NOPE
QKV_COLS = Q_LORA + KV_LORA + 2 * HEAD_PAD
HEADS_PER_STEP = 4
NMOD = 16

FFT_N1 = 64
FFT_N2 = 128
VMEM_LIMIT = 56 * 1024 * 1024


def _cparams(*sem):
    return pltpu.CompilerParams(dimension_semantics=sem, vmem_limit_bytes=VMEM_LIMIT)


def _rms(x, g):
    return x * lax.rsqrt(jnp.mean(x * x, axis=-1, keepdims=True) + NORM_EPS) * g


def _silu(x):
    return x * (1.0 / (1.0 + jnp.exp(-x)))


def _sigmoid(x):
    return 1.0 / (1.0 + jnp.exp(-x))


def _mod_row(mods_ref, row, k):
    return mods_ref[pl.ds(row, 1), k * D_MODEL:(k + 1) * D_MODEL]


def _ada_kernel(c_ref, w_ref, b_ref, o_ref):
    s = _silu(c_ref[...])
    o_ref[...] = jnp.dot(s, w_ref[...], preferred_element_type=F32, precision=HIGHEST) + b_ref[...]


def _ada(cc, ada_w, ada_b):
    depth = ada_w.shape[0]
    tn = D_MODEL
    return pl.pallas_call(
        _ada_kernel,
        grid=(depth, 6 * D_MODEL // tn),
        in_specs=[pl.BlockSpec((NMOD, D_MODEL), lambda l, j: (0, 0)),
                  pl.BlockSpec((None, D_MODEL, tn), lambda l, j: (l, 0, j)),
                  pl.BlockSpec((None, 1, tn), lambda l, j: (l, 0, j))],
        out_specs=pl.BlockSpec((None, NMOD, tn), lambda l, j: (l, 0, j)),
        out_shape=jax.ShapeDtypeStruct((depth, NMOD, 6 * D_MODEL), F32),
        compiler_params=_cparams("arbitrary", "arbitrary"),
        name="ada",
    )(cc, ada_w, ada_b.reshape(depth, 1, 6 * D_MODEL))


def _norm_mod_kernel(h_ref, g_ref, mods_ref, o_ref, *, tpb, base, k_shift, k_scale):
    row = base + pl.program_id(0) // tpb
    y = _rms(h_ref[...], g_ref[...])
    o_ref[...] = (y * (1.0 + _mod_row(mods_ref, row, k_scale)) + _mod_row(mods_ref, row, k_shift)).astype(BF16)


def _norm_mod(h, g, mods, *, tpb, base, k_shift, k_scale, tm=512):
    t = h.shape[0]
    return pl.pallas_call(
        functools.partial(_norm_mod_kernel, tpb=tpb, base=base, k_shift=k_shift, k_scale=k_scale),
        grid=(t // tm,),
        in_specs=[pl.BlockSpec((tm, D_MODEL), lambda i: (i, 0)),
                  pl.BlockSpec((1, D_MODEL), lambda i: (0, 0)),
                  pl.BlockSpec((NMOD, 6 * D_MODEL), lambda i: (0, 0))],
        out_specs=pl.BlockSpec((tm, D_MODEL), lambda i: (i, 0)),
        out_shape=jax.ShapeDtypeStruct((t, D_MODEL), BF16),
        compiler_params=_cparams("parallel"),
        name="norm_mod",
    )(h, g.reshape(1, D_MODEL), mods)


def _mm_kernel(a_ref, w_ref, o_ref):
    o_ref[...] = jnp.dot(a_ref[...], w_ref[...], preferred_element_type=F32).astype(o_ref.dtype)


def _matmul(a, w, out_dtype, tm=512):
    m, k = a.shape
    n = w.shape[1]
    return pl.pallas_call(
        _mm_kernel,
        grid=(m // tm,),
        in_specs=[pl.BlockSpec((tm, k), lambda i: (i, 0)),
                  pl.BlockSpec((k, n), lambda i: (0, 0))],
        out_specs=pl.BlockSpec((tm, n), lambda i: (i, 0)),
        out_shape=jax.ShapeDtypeStruct((m, n), out_dtype),
        compiler_params=_cparams("parallel"),
        name="matmul",
    )(a, w)


def _mm_nt_kernel(w_ref, a_ref, o_ref):
    o_ref[...] = lax.dot_general(w_ref[...], a_ref[...], (((1,), (1,)), ((), ())),
                                 preferred_element_type=F32).astype(o_ref.dtype)


def _matmul_nt(wt, a, out_dtype, tm=512):
    n, k = wt.shape
    m = a.shape[0]
    return pl.pallas_call(
        _mm_nt_kernel,
        grid=(m // tm,),
        in_specs=[pl.BlockSpec((n, k), lambda i: (0, 0)),
                  pl.BlockSpec((tm, k), lambda i: (i, 0))],
        out_specs=pl.BlockSpec((n, tm), lambda i: (0, i)),
        out_shape=jax.ShapeDtypeStruct((n, m), out_dtype),
        compiler_params=_cparams("parallel"),
        name="matmul_nt",
    )(wt, a)


def _mm_hi_kernel(a_ref, w_ref, o_ref):
    o_ref[...] = jnp.dot(a_ref[...], w_ref[...], preferred_element_type=F32, precision=HIGHEST)


def _matmul_hi(a, w, tm):
    m, k = a.shape
    n = w.shape[1]
    return pl.pallas_call(
        _mm_hi_kernel,
        grid=(m // tm,),
        in_specs=[pl.BlockSpec((tm, k), lambda i: (i, 0)),
                  pl.BlockSpec((k, n), lambda i: (0, 0))],
        out_specs=pl.BlockSpec((tm, n), lambda i: (i, 0)),
        out_shape=jax.ShapeDtypeStruct((m, n), F32),
        compiler_params=_cparams("parallel"),
        name="matmul_hi",
    )(a, w)


def _qkv_kernel(u_ref, qg_ref, kvg_ref, wq_ref, wk_ref, wv_ref, ct_ref, st_ref, q_ref, k_ref, v_ref):
    u = u_ref[...]
    nq = _rms(u[:, :Q_LORA], qg_ref[...]).astype(BF16)
    nkv = _rms(u[:, Q_LORA:Q_LORA + KV_LORA], kvg_ref[...]).astype(BF16)
    ka = u[:, Q_LORA + KV_LORA:Q_LORA + KV_LORA + HEAD_PAD]
    kb = u[:, Q_LORA + KV_LORA + HEAD_PAD:]
    ct = ct_ref[...]
    st = st_ref[...]
    kr = ka * ct + kb * st
    qq = jnp.dot(nq, wq_ref[...], preferred_element_type=F32)
    kn = jnp.dot(nkv, wk_ref[...], preferred_element_type=F32)
    nh = MLA_HEADS * HEAD_PAD
    for h in range(MLA_HEADS):
        lo, hi = h * HEAD_PAD, (h + 1) * HEAD_PAD
        qh = qq[:, lo:hi] * ct + qq[:, nh + lo:nh + hi] * st
        q_ref[:, lo:hi] = (qh * Q_SCALE).astype(BF16)
        k_ref[:, lo:hi] = (kn[:, lo:hi] + kr).astype(BF16)
    v_ref[...] = jnp.dot(nkv, wv_ref[...], preferred_element_type=F32).astype(BF16)


def _qkv_up(u, qg, kvg, wq, wk, wv, ctab, stab, *, rope, tm=512):
    t = u.shape[0]
    ntab = ctab.shape[0] // tm
    tab_map = (lambda i: (i % ntab, 0)) if rope else (lambda i: (0, 0))
    nh = MLA_HEADS * HEAD_PAD
    nv = MLA_HEADS * V_HEAD
    const = lambda i: (0, 0)
    return pl.pallas_call(
        _qkv_kernel,
        grid=(t // tm,),
        in_specs=[pl.BlockSpec((tm, QKV_COLS), lambda i: (i, 0)),
                  pl.BlockSpec((1, Q_LORA), const),
                  pl.BlockSpec((1, KV_LORA), const),
                  pl.BlockSpec((Q_LORA, 2 * nh), const),
                  pl.BlockSpec((KV_LORA, nh), const),
                  pl.BlockSpec((KV_LORA, nv), const),
                  pl.BlockSpec((tm, HEAD_PAD), tab_map),
                  pl.BlockSpec((tm, HEAD_PAD), tab_map)],
        out_specs=[pl.BlockSpec((tm, nh), lambda i: (i, 0)),
                   pl.BlockSpec((tm, nh), lambda i: (i, 0)),
                   pl.BlockSpec((tm, nv), lambda i: (i, 0))],
        out_shape=[jax.ShapeDtypeStruct((t, nh), BF16),
                   jax.ShapeDtypeStruct((t, nh), BF16),
                   jax.ShapeDtypeStruct((t, nv), BF16)],
        compiler_params=_cparams("parallel"),
        name="qkv_up",
    )(u, qg, kvg, wq, wk, wv, ctab, stab)


def _attn_kernel(*refs, lks, n_sub, tq):
    n_kv = len(lks)
    q_ref = refs[0]
    k_refs = refs[1:1 + n_kv]
    v_refs = refs[1 + n_kv:1 + 2 * n_kv]
    o_ref = refs[1 + 2 * n_kv]
    nt = (((1,), (1,)), ((), ()))
    vw = HEADS_PER_STEP * V_HEAD
    lane_head = lax.broadcasted_iota(jnp.int32, (tq, vw), 1) // V_HEAD
    units = [(j, h) for j in range(n_sub) for h in range(HEADS_PER_STEP)]

    def scores(j, h):
        q = q_ref[j * tq:(j + 1) * tq, h * HEAD_PAD:(h + 1) * HEAD_PAD]
        return [lax.dot_general(q, k_ref[:, h * HEAD_PAD:(h + 1) * HEAD_PAD], nt, preferred_element_type=F32)
                for k_ref in k_refs]

    s_next = scores(*units[0])
    res = None
    for idx, (j, h) in enumerate(units):
        s = s_next
        if idx + 1 < len(units):
            s_next = scores(*units[idx + 1])
        m = s[0].max(axis=-1, keepdims=True)
        for si in s[1:]:
            m = jnp.maximum(m, si.max(axis=-1, keepdims=True))
        p = [jnp.exp2(si - m) for si in s]
        l = p[0].sum(axis=-1, keepdims=True)
        for pi in p[1:]:
            l = l + pi.sum(axis=-1, keepdims=True)
        o = jnp.dot(p[0].astype(BF16), v_refs[0][...], preferred_element_type=F32)
        for pi, v_ref in zip(p[1:], v_refs[1:]):
            o = o + jnp.dot(pi.astype(BF16), v_ref[...], preferred_element_type=F32)
        o = o / l
        res = o if h == 0 else jnp.where(lane_head == h, o, res)
        if h == HEADS_PER_STEP - 1:
            o_ref[j * tq:(j + 1) * tq, :] = res.astype(BF16)


def _attention(q, kv_sets, *, nb, tq=256, n_sub=2):
    t = q.shape[0]
    lq = t // nb
    n_sub = min(n_sub, lq // tq)
    tb = tq * n_sub
    nq = lq // tb
    groups = MLA_HEADS // HEADS_PER_STEP
    qw = HEADS_PER_STEP * HEAD_PAD
    vw = HEADS_PER_STEP * V_HEAD
    in_specs = [pl.BlockSpec((tb, qw), lambda b, g, i: (b * nq + i, g))]
    k_specs, v_specs, ks, vs = [], [], [], []
    for k, v, lk in kv_sets:
        k_specs.append(pl.BlockSpec((lk, qw), lambda b, g, i: (b, g)))
        v_specs.append(pl.BlockSpec((lk, vw), lambda b, g, i: (b, g)))
        ks.append(k)
        vs.append(v)
    lks = tuple(lk for _, _, lk in kv_sets)
    return pl.pallas_call(
        functools.partial(_attn_kernel, lks=lks, n_sub=n_sub, tq=tq),
        grid=(nb, groups, nq),
        in_specs=in_specs + k_specs + v_specs,
        out_specs=pl.BlockSpec((tb, vw), lambda b, g, i: (b * nq + i, g)),
        out_shape=jax.ShapeDtypeStruct((t, MLA_HEADS * V_HEAD), BF16),
        compiler_params=_cparams("parallel", "parallel", "arbitrary"),
        name="attention",
    )(q, *ks, *vs)


def _filt_mlp_kernel(z_ref, w1_ref, b1_ref, w2_ref, b2_ref, f0_ref, f1_ref, o_ref):
    h = jnp.dot(w1_ref[...], z_ref[...], preferred_element_type=F32, precision=HIGHEST) + b1_ref[...]
    h = jnp.sin(f0_ref[...] * h)
    h = jnp.dot(w2_ref[...], h, preferred_element_type=F32, precision=HIGHEST) + b2_ref[...]
    o_ref[...] = jnp.sin(f1_ref[...] * h)


def _filt_mlp(zt, w1, b1, w2, b2, freq):
    kz, n = zt.shape
    w1t = jnp.zeros((FILTER_HIDDEN, kz), F32).at[:, :w1.shape[0]].set(w1.T)
    col = lambda a: a.reshape(FILTER_HIDDEN, 1)
    args = (zt, w1t, col(b1), w2.T, col(b2), col(freq[0]), col(freq[1]))
    return pl.pallas_call(
        _filt_mlp_kernel,
        grid=(1,),
        in_specs=[pl.BlockSpec(a.shape, lambda i: (0, 0)) for a in args],
        out_specs=pl.BlockSpec((FILTER_HIDDEN, n), lambda i: (0, 0)),
        out_shape=jax.ShapeDtypeStruct((FILTER_HIDDEN, n), F32),
        compiler_params=_cparams("arbitrary"),
        name="filt_mlp",
    )(*args)


def _filt_taps_kernel(h_ref, w3_ref, d_ref, t_ref, o_ref, *, seq):
    n = h_ref.shape[1]
    pos = lax.broadcasted_iota(jnp.int32, (1, n), 1)
    win = jnp.exp(-t_ref[...] * d_ref[...])
    h = h_ref[...]
    for o in range(HYENA_ORDER):
        fwd = jnp.dot(w3_ref[0, o], h, preferred_element_type=F32, precision=HIGHEST)
        bwd = jnp.dot(w3_ref[1, o], h, preferred_element_type=F32, precision=HIGHEST)
        full = jnp.where(pos < seq, fwd, jnp.where(pos > seq, bwd, 0.0)) * win
        o_ref[o] = full / jnp.sum(jnp.abs(full), axis=1, keepdims=True)


def _filt_taps(h2t, w3, dabs, tvec, seq, cf=64):
    n = h2t.shape[1]
    w3t = w3.T.reshape(2, HYENA_ORDER, HYENA_W, FILTER_HIDDEN)
    return pl.pallas_call(
        functools.partial(_filt_taps_kernel, seq=seq),
        grid=(HYENA_W // cf,),
        in_specs=[pl.BlockSpec((FILTER_HIDDEN, n), lambda c: (0, 0)),
                  pl.BlockSpec((2, HYENA_ORDER, cf, FILTER_HIDDEN), lambda c: (0, 0, c, 0)),
                  pl.BlockSpec((cf, 1), lambda c: (c, 0)),
                  pl.BlockSpec((1, n), lambda c: (0, 0))],
        out_specs=pl.BlockSpec((HYENA_ORDER, cf, n), lambda c: (0, c, 0)),
        out_shape=jax.ShapeDtypeStruct((HYENA_ORDER, HYENA_W, n), F32),
        compiler_params=_cparams("parallel"),
        name="filt_taps",
    )(h2t, w3t, dabs, tvec)


def _filt_spec_kernel(x_ref, f1_ref, tc_ref, ts_ref, f2_ref, o_ref):
    ns = x_ref.shape[0]
    x = jnp.concatenate([x_ref[s] for s in range(ns)], axis=1)
    a = jnp.dot(f1_ref[...], x, preferred_element_type=F32, precision=HIGHEST)
    tc = tc_ref[...]
    ts = ts_ref[...]
    rows = []
    for s in range(ns):
        ar = a[:FFT_N1, s * FFT_N2:(s + 1) * FFT_N2]
        ai = a[FFT_N1:, s * FFT_N2:(s + 1) * FFT_N2]
        rows.append(jnp.concatenate([ar * tc + ai * ts, ai * tc - ar * ts], axis=1))
    b = jnp.concatenate(rows, axis=0)
    z = jnp.dot(b, f2_ref[...], preferred_element_type=F32, precision=HIGHEST)
    o_ref[...] = z.reshape(ns, FFT_N1, 2 * FFT_N2)


def _filt_spec(hfull, consts, ns=16):
    r = hfull.shape[0]
    x = hfull.reshape(r, FFT_N1, FFT_N2)
    f1h, tc, ts, f2 = consts["f1h"], consts["tc"], consts["ts"], consts["f2_f32"]
    const = lambda i: (0, 0)
    return pl.pallas_call(
        _filt_spec_kernel,
        grid=(r // ns,),
        in_specs=[pl.BlockSpec((ns, FFT_N1, FFT_N2), lambda i: (i, 0, 0)),
                  pl.BlockSpec(f1h.shape, const), pl.BlockSpec(tc.shape, const),
                  pl.BlockSpec(ts.shape, const), pl.BlockSpec(f2.shape, const)],
        out_specs=pl.BlockSpec((ns, FFT_N1, 2 * FFT_N2), lambda i: (i, 0, 0)),
        out_shape=jax.ShapeDtypeStruct((r, FFT_N1, 2 * FFT_N2), F32),
        compiler_params=_cparams("parallel"),
        name="filt_spec",
    )(x, f1h, tc, ts, f2)


def _short_conv_tile(x, w, b):
    nrow = x.shape[1]
    lane = lax.broadcasted_iota(jnp.int32, x.shape, 2)
    row = lax.broadcasted_iota(jnp.int32, x.shape, 1)
    r1 = pltpu.roll(x, 1, 2)
    r2 = pltpu.roll(r1, 1, 1)
    prev = jnp.where(lane == 0, jnp.where(row == 0, 0.0, r2), r1)
    l1 = pltpu.roll(x, FFT_N2 - 1, 2)
    l2 = pltpu.roll(l1, nrow - 1, 1)
    nxt = jnp.where(lane == FFT_N2 - 1, jnp.where(row == nrow - 1, 0.0, l2), l1)
    return prev * w[0] + x * w[1] + nxt * w[2] + b


def _long_conv_pair(vr, vi, h, f1, f1i, f2, f2i, tc, ts):
    ns = vr.shape[0]
    half = FFT_N2
    x = jnp.concatenate([jnp.concatenate([vr[s] for s in range(ns)], axis=1),
                         jnp.concatenate([vi[s] for s in range(ns)], axis=1)], axis=0).astype(BF16)
    a = jnp.dot(f1, x, preferred_element_type=F32)
    rows = []
    for s in range(ns):
        ar = a[:FFT_N1, s * half:(s + 1) * half]
        ai = a[FFT_N1:, s * half:(s + 1) * half]
        rows.append(jnp.concatenate([ar * tc + ai * ts, ai * tc - ar * ts], axis=1))
    b = jnp.concatenate(rows, axis=0).astype(BF16)
    z = jnp.dot(b, f2, preferred_element_type=F32).reshape(ns, FFT_N1, 2 * half)
    zr, zi = z[:, :, :half], z[:, :, half:]
    hr, hi = h[:, :, :half], h[:, :, half:]
    w = jnp.concatenate([zr * hr - zi * hi, zr * hi + zi * hr], axis=2)
    w = w.reshape(ns * FFT_N1, 2 * half).astype(BF16)
    c = jnp.dot(w, f2i, preferred_element_type=F32).reshape(ns, FFT_N1, 2 * half)
    cr, ci = c[:, :, :half], c[:, :, half:]
    dr = cr * tc - ci * ts
    di = cr * ts + ci * tc
    d = jnp.concatenate([jnp.concatenate([dr[s] for s in range(ns)], axis=1),
                         jnp.concatenate([di[s] for s in range(ns)], axis=1)], axis=0).astype(BF16)
    y = jnp.dot(f1i, d, preferred_element_type=F32)
    nrow = y.shape[0] // 2
    yr = jnp.stack([y[:nrow, s * half:(s + 1) * half] for s in range(ns)], axis=0)
    yi = jnp.stack([y[nrow:, s * half:(s + 1) * half] for s in range(ns)], axis=0)
    return yr, yi


def _hyena_kernel(x0_ref, x1_ref, v_ref, cw_ref, cb_ref, hb_ref, h_ref,
                  f1_ref, f1i_ref, f2_ref, f2i_ref, tc_ref, ts_ref, o_ref):
    consts = (f1_ref[...], f1i_ref[...], f2_ref[...], f2i_ref[...], tc_ref[...], ts_ref[...])
    g0, g1, v = [], [], []
    for half in range(2):
        g0.append(_short_conv_tile(x0_ref[:, half], cw_ref[0], cb_ref[0]))
        g1.append(_short_conv_tile(x1_ref[:, half], cw_ref[1], cb_ref[1]))
        v.append(_short_conv_tile(v_ref[:, half], cw_ref[2], cb_ref[2]))
    gates = (g0, g1)
    for o in range(HYENA_ORDER):
        yr, yi = _long_conv_pair(v[0], v[1], h_ref[o], *consts)
        v = [gates[o][0] * (yr + hb_ref[o] * v[0]), gates[o][1] * (yi + hb_ref[o] * v[1])]
    o_ref[:, 0] = v[0].astype(BF16)
    o_ref[:, 1] = v[1].astype(BF16)


def _hyena_latent(ut, cw, cb, hb, hspec, consts, *, nb, ct=32):
    seq = ut.shape[1] // nb
    nrow = seq // FFT_N2
    npair = nb // 2
    nct = HYENA_W // ct
    u5 = ut.reshape(3 * HYENA_W, 2, npair, nrow, FFT_N2)
    part = lambda p: pl.BlockSpec((ct, 2, None, nrow, FFT_N2), lambda c, j: (p * nct + c, 0, j, 0, 0))
    const2 = lambda c, j: (0, 0)
    cnames = ("f1", "f1i", "f2", "f2i", "tc", "ts")
    out = pl.pallas_call(
        _hyena_kernel,
        grid=(nct, npair),
        in_specs=[part(0), part(1), part(2),
                  pl.BlockSpec((3, 3, ct, 1, 1), lambda c, j: (0, 0, c, 0, 0)),
                  pl.BlockSpec((3, ct, 1, 1), lambda c, j: (0, c, 0, 0)),
                  pl.BlockSpec((HYENA_ORDER, ct, 1, 1), lambda c, j: (0, c, 0, 0)),
                  pl.BlockSpec((HYENA_ORDER, ct, FFT_N1, 2 * FFT_N2), lambda c, j: (0, c, 0, 0))]
                 + [pl.BlockSpec(consts[k].shape, const2) for k in cnames],
        out_specs=pl.BlockSpec((ct, 2, None, nrow, FFT_N2), lambda c, j: (c, 0, j, 0, 0)),
        out_shape=jax.ShapeDtypeStruct((HYENA_W, 2, npair, nrow, FFT_N2), BF16),
        compiler_params=_cparams("parallel", "arbitrary"),
        name="hyena",
    )(u5, u5, u5, cw, cb, hb, hspec, *[consts[k] for k in cnames])
    return out.reshape(HYENA_W, nb * seq)


def _hyena_ctx_kernel(x0_ref, x1_ref, v_ref, cw_ref, cb_ref, hb_ref, h_ref, cf_ref, ci_ref, o_ref, *, nb):
    seq = x0_ref.shape[1] // nb
    cc = x0_ref.shape[0]
    half = h_ref.shape[2] // 2

    def rows(ref, p):
        x = jnp.concatenate([ref[:, b * seq:(b + 1) * seq] for b in range(nb)], axis=0)
        lane = lax.broadcasted_iota(jnp.int32, x.shape, 1)
        prev = jnp.where(lane == 0, 0.0, pltpu.roll(x, 1, 1))
        nxt = jnp.where(lane == seq - 1, 0.0, pltpu.roll(x, seq - 1, 1))
        w = [jnp.concatenate([cw_ref[p, j]] * nb, axis=0) for j in range(3)]
        return prev * w[0] + x * w[1] + nxt * w[2] + jnp.concatenate([cb_ref[p]] * nb, axis=0)

    g = (rows(x0_ref, 0), rows(x1_ref, 1))
    v = rows(v_ref, 2)
    for o in range(HYENA_ORDER):
        spec = jnp.dot(v.astype(BF16), cf_ref[...], preferred_element_type=F32)
        xr, xi = spec[:, :half], spec[:, half:]
        hh = jnp.concatenate([h_ref[o]] * nb, axis=0)
        hr, hi = hh[:, :half], hh[:, half:]
        y = jnp.concatenate([xr * hr - xi * hi, xr * hi + xi * hr], axis=1).astype(BF16)
        conv = jnp.dot(y, ci_ref[...], preferred_element_type=F32)
        v = g[o] * (conv + jnp.concatenate([hb_ref[o]] * nb, axis=0) * v)
    for b in range(nb):
        o_ref[:, b * seq:(b + 1) * seq] = v[b * cc:(b + 1) * cc].astype(BF16)


def _hyena_ctx(ut, cw, cb, hb, hspec, cfwd, cinv, *, nb, cc=64):
    n = ut.shape[1]
    ncc = HYENA_W // cc
    part = lambda p: pl.BlockSpec((cc, n), lambda c: (p * ncc + c, 0))
    return pl.pallas_call(
        functools.partial(_hyena_ctx_kernel, nb=nb),
        grid=(ncc,),
        in_specs=[part(0), part(1), part(2),
                  pl.BlockSpec((3, 3, cc, 1), lambda c: (0, 0, c, 0)),
                  pl.BlockSpec((3, cc, 1), lambda c: (0, c, 0)),
                  pl.BlockSpec((HYENA_ORDER, cc, 1), lambda c: (0, c, 0)),
                  pl.BlockSpec((HYENA_ORDER, cc, hspec.shape[2]), lambda c: (0, c, 0)),
                  pl.BlockSpec(cfwd.shape, lambda c: (0, 0)),
                  pl.BlockSpec(cinv.shape, lambda c: (0, 0))],
        out_specs=pl.BlockSpec((cc, n), lambda c: (c, 0)),
        out_shape=jax.ShapeDtypeStruct((HYENA_W, n), BF16),
        compiler_params=_cparams("parallel"),
        name="hyena_ctx",
    )(ut, ut, ut, cw, cb, hb, hspec, cfwd, cinv)


def _merge_kernel(h_ref, ug_ref, gb_ref, yt_ref, att_ref, hp_ref, mp_ref, wo_ref, mods_ref, o_ref, *, tpb, base):
    row = base + pl.program_id(0) // tpb
    g = _sigmoid(ug_ref[...] + gb_ref[...])
    yh = lax.dot_general(yt_ref[...], hp_ref[...], (((0,), (0,)), ((), ())), preferred_element_type=F32)
    ya = jnp.dot(att_ref[...], mp_ref[...], preferred_element_type=F32)
    mix = (g[:, :D_MODEL] * yh + g[:, D_MODEL:] * ya).astype(BF16)
    out = jnp.dot(mix, wo_ref[...], preferred_element_type=F32)
    o_ref[...] = h_ref[...] + _mod_row(mods_ref, row, 2) * out


def _merge(h, ug, gate_b, yt, att, hy_proj, mla_proj, w_out, mods, *, tpb, base, tm=512):
    t = h.shape[0]
    const = lambda i: (0, 0)
    return pl.pallas_call(
        functools.partial(_merge_kernel, tpb=tpb, base=base),
        grid=(t // tm,),
        in_specs=[pl.BlockSpec((tm, D_MODEL), lambda i: (i, 0)),
                  pl.BlockSpec((tm, 2 * D_MODEL), lambda i: (i, 0)),
                  pl.BlockSpec((1, 2 * D_MODEL), const),
                  pl.BlockSpec((HYENA_W, tm), lambda i: (0, i)),
                  pl.BlockSpec((tm, MLA_HEADS * V_HEAD), lambda i: (i, 0)),
                  pl.BlockSpec(hy_proj.shape, const),
                  pl.BlockSpec(mla_proj.shape, const),
                  pl.BlockSpec(w_out.shape, const),
                  pl.BlockSpec((NMOD, 6 * D_MODEL), const)],
        out_specs=pl.BlockSpec((tm, D_MODEL), lambda i: (i, 0)),
        out_shape=jax.ShapeDtypeStruct((t, D_MODEL), F32),
        compiler_params=_cparams("parallel"),
        name="merge",
    )(h, ug, gate_b.reshape(1, -1), yt, att, hy_proj, mla_proj, w_out, mods)


def _ffn_kernel(h_ref, g_ref, mods_ref, wg_ref, wu_ref, wo_ref, fg_ref, o_ref, f_scr, *, tpb, base, final):
    j = pl.program_id(1)
    row = base + pl.program_id(0) // tpb

    @pl.when(j == 0)
    def _():
        h = h_ref[...]
        f = _rms(h, g_ref[...]) * (1.0 + _mod_row(mods_ref, row, 4)) + _mod_row(mods_ref, row, 3)
        f_scr[...] = f.astype(BF16)
        o_ref[...] = h

    f = f_scr[...]
    act = _silu(jnp.dot(f, wg_ref[...], preferred_element_type=F32)) * jnp.dot(f, wu_ref[...], preferred_element_type=F32)
    y = jnp.dot(act.astype(BF16), wo_ref[...], preferred_element_type=F32)
    o_ref[...] += _mod_row(mods_ref, row, 5) * y

    if final:
        @pl.when(j == pl.num_programs(1) - 1)
        def _():
            o_ref[...] = _rms(o_ref[...], fg_ref[...])


def _ffn(h, g, mods, w_gate, w_up, w_down, final_g, *, tpb, base, final, tm=512, nchunk=2):
    t = h.shape[0]
    fc = D_FF // nchunk
    const = lambda i, j: (0, 0)
    return pl.pallas_call(
        functools.partial(_ffn_kernel, tpb=tpb, base=base, final=final),
        grid=(t // tm, nchunk),
        in_specs=[pl.BlockSpec((tm, D_MODEL), lambda i, j: (i, 0)),
                  pl.BlockSpec((1, D_MODEL), const),
                  pl.BlockSpec((NMOD, 6 * D_MODEL), const),
                  pl.BlockSpec((D_MODEL, fc), lambda i, j: (0, j)),
                  pl.BlockSpec((D_MODEL, fc), lambda i, j: (0, j)),
                  pl.BlockSpec((fc, D_MODEL), lambda i, j: (j, 0)),
                  pl.BlockSpec((1, D_MODEL), const)],
        out_specs=pl.BlockSpec((tm, D_MODEL), lambda i, j: (i, 0)),
        out_shape=jax.ShapeDtypeStruct((t, D_MODEL), F32),
        scratch_shapes=[pltpu.VMEM((tm, D_MODEL), BF16)],
        compiler_params=_cparams("parallel", "arbitrary"),
        name="ffn",
    )(h, g.reshape(1, -1), mods, w_gate, w_up, w_down, final_g.reshape(1, -1))


def _dft_consts():
    n = FFT_N1 * FFT_N2
    k1 = np.arange(FFT_N1)
    th = 2 * np.pi * np.outer(k1, np.arange(FFT_N1)) / FFT_N1
    c, s = np.cos(th), np.sin(th)
    hn = FFT_N1 // 2
    f1 = np.block([[c[:, :hn], s[:, :hn]], [-s[:, :hn], c[:, :hn]]])
    f1h = np.concatenate([c, -s], axis=0)
    f1i = np.block([[c[:, :hn].T, -s[:, :hn].T], [s[:, :hn].T, c[:, :hn].T]]) / n
    ph = 2 * np.pi * np.outer(k1, np.arange(FFT_N2)) / n
    th2 = 2 * np.pi * np.outer(np.arange(FFT_N2), np.arange(FFT_N2)) / FFT_N2
    c2, s2 = np.cos(th2), np.sin(th2)
    f2 = np.block([[c2, -s2], [s2, c2]])
    f2i = np.block([[c2, s2], [-s2, c2]])
    return {"f1": jnp.asarray(f1, BF16), "f1h": jnp.asarray(f1h, F32), "f1i": jnp.asarray(f1i, BF16),
            "f2": jnp.asarray(f2, BF16), "f2i": jnp.asarray(f2i, BF16), "f2_f32": jnp.asarray(f2, F32),
            "tc": jnp.asarray(np.cos(ph), F32), "ts": jnp.asarray(np.sin(ph), F32)}


def _dense_dft(seq):
    n = 2 * seq
    th = 2 * np.pi * np.outer(np.arange(n), np.arange(n)) / n
    c, s = np.cos(th), np.sin(th)
    full = np.concatenate([c, -s], axis=1)
    inv = np.concatenate([c[:, :seq], -s[:, :seq]], axis=0) / n
    return jnp.asarray(full, F32), jnp.asarray(full[:seq], BF16), jnp.asarray(inv, BF16)


def _filter_positions(seq):
    n = np.arange(2 * seq)
    lag = np.minimum(np.where(n < seq, n, 2 * seq - n), seq - 1)
    return jnp.linspace(0.0, 1.0, seq, dtype=F32)[lag]


def _rope_tables(seq):
    rows = seq // GRID_W
    row = jnp.repeat(jnp.arange(rows, dtype=F32), GRID_W)
    col = jnp.tile(jnp.arange(GRID_W, dtype=F32), rows)
    inv = ROPE_BASE ** (-jnp.arange(0, AXIS_ROPE, 2, dtype=F32) / AXIS_ROPE)
    ang = jnp.concatenate([row[:, None] * inv, col[:, None] * inv], axis=-1)
    cos, sin = jnp.cos(ang), jnp.sin(ang)
    ones = jnp.ones((seq, QK_NOPE), F32)
    pad = HEAD_PAD - QK_NOPE - QK_ROPE
    ctab = jnp.concatenate([ones, cos, cos, jnp.ones((seq, pad), F32)], axis=1)
    stab = jnp.concatenate([0 * ones, sin, sin, jnp.zeros((seq, pad), F32)], axis=1)
    return ctab, stab


def _rot_cols(w):
    half = QK_ROPE // 2
    return jnp.concatenate([-w[..., half:], w[..., :half]], axis=-1)


def _pad_heads(nope, rope):
    k = nope.shape[0]
    z = jnp.zeros((k, MLA_HEADS, HEAD_PAD - QK_NOPE - QK_ROPE), nope.dtype)
    return jnp.concatenate([nope, rope, z], axis=-1).reshape(k, MLA_HEADS * HEAD_PAD)


def kernel(x, c, ctx, c_ctx, ada_w, ada_b, norm1_g, norm2_g, w_in, gate_b, hy_conv_w, hy_conv_b,
           hy_filt_w1, hy_filt_b1, hy_filt_w2, hy_filt_b2, hy_filt_w3, hy_filt_freq, hy_bias, hy_proj,
           q_norm_g, kv_norm_g, w_uq, w_ukv, mla_proj, w_out, ffn_w_in, ffn_w_out, final_norm_g):
    nb, seq, d = x.shape
    lc = ctx.shape[1]
    depth = ada_w.shape[0]
    assert d == D_MODEL and seq == FFT_N1 * FFT_N2 // 2 and nb % 2 == 0 and nb < NMOD
    tm = 512
    tpb_lat = seq // tm
    tpb_ctx = 1 << 30

    cc = jnp.zeros((NMOD, d), F32).at[:nb].set(c).at[nb].set(c_ctx)
    mods_all = _ada(cc, ada_w, ada_b)

    consts = _dft_consts()
    cfull_c, cfwd_c, cinv_c = _dense_dft(lc)
    ctab, stab = _rope_tables(seq)
    one_tab = jnp.ones((tm, HEAD_PAD), F32)
    zero_tab = jnp.zeros((tm, HEAD_PAD), F32)
    dabs = jnp.abs(jnp.linspace(HYENA_MIN_DECAY, HYENA_MAX_DECAY, HYENA_W, dtype=F32)).reshape(HYENA_W, 1)
    bands = jnp.arange(1, N_BANDS + 1, dtype=F32)

    def pos_features(length):
        t = _filter_positions(length)
        ang = 2 * math.pi * bands[:, None] * t[None, :]
        zt = jnp.concatenate([t[None, :], jnp.cos(ang), jnp.sin(ang),
                              jnp.zeros((7, 2 * length), F32)], axis=0)
        return zt, t.reshape(1, -1)

    zt_lat, tv_lat = pos_features(seq)
    zt_ctx, tv_ctx = pos_features(lc)

    h_lat = x.reshape(nb * seq, d)
    h_ctx = ctx.reshape(nb * lc, d)

    for i in range(depth):
        last = i == depth - 1
        mods = mods_all[i]

        wi = w_in[i]
        w_hy_t = wi[:, :OFF_Q].T.astype(BF16)
        kr_w = wi[:, OFF_KR:OFF_G]
        zl = jnp.zeros((d, ROPE_LANE), F32)
        zr = jnp.zeros((d, HEAD_PAD - ROPE_LANE - QK_ROPE), F32)
        w_qkv = jnp.concatenate([wi[:, OFF_Q:OFF_KR], zl, kr_w, zr, zl, _rot_cols(kr_w), zr], axis=1).astype(BF16)
        w_gate_in = wi[:, OFF_G:].astype(BF16)
        uq = w_uq[i].reshape(Q_LORA, MLA_HEADS, QK_NOPE + QK_ROPE)
        wq = jnp.concatenate([_pad_heads(uq[..., :QK_NOPE], uq[..., QK_NOPE:]),
                              _pad_heads(0 * uq[..., :QK_NOPE], _rot_cols(uq[..., QK_NOPE:]))], axis=1).astype(BF16)
        ukv = w_ukv[i].reshape(KV_LORA, MLA_HEADS, QK_NOPE + V_HEAD)
        wk = _pad_heads(ukv[..., :QK_NOPE], jnp.zeros((KV_LORA, MLA_HEADS, QK_ROPE), F32)).astype(BF16)
        wv = ukv[..., QK_NOPE:].reshape(KV_LORA, MLA_HEADS * V_HEAD).astype(BF16)
        qg = q_norm_g[i].reshape(1, -1)
        kvg = kv_norm_g[i].reshape(1, -1)
        hp, mp, wo = hy_proj[i].astype(BF16), mla_proj[i].astype(BF16), w_out[i].astype(BF16)
        w_ffg = ffn_w_in[i][:, :D_FF].astype(BF16)
        w_ffu = ffn_w_in[i][:, D_FF:].astype(BF16)
        w_ffo = ffn_w_out[i].astype(BF16)
        cw = jnp.transpose(hy_conv_w[i].reshape(3, 3, HYENA_W), (1, 0, 2))
        cb = hy_conv_b[i].reshape(3, HYENA_W)
        hb = hy_bias[i]

        filt = (hy_filt_w1[i], hy_filt_b1[i], hy_filt_w2[i], hy_filt_b2[i], hy_filt_freq[i])
        hfull = _filt_taps(_filt_mlp(zt_lat, *filt), hy_filt_w3[i], dabs, tv_lat, seq)
        hspec = _filt_spec(hfull.reshape(HYENA_ORDER * HYENA_W, 2 * seq), consts)
        hspec = hspec.reshape(HYENA_ORDER, HYENA_W, FFT_N1, 2 * FFT_N2)

        a_l = _norm_mod(h_lat, norm1_g[i], mods, tpb=tpb_lat, base=0, k_shift=0, k_scale=1)
        a_c = _norm_mod(h_ctx, norm1_g[i], mods, tpb=tpb_ctx, base=nb, k_shift=0, k_scale=1)
        q_l, k_l, v_l = _qkv_up(_matmul(a_l, w_qkv, F32), qg, kvg, wq, wk, wv, ctab, stab, rope=True)
        q_c, k_c, v_c = _qkv_up(_matmul(a_c, w_qkv, F32), qg, kvg, wq, wk, wv, one_tab, zero_tab, rope=False)
        att_l = _attention(q_l, [(k_l, v_l, seq), (k_c, v_c, lc)], nb=nb)
        ut_l = _matmul_nt(w_hy_t, a_l, F32)
        yt_l = _hyena_latent(ut_l, cw.reshape(3, 3, HYENA_W, 1, 1), cb.reshape(3, HYENA_W, 1, 1),
                             hb.reshape(HYENA_ORDER, HYENA_W, 1, 1), hspec, consts, nb=nb)
        ug_l = _matmul(a_l, w_gate_in, F32)
        h_lat = _merge(h_lat, ug_l, gate_b[i], yt_l, att_l, hp, mp, wo, mods, tpb=tpb_lat, base=0)

        h_lat = _ffn(h_lat, norm2_g[i], mods, w_ffg, w_ffu, w_ffo, final_norm_g,
                     tpb=tpb_lat, base=0, final=last)

        if not last:
            hfull_c = _filt_taps(_filt_mlp(zt_ctx, *filt), hy_filt_w3[i], dabs, tv_ctx, lc)
            hspec_c = _matmul_hi(hfull_c.reshape(HYENA_ORDER * HYENA_W, 2 * lc), cfull_c, tm=256)
            hspec_c = hspec_c.reshape(HYENA_ORDER, HYENA_W, 4 * lc)
            att_c = _attention(q_c, [(k_c, v_c, lc)], nb=nb)
            ut_c = _matmul_nt(w_hy_t, a_c, F32)
            yt_c = _hyena_ctx(ut_c, cw.reshape(3, 3, HYENA_W, 1), cb.reshape(3, HYENA_W, 1),
                              hb.reshape(HYENA_ORDER, HYENA_W, 1), hspec_c, cfwd_c, cinv_c, nb=nb)
            ug_c = _matmul(a_c, w_gate_in, F32)
            h_ctx = _merge(h_ctx, ug_c, gate_b[i], yt_c, att_c, hp, mp, wo, mods, tpb=tpb_ctx, base=nb)
            h_ctx = _ffn(h_ctx, norm2_g[i], mods, w_ffg, w_ffu, w_ffo, final_norm_g,
                         tpb=tpb_ctx, base=nb, final=False)

    return h_lat.reshape(nb, seq, d)
```

```python
import functools
import math

import numpy as np
import jax
import jax.numpy as jnp
from jax import lax
from jax.experimental import pallas as pl
from jax.experimental.pallas import tpu as pltpu

F32 = jnp.float32
BF16 = jnp.bfloat16
HIGHEST = lax.Precision.HIGHEST

D_MODEL = 1024
GRID_W = 64
NORM_EPS = 1e-6

HYENA_W = 512
HYENA_ORDER = 2
N_BANDS = 16
FILTER_HIDDEN = 64
HYENA_TARGET = 1e-2
HYENA_MAX_DECAY = math.log(HYENA_TARGET) / 0.3
HYENA_MIN_DECAY = math.log(HYENA_TARGET) / 1.5

MLA_HEADS = 16
QK_NOPE = 64
QK_ROPE = 32
V_HEAD = 64
Q_LORA = 384
KV_LORA = 256
ROPE_BASE = 10000.0
AXIS_ROPE = QK_ROPE // 2
ATTN_SCALE = (QK_NOPE + QK_ROPE) ** -0.5
Q_SCALE = ATTN_SCALE * math.log2(math.e)
D_FF = 2816

HY_COLS = (HYENA_ORDER + 1) * HYENA_W
OFF_Q = HY_COLS
OFF_KV = OFF_Q + Q_LORA
OFF_KR = OFF_KV + KV_LORA
OFF_G = OFF_KR + QK_ROPE

HEAD_PAD = 128
ROPE_LANE = QK_NOPE
QKV_COLS = Q_LORA + KV_LORA + 2 * HEAD_PAD
HEADS_PER_STEP = 4
KEY_SPLIT = 1024
NMOD = 16

FFT_N1 = 64
FFT_N2 = 128
VMEM_LIMIT = 56 * 1024 * 1024


def _cparams(*sem):
    return pltpu.CompilerParams(dimension_semantics=sem, vmem_limit_bytes=VMEM_LIMIT)


def _rms(x, g):
    return x * lax.rsqrt(jnp.mean(x * x, axis=-1, keepdims=True) + NORM_EPS) * g


def _silu(x):
    return x * (1.0 / (1.0 + jnp.exp(-x)))


def _sigmoid(x):
    return 1.0 / (1.0 + jnp.exp(-x))


def _mod_row(mods_ref, row, k):
    return mods_ref[pl.ds(row, 1), k * D_MODEL:(k + 1) * D_MODEL]


def _ada_kernel(c_ref, w_ref, b_ref, o_ref):
    s = _silu(c_ref[...])
    o_ref[...] = jnp.dot(s, w_ref[...], preferred_element_type=F32, precision=HIGHEST) + b_ref[...]


def _ada(cc, ada_w, ada_b):
    depth = ada_w.shape[0]
    tn = D_MODEL
    return pl.pallas_call(
        _ada_kernel,
        grid=(depth, 6 * D_MODEL // tn),
        in_specs=[pl.BlockSpec((NMOD, D_MODEL), lambda l, j: (0, 0)),
                  pl.BlockSpec((None, D_MODEL, tn), lambda l, j: (l, 0, j)),
                  pl.BlockSpec((None, 1, tn), lambda l, j: (l, 0, j))],
        out_specs=pl.BlockSpec((None, NMOD, tn), lambda l, j: (l, 0, j)),
        out_shape=jax.ShapeDtypeStruct((depth, NMOD, 6 * D_MODEL), F32),
        compiler_params=_cparams("arbitrary", "arbitrary"),
        name="ada",
    )(cc, ada_w, ada_b.reshape(depth, 1, 6 * D_MODEL))


def _norm_mod_kernel(h_ref, g_ref, mods_ref, o_ref, *, tpb, base, k_shift, k_scale):
    row = base + pl.program_id(0) // tpb
    y = _rms(h_ref[...], g_ref[...])
    o_ref[...] = (y * (1.0 + _mod_row(mods_ref, row, k_scale)) + _mod_row(mods_ref, row, k_shift)).astype(BF16)


def _norm_mod(h, g, mods, *, tpb, base, k_shift, k_scale, tm=512):
    t = h.shape[0]
    return pl.pallas_call(
        functools.partial(_norm_mod_kernel, tpb=tpb, base=base, k_shift=k_shift, k_scale=k_scale),
        grid=(t // tm,),
        in_specs=[pl.BlockSpec((tm, D_MODEL), lambda i: (i, 0)),
                  pl.BlockSpec((1, D_MODEL), lambda i: (0, 0)),
                  pl.BlockSpec((NMOD, 6 * D_MODEL), lambda i: (0, 0))],
        out_specs=pl.BlockSpec((tm, D_MODEL), lambda i: (i, 0)),
        out_shape=jax.ShapeDtypeStruct((t, D_MODEL), BF16),
        compiler_params=_cparams("parallel"),
        name="norm_mod",
    )(h, g.reshape(1, D_MODEL), mods)


def _mm_kernel(a_ref, w_ref, o_ref):
    o_ref[...] = jnp.dot(a_ref[...], w_ref[...], preferred_element_type=F32).astype(o_ref.dtype)


def _matmul(a, w, out_dtype, tm=512):
    m, k = a.shape
    n = w.shape[1]
    return pl.pallas_call(
        _mm_kernel,
        grid=(m // tm,),
        in_specs=[pl.BlockSpec((tm, k), lambda i: (i, 0)),
                  pl.BlockSpec((k, n), lambda i: (0, 0))],
        out_specs=pl.BlockSpec((tm, n), lambda i: (i, 0)),
        out_shape=jax.ShapeDtypeStruct((m, n), out_dtype),
        compiler_params=_cparams("parallel"),
        name="matmul",
    )(a, w)


def _mm_nt_kernel(w_ref, a_ref, o_ref):
    o_ref[...] = lax.dot_general(w_ref[...], a_ref[...], (((1,), (1,)), ((), ())),
                                 preferred_element_type=F32).astype(o_ref.dtype)


def _matmul_nt(wt, a, out_dtype, tm=512):
    n, k = wt.shape
    m = a.shape[0]
    return pl.pallas_call(
        _mm_nt_kernel,
        grid=(m // tm,),
        in_specs=[pl.BlockSpec((n, k), lambda i: (0, 0)),
                  pl.BlockSpec((tm, k), lambda i: (i, 0))],
        out_specs=pl.BlockSpec((n, tm), lambda i: (0, i)),
        out_shape=jax.ShapeDtypeStruct((n, m), out_dtype),
        compiler_params=_cparams("parallel"),
        name="matmul_nt",
    )(wt, a)


def _mm_hi_kernel(a_ref, w_ref, o_ref):
    o_ref[...] = jnp.dot(a_ref[...], w_ref[...], preferred_element_type=F32, precision=HIGHEST)


def _matmul_hi(a, w, tm):
    m, k = a.shape
    n = w.shape[1]
    return pl.pallas_call(
        _mm_hi_kernel,
        grid=(m // tm,),
        in_specs=[pl.BlockSpec((tm, k), lambda i: (i, 0)),
                  pl.BlockSpec((k, n), lambda i: (0, 0))],
        out_specs=pl.BlockSpec((tm, n), lambda i: (i, 0)),
        out_shape=jax.ShapeDtypeStruct((m, n), F32),
        compiler_params=_cparams("parallel"),
        name="matmul_hi",
    )(a, w)


def _qkv_kernel(u_ref, qg_ref, kvg_ref, wq_ref, wk_ref, wv_ref, ct_ref, st_ref, q_ref, k_ref, v_ref):
    u = u_ref[...]
    nq = _rms(u[:, :Q_LORA], qg_ref[...]).astype(BF16)
    nkv = _rms(u[:, Q_LORA:Q_LORA + KV_LORA], kvg_ref[...]).astype(BF16)
    ka = u[:, Q_LORA + KV_LORA:Q_LORA + KV_LORA + HEAD_PAD]
    kb = u[:, Q_LORA + KV_LORA + HEAD_PAD:]
    ct = ct_ref[...]
    st = st_ref[...]
    kr = ka * ct + kb * st
    qq = jnp.dot(nq, wq_ref[...], preferred_element_type=F32)
    kn = jnp.dot(nkv, wk_ref[...], preferred_element_type=F32)
    nh = MLA_HEADS * HEAD_PAD
    for h in range(MLA_HEADS):
        lo, hi = h * HEAD_PAD, (h + 1) * HEAD_PAD
        qh = qq[:, lo:hi] * ct + qq[:, nh + lo:nh + hi] * st
        q_ref[:, lo:hi] = (qh * Q_SCALE).astype(BF16)
        k_ref[:, lo:hi] = (kn[:, lo:hi] + kr).astype(BF16)
    v_ref[...] = lax.dot_general(wv_ref[...], nkv, (((1,), (1,)), ((), ())),
                                 preferred_element_type=F32).astype(BF16)


def _qkv_up(u, qg, kvg, wq, wk, wvt, ctab, stab, *, rope, tm=512):
    t = u.shape[0]
    ntab = ctab.shape[0] // tm
    tab_map = (lambda i: (i % ntab, 0)) if rope else (lambda i: (0, 0))
    nh = MLA_HEADS * HEAD_PAD
    nv = MLA_HEADS * V_HEAD
    const = lambda i: (0, 0)
    return pl.pallas_call(
        _qkv_kernel,
        grid=(t // tm,),
        in_specs=[pl.BlockSpec((tm, QKV_COLS), lambda i: (i, 0)),
                  pl.BlockSpec((1, Q_LORA), const),
                  pl.BlockSpec((1, KV_LORA), const),
                  pl.BlockSpec((Q_LORA, 2 * nh), const),
                  pl.BlockSpec((KV_LORA, nh), const),
                  pl.BlockSpec((nv, KV_LORA), const),
                  pl.BlockSpec((tm, HEAD_PAD), tab_map),
                  pl.BlockSpec((tm, HEAD_PAD), tab_map)],
        out_specs=[pl.BlockSpec((tm, nh), lambda i: (i, 0)),
                   pl.BlockSpec((tm, nh), lambda i: (i, 0)),
                   pl.BlockSpec((nv, tm), lambda i: (0, i))],
        out_shape=[jax.ShapeDtypeStruct((t, nh), BF16),
                   jax.ShapeDtypeStruct((t, nh), BF16),
                   jax.ShapeDtypeStruct((nv, t), BF16)],
        compiler_params=_cparams("parallel"),
        name="qkv_up",
    )(u, qg, kvg, wq, wk, wvt, ctab, stab)


def _attn_kernel(*refs, lks, n_sub, tq):
    n_kv = len(lks)
    q_ref = refs[0]
    k_refs = refs[1:1 + n_kv]
    v_refs = refs[1 + n_kv:1 + 2 * n_kv]
    o_ref = refs[1 + 2 * n_kv]
    nt = (((1,), (1,)), ((), ()))
    units = [(j, h) for j in range(n_sub) for h in range(HEADS_PER_STEP)]

    def scores(j, h):
        q = q_ref[j * tq:(j + 1) * tq, h * HEAD_PAD:(h + 1) * HEAD_PAD]
        out = []
        for k_ref, lk in zip(k_refs, lks):
            step = min(lk, KEY_SPLIT)
            for c in range(0, lk, step):
                out.append(lax.dot_general(k_ref[c:c + step, h * HEAD_PAD:(h + 1) * HEAD_PAD], q, nt,
                                           preferred_element_type=F32))
        return out

    s_next = scores(*units[0])
    for idx, (j, h) in enumerate(units):
        s = s_next
        if idx + 1 < len(units):
            s_next = scores(*units[idx + 1])
        m = s[0].max(axis=0, keepdims=True)
        for si in s[1:]:
            m = jnp.maximum(m, si.max(axis=0, keepdims=True))
        p = [jnp.exp2(si - m) for si in s]
        l = p[0].sum(axis=0, keepdims=True)
        for pi in p[1:]:
            l = l + pi.sum(axis=0, keepdims=True)
        o, i = None, 0
        for v_ref, lk in zip(v_refs, lks):
            step = min(lk, KEY_SPLIT)
            for c in range(0, lk, step):
                d = jnp.dot(v_ref[h * V_HEAD:(h + 1) * V_HEAD, c:c + step], p[i].astype(BF16),
                            preferred_element_type=F32)
                o = d if o is None else o + d
                i += 1
        o_ref[h * V_HEAD:(h + 1) * V_HEAD, j * tq:(j + 1) * tq] = (o / l).astype(BF16)


def _attention(q, kv_sets, *, nb, tq=256, n_sub=2):
    t = q.shape[0]
    lq = t // nb
    n_sub = min(n_sub, lq // tq)
    tb = tq * n_sub
    nq = lq // tb
    groups = MLA_HEADS // HEADS_PER_STEP
    qw = HEADS_PER_STEP * HEAD_PAD
    vw = HEADS_PER_STEP * V_HEAD
    in_specs = [pl.BlockSpec((tb, qw), lambda b, g, i: (b * nq + i, g))]
    k_specs, v_specs, ks, vs = [], [], [], []
    for k, vt, lk in kv_sets:
        k_specs.append(pl.BlockSpec((lk, qw), lambda b, g, i: (b, g)))
        v_specs.append(pl.BlockSpec((vw, lk), lambda b, g, i: (g, b)))
        ks.append(k)
        vs.append(vt)
    lks = tuple(lk for _, _, lk in kv_sets)
    return pl.pallas_call(
        functools.partial(_attn_kernel, lks=lks, n_sub=n_sub, tq=tq),
        grid=(nb, groups, nq),
        in_specs=in_specs + k_specs + v_specs,
        out_specs=pl.BlockSpec((vw, tb), lambda b, g, i: (g, b * nq + i)),
        out_shape=jax.ShapeDtypeStruct((MLA_HEADS * V_HEAD, t), BF16),
        compiler_params=_cparams("parallel", "parallel", "arbitrary"),
        name="attention",
    )(q, *ks, *vs)


def _filt_mlp_kernel(z_ref, w1_ref, b1_ref, w2_ref, b2_ref, f0_ref, f1_ref, o_ref):
    h = jnp.dot(w1_ref[...], z_ref[...], preferred_element_type=F32, precision=HIGHEST) + b1_ref[...]
    h = jnp.sin(f0_ref[...] * h)
    h = jnp.dot(w2_ref[...], h, preferred_element_type=F32, precision=HIGHEST) + b2_ref[...]
    o_ref[...] = jnp.sin(f1_ref[...] * h)


def _filt_mlp(zt, w1, b1, w2, b2, freq):
    kz, n = zt.shape
    w1t = jnp.zeros((FILTER_HIDDEN, kz), F32).at[:, :w1.shape[0]].set(w1.T)
    col = lambda a: a.reshape(FILTER_HIDDEN, 1)
    args = (zt, w1t, col(b1), w2.T, col(b2), col(freq[0]), col(freq[1]))
    return pl.pallas_call(
        _filt_mlp_kernel,
        grid=(1,),
        in_specs=[pl.BlockSpec(a.shape, lambda i: (0, 0)) for a in args],
        out_specs=pl.BlockSpec((FILTER_HIDDEN, n), lambda i: (0, 0)),
        out_shape=jax.ShapeDtypeStruct((FILTER_HIDDEN, n), F32),
        compiler_params=_cparams("arbitrary"),
        name="filt_mlp",
    )(*args)


def _filt_taps_kernel(h_ref, w3_ref, d_ref, t_ref, o_ref, *, seq):
    n = h_ref.shape[1]
    pos = lax.broadcasted_iota(jnp.int32, (1, n), 1)
    win = jnp.exp(-t_ref[...] * d_ref[...])
    h = h_ref[...]
    for o in range(HYENA_ORDER):
        fwd = jnp.dot(w3_ref[0, o], h, preferred_element_type=F32, precision=HIGHEST)
        bwd = jnp.dot(w3_ref[1, o], h, preferred_element_type=F32, precision=HIGHEST)
        full = jnp.where(pos < seq, fwd, jnp.where(pos > seq, bwd, 0.0)) * win
        o_ref[o] = full / jnp.sum(jnp.abs(full), axis=1, keepdims=True)


def _filt_taps(h2t, w3, dabs, tvec, seq, cf=64):
    n = h2t.shape[1]
    w3t = w3.T.reshape(2, HYENA_ORDER, HYENA_W, FILTER_HIDDEN)
    return pl.pallas_call(
        functools.partial(_filt_taps_kernel, seq=seq),
        grid=(HYENA_W // cf,),
        in_specs=[pl.BlockSpec((FILTER_HIDDEN, n), lambda c: (0, 0)),
                  pl.BlockSpec((2, HYENA_ORDER, cf, FILTER_HIDDEN), lambda c: (0, 0, c, 0)),
                  pl.BlockSpec((cf, 1), lambda c: (c, 0)),
                  pl.BlockSpec((1, n), lambda c: (0, 0))],
        out_specs=pl.BlockSpec((HYENA_ORDER, cf, n), lambda c: (0, c, 0)),
        out_shape=jax.ShapeDtypeStruct((HYENA_ORDER, HYENA_W, n), F32),
        compiler_params=_cparams("parallel"),
        name="filt_taps",
    )(h2t, w3t, dabs, tvec)


def _filt_spec_kernel(x_ref, f1_ref, tc_ref, ts_ref, f2_ref, o_ref):
    ns = x_ref.shape[0]
    x = jnp.concatenate([x_ref[s] for s in range(ns)], axis=1)
    a = jnp.dot(f1_ref[...], x, preferred_element_type=F32, precision=HIGHEST)
    tc = tc_ref[...]
    ts = ts_ref[...]
    rows = []
    for s in range(ns):
        ar = a[:FFT_N1, s * FFT_N2:(s + 1) * FFT_N2]
        ai = a[FFT_N1:, s * FFT_N2:(s + 1) * FFT_N2]
        rows.append(jnp.concatenate([ar * tc + ai * ts, ai * tc - ar * ts], axis=1))
    b = jnp.concatenate(rows, axis=0)
    z = jnp.dot(b, f2_ref[...], preferred_element_type=F32, precision=HIGHEST)
    o_ref[...] = z.reshape(ns, FFT_N1, 2 * FFT_N2)


def _filt_spec(hfull, consts, ns=16):
    r = hfull.shape[0]
    x = hfull.reshape(r, FFT_N1, FFT_N2)
    f1h, tc, ts, f2 = consts["f1h"], consts["tc"], consts["ts"], consts["f2_f32"]
    const = lambda i: (0, 0)
    return pl.pallas_call(
        _filt_spec_kernel,
        grid=(r // ns,),
        in_specs=[pl.BlockSpec((ns, FFT_N1, FFT_N2), lambda i: (i, 0, 0)),
                  pl.BlockSpec(f1h.shape, const), pl.BlockSpec(tc.shape, const),
                  pl.BlockSpec(ts.shape, const), pl.BlockSpec(f2.shape, const)],
        out_specs=pl.BlockSpec((ns, FFT_N1, 2 * FFT_N2), lambda i: (i, 0, 0)),
        out_shape=jax.ShapeDtypeStruct((r, FFT_N1, 2 * FFT_N2), F32),
        compiler_params=_cparams("parallel"),
        name="filt_spec",
    )(x, f1h, tc, ts, f2)


def _short_conv_tile(x, w, b):
    nrow = x.shape[1]
    lane = lax.broadcasted_iota(jnp.int32, x.shape, 2)
    row = lax.broadcasted_iota(jnp.int32, x.shape, 1)
    r1 = pltpu.roll(x, 1, 2)
    r2 = pltpu.roll(r1, 1, 1)
    prev = jnp.where(lane == 0, jnp.where(row == 0, 0.0, r2), r1)
    l1 = pltpu.roll(x, FFT_N2 - 1, 2)
    l2 = pltpu.roll(l1, nrow - 1, 1)
    nxt = jnp.where(lane == FFT_N2 - 1, jnp.where(row == nrow - 1, 0.0, l2), l1)
    return prev * w[0] + x * w[1] + nxt * w[2] + b


def _long_conv_pair(vr, vi, h, f1, f1i, f2, f2i, tc, ts):
    ns = vr.shape[0]
    half = FFT_N2
    x = jnp.concatenate([jnp.concatenate([vr[s] for s in range(ns)], axis=1),
                         jnp.concatenate([vi[s] for s in range(ns)], axis=1)], axis=0).astype(BF16)
    a = jnp.dot(f1, x, preferred_element_type=F32)
    rows = []
    for s in range(ns):
        ar = a[:FFT_N1, s * half:(s + 1) * half]
        ai = a[FFT_N1:, s * half:(s + 1) * half]
        rows.append(jnp.concatenate([ar * tc + ai * ts, ai * tc - ar * ts], axis=1))
    b = jnp.concatenate(rows, axis=0).astype(BF16)
    z = jnp.dot(b, f2, preferred_element_type=F32).reshape(ns, FFT_N1, 2 * half)
    zr, zi = z[:, :, :half], z[:, :, half:]
    hr, hi = h[:, :, :half], h[:, :, half:]
    w = jnp.concatenate([zr * hr - zi * hi, zr * hi + zi * hr], axis=2)
    w = w.reshape(ns * FFT_N1, 2 * half).astype(BF16)
    c = jnp.dot(w, f2i, preferred_element_type=F32).reshape(ns, FFT_N1, 2 * half)
    cr, ci = c[:, :, :half], c[:, :, half:]
    dr = cr * tc - ci * ts
    di = cr * ts + ci * tc
    d = jnp.concatenate([jnp.concatenate([dr[s] for s in range(ns)], axis=1),
                         jnp.concatenate([di[s] for s in range(ns)], axis=1)], axis=0).astype(BF16)
    y = jnp.dot(f1i, d, preferred_element_type=F32)
    nrow = y.shape[0] // 2
    yr = jnp.stack([y[:nrow, s * half:(s + 1) * half] for s in range(ns)], axis=0)
    yi = jnp.stack([y[nrow:, s * half:(s + 1) * half] for s in range(ns)], axis=0)
    return yr, yi


def _hyena_kernel(x0_ref, x1_ref, v_ref, cw_ref, cb_ref, hb_ref, h_ref,
                  f1_ref, f1i_ref, f2_ref, f2i_ref, tc_ref, ts_ref, o_ref):
    consts = (f1_ref[...], f1i_ref[...], f2_ref[...], f2i_ref[...], tc_ref[...], ts_ref[...])
    g0, g1, v = [], [], []
    for half in range(2):
        g0.append(_short_conv_tile(x0_ref[:, half], cw_ref[0], cb_ref[0]))
        g1.append(_short_conv_tile(x1_ref[:, half], cw_ref[1], cb_ref[1]))
        v.append(_short_conv_tile(v_ref[:, half], cw_ref[2], cb_ref[2]))
    gates = (g0, g1)
    for o in range(HYENA_ORDER):
        yr, yi = _long_conv_pair(v[0], v[1], h_ref[o], *consts)
        v = [gates[o][0] * (yr + hb_ref[o] * v[0]), gates[o][1] * (yi + hb_ref[o] * v[1])]
    o_ref[:, 0] = v[0].astype(BF16)
    o_ref[:, 1] = v[1].astype(BF16)


def _hyena_latent(ut, cw, cb, hb, hspec, consts, *, nb, ct=32):
    seq = ut.shape[1] // nb
    nrow = seq // FFT_N2
    npair = nb // 2
    nct = HYENA_W // ct
    u5 = ut.reshape(3 * HYENA_W, 2, npair, nrow, FFT_N2)
    part = lambda p: pl.BlockSpec((ct, 2, None, nrow, FFT_N2), lambda c, j: (p * nct + c, 0, j, 0, 0))
    const2 = lambda c, j: (0, 0)
    cnames = ("f1", "f1i", "f2", "f2i", "tc", "ts")
    out = pl.pallas_call(
        _hyena_kernel,
        grid=(nct, npair),
        in_specs=[part(0), part(1), part(2),
                  pl.BlockSpec((3, 3, ct, 1, 1), lambda c, j: (0, 0, c, 0, 0)),
                  pl.BlockSpec((3, ct, 1, 1), lambda c, j: (0, c, 0, 0)),
                  pl.BlockSpec((HYENA_ORDER, ct, 1, 1), lambda c, j: (0, c, 0, 0)),
                  pl.BlockSpec((HYENA_ORDER, ct, FFT_N1, 2 * FFT_N2), lambda c, j: (0, c, 0, 0))]
                 + [pl.BlockSpec(consts[k].shape, const2) for k in cnames],
        out_specs=pl.BlockSpec((ct, 2, None, nrow, FFT_N2), lambda c, j: (c, 0, j, 0, 0)),
        out_shape=jax.ShapeDtypeStruct((HYENA_W, 2, npair, nrow, FFT_N2), BF16),
        compiler_params=_cparams("parallel", "arbitrary"),
        name="hyena",
    )(u5, u5, u5, cw, cb, hb, hspec, *[consts[k] for k in cnames])
    return out.reshape(HYENA_W, nb * seq)


def _hyena_ctx_kernel(x0_ref, x1_ref, v_ref, cw_ref, cb_ref, hb_ref, h_ref, cf_ref, ci_ref, o_ref, *, nb):
    seq = x0_ref.shape[1] // nb
    cc = x0_ref.shape[0]
    half = h_ref.shape[2] // 2

    def rows(ref, p):
        x = jnp.concatenate([ref[:, b * seq:(b + 1) * seq] for b in range(nb)], axis=0)
        lane = lax.broadcasted_iota(jnp.int32, x.shape, 1)
        prev = jnp.where(lane == 0, 0.0, pltpu.roll(x, 1, 1))
        nxt = jnp.where(lane == seq - 1, 0.0, pltpu.roll(x, seq - 1, 1))
        w = [jnp.concatenate([cw_ref[p, j]] * nb, axis=0) for j in range(3)]
        return prev * w[0] + x * w[1] + nxt * w[2] + jnp.concatenate([cb_ref[p]] * nb, axis=0)

    g = (rows(x0_ref, 0), rows(x1_ref, 1))
    v = rows(v_ref, 2)
    for o in range(HYENA_ORDER):
        spec = jnp.dot(v.astype(BF16), cf_ref[...], preferred_element_type=F32)
        xr, xi = spec[:, :half], spec[:, half:]
        hh = jnp.concatenate([h_ref[o]] * nb, axis=0)
        hr, hi = hh[:, :half], hh[:, half:]
        y = jnp.concatenate([xr * hr - xi * hi, xr * hi + xi * hr], axis=1).astype(BF16)
        conv = jnp.dot(y, ci_ref[...], preferred_element_type=F32)
        v = g[o] * (conv + jnp.concatenate([hb_ref[o]] * nb, axis=0) * v)
    for b in range(nb):
        o_ref[:, b * seq:(b + 1) * seq] = v[b * cc:(b + 1) * cc].astype(BF16)


def _hyena_ctx(ut, cw, cb, hb, hspec, cfwd, cinv, *, nb, cc=64):
    n = ut.shape[1]
    ncc = HYENA_W // cc
    part = lambda p: pl.BlockSpec((cc, n), lambda c: (p * ncc + c, 0))
    return pl.pallas_call(
        functools.partial(_hyena_ctx_kernel, nb=nb),
        grid=(ncc,),
        in_specs=[part(0), part(1), part(2),
                  pl.BlockSpec((3, 3, cc, 1), lambda c: (0, 0, c, 0)),
                  pl.BlockSpec((3, cc, 1), lambda c: (0, c, 0)),
                  pl.BlockSpec((HYENA_ORDER, cc, 1), lambda c: (0, c, 0)),
                  pl.BlockSpec((HYENA_ORDER, cc, hspec.shape[2]), lambda c: (0, c, 0)),
                  pl.BlockSpec(cfwd.shape, lambda c: (0, 0)),
                  pl.BlockSpec(cinv.shape, lambda c: (0, 0))],
        out_specs=pl.BlockSpec((cc, n), lambda c: (c, 0)),
        out_shape=jax.ShapeDtypeStruct((HYENA_W, n), BF16),
        compiler_params=_cparams("parallel"),
        name="hyena_ctx",
    )(ut, ut, ut, cw, cb, hb, hspec, cfwd, cinv)


def _merge_kernel(h_ref, ug_ref, gb_ref, yt_ref, att_ref, hp_ref, mp_ref, wo_ref, mods_ref, o_ref, *, tpb, base):
    row = base + pl.program_id(0) // tpb
    g = _sigmoid(ug_ref[...] + gb_ref[...])
    yh = lax.dot_general(yt_ref[...], hp_ref[...], (((0,), (0,)), ((), ())), preferred_element_type=F32)
    ya = lax.dot_general(att_ref[...], mp_ref[...], (((0,), (0,)), ((), ())), preferred_element_type=F32)
    mix = (g[:, :D_MODEL] * yh + g[:, D_MODEL:] * ya).astype(BF16)
    out = jnp.dot(mix, wo_ref[...], preferred_element_type=F32)
    o_ref[...] = h_ref[...] + _mod_row(mods_ref, row, 2) * out


def _merge(h, ug, gate_b, yt, att, hy_proj, mla_proj, w_out, mods, *, tpb, base, tm=512):
    t = h.shape[0]
    const = lambda i: (0, 0)
    return pl.pallas_call(
        functools.partial(_merge_kernel, tpb=tpb, base=base),
        grid=(t // tm,),
        in_specs=[pl.BlockSpec((tm, D_MODEL), lambda i: (i, 0)),
                  pl.BlockSpec((tm, 2 * D_MODEL), lambda i: (i, 0)),
                  pl.BlockSpec((1, 2 * D_MODEL), const),
                  pl.BlockSpec((HYENA_W, tm), lambda i: (0, i)),
                  pl.BlockSpec((MLA_HEADS * V_HEAD, tm), lambda i: (0, i)),
                  pl.BlockSpec(hy_proj.shape, const),
                  pl.BlockSpec(mla_proj.shape, const),
                  pl.BlockSpec(w_out.shape, const),
                  pl.BlockSpec((NMOD, 6 * D_MODEL), const)],
        out_specs=pl.BlockSpec((tm, D_MODEL), lambda i: (i, 0)),
        out_shape=jax.ShapeDtypeStruct((t, D_MODEL), F32),
        compiler_params=_cparams("parallel"),
        name="merge",
    )(h, ug, gate_b.reshape(1, -1), yt, att, hy_proj, mla_proj, w_out, mods)


def _ffn_kernel(h_ref, g_ref, mods_ref, wg_ref, wu_ref, wo_ref, fg_ref, o_ref, f_scr, *, tpb, base, final):
    j = pl.program_id(1)
    row = base + pl.program_id(0) // tpb

    @pl.when(j == 0)
    def _():
        h = h_ref[...]
        f = _rms(h, g_ref[...]) * (1.0 + _mod_row(mods_ref, row, 4)) + _mod_row(mods_ref, row, 3)
        f_scr[...] = f.astype(BF16)
        o_ref[...] = h

    f = f_scr[...]
    act = _silu(jnp.dot(f, wg_ref[...], preferred_element_type=F32)) * jnp.dot(f, wu_ref[...], preferred_element_type=F32)
    y = jnp.dot(act.astype(BF16), wo_ref[...], preferred_element_type=F32)
    o_ref[...] += _mod_row(mods_ref, row, 5) * y

    if final:
        @pl.when(j == pl.num_programs(1) - 1)
        def _():
            o_ref[...] = _rms(o_ref[...], fg_ref[...])


def _ffn(h, g, mods, w_gate, w_up, w_down, final_g, *, tpb, base, final, tm=512, nchunk=2):
    t = h.shape[0]
    fc = D_FF // nchunk
    const = lambda i, j: (0, 0)
    return pl.pallas_call(
        functools.partial(_ffn_kernel, tpb=tpb, base=base, final=final),
        grid=(t // tm, nchunk),
        in_specs=[pl.BlockSpec((tm, D_MODEL), lambda i, j: (i, 0)),
                  pl.BlockSpec((1, D_MODEL), const),
                  pl.BlockSpec((NMOD, 6 * D_MODEL), const),
                  pl.BlockSpec((D_MODEL, fc), lambda i, j: (0, j)),
                  pl.BlockSpec((D_MODEL, fc), lambda i, j: (0, j)),
                  pl.BlockSpec((fc, D_MODEL), lambda i, j: (j, 0)),
                  pl.BlockSpec((1, D_MODEL), const)],
        out_specs=pl.BlockSpec((tm, D_MODEL), lambda i, j: (i, 0)),
        out_shape=jax.ShapeDtypeStruct((t, D_MODEL), F32),
        scratch_shapes=[pltpu.VMEM((tm, D_MODEL), BF16)],
        compiler_params=_cparams("parallel", "arbitrary"),
        name="ffn",
    )(h, g.reshape(1, -1), mods, w_gate, w_up, w_down, final_g.reshape(1, -1))


def _dft_consts():
    n = FFT_N1 * FFT_N2
    k1 = np.arange(FFT_N1)
    th = 2 * np.pi * np.outer(k1, np.arange(FFT_N1)) / FFT_N1
    c, s = np.cos(th), np.sin(th)
    hn = FFT_N1 // 2
    f1 = np.block([[c[:, :hn], s[:, :hn]], [-s[:, :hn], c[:, :hn]]])
    f1h = np.concatenate([c, -s], axis=0)
    f1i = np.block([[c[:, :hn].T, -s[:, :hn].T], [s[:, :hn].T, c[:, :hn].T]]) / n
    ph = 2 * np.pi * np.outer(k1, np.arange(FFT_N2)) / n
    th2 = 2 * np.pi * np.outer(np.arange(FFT_N2), np.arange(FFT_N2)) / FFT_N2
    c2, s2 = np.cos(th2), np.sin(th2)
    f2 = np.block([[c2, -s2], [s2, c2]])
    f2i = np.block([[c2, s2], [-s2, c2]])
    return {"f1": jnp.asarray(f1, BF16), "f1h": jnp.asarray(f1h, F32), "f1i": jnp.asarray(f1i, BF16),
            "f2": jnp.asarray(f2, BF16), "f2i": jnp.asarray(f2i, BF16), "f2_f32": jnp.asarray(f2, F32),
            "tc": jnp.asarray(np.cos(ph), F32), "ts": jnp.asarray(np.sin(ph), F32)}


def _dense_dft(seq):
    n = 2 * seq
    th = 2 * np.pi * np.outer(np.arange(n), np.arange(n)) / n
    c, s = np.cos(th), np.sin(th)
    full = np.concatenate([c, -s], axis=1)
    inv = np.concatenate([c[:, :seq], -s[:, :seq]], axis=0) / n
    return jnp.asarray(full, F32), jnp.asarray(full[:seq], BF16), jnp.asarray(inv, BF16)


def _filter_positions(seq):
    n = np.arange(2 * seq)
    lag = np.minimum(np.where(n < seq, n, 2 * seq - n), seq - 1)
    return jnp.linspace(0.0, 1.0, seq, dtype=F32)[lag]


def _rope_tables(seq):
    rows = seq // GRID_W
    row = jnp.repeat(jnp.arange(rows, dtype=F32), GRID_W)
    col = jnp.tile(jnp.arange(GRID_W, dtype=F32), rows)
    inv = ROPE_BASE ** (-jnp.arange(0, AXIS_ROPE, 2, dtype=F32) / AXIS_ROPE)
    ang = jnp.concatenate([row[:, None] * inv, col[:, None] * inv], axis=-1)
    cos, sin = jnp.cos(ang), jnp.sin(ang)
    ones = jnp.ones((seq, QK_NOPE), F32)
    pad = HEAD_PAD - QK_NOPE - QK_ROPE
    ctab = jnp.concatenate([ones, cos, cos, jnp.ones((seq, pad), F32)], axis=1)
    stab = jnp.concatenate([0 * ones, sin, sin, jnp.zeros((seq, pad), F32)], axis=1)
    return ctab, stab


def _rot_cols(w):
    half = QK_ROPE // 2
    return jnp.concatenate([-w[..., half:], w[..., :half]], axis=-1)


def _pad_heads(nope, rope):
    k = nope.shape[0]
    z = jnp.zeros((k, MLA_HEADS, HEAD_PAD - QK_NOPE - QK_ROPE), nope.dtype)
    return jnp.concatenate([nope, rope, z], axis=-1).reshape(k, MLA_HEADS * HEAD_PAD)


def kernel(x, c, ctx, c_ctx, ada_w, ada_b, norm1_g, norm2_g, w_in, gate_b, hy_conv_w, hy_conv_b,
           hy_filt_w1, hy_filt_b1, hy_filt_w2, hy_filt_b2, hy_filt_w3, hy_filt_freq, hy_bias, hy_proj,
           q_norm_g, kv_norm_g, w_uq, w_ukv, mla_proj, w_out, ffn_w_in, ffn_w_out, final_norm_g):
    nb, seq, d = x.shape
    lc = ctx.shape[1]
    depth = ada_w.shape[0]
    assert d == D_MODEL and seq == FFT_N1 * FFT_N2 // 2 and nb % 2 == 0 and nb < NMOD
    tm = 512
    tpb_lat = seq // tm
    tpb_ctx = 1 << 30

    cc = jnp.zeros((NMOD, d), F32).at[:nb].set(c).at[nb].set(c_ctx)
    mods_all = _ada(cc, ada_w, ada_b)

    consts = _dft_consts()
    cfull_c, cfwd_c, cinv_c = _dense_dft(lc)
    ctab, stab = _rope_tables(seq)
    one_tab = jnp.ones((tm, HEAD_PAD), F32)
    zero_tab = jnp.zeros((tm, HEAD_PAD), F32)
    dabs = jnp.abs(jnp.linspace(HYENA_MIN_DECAY, HYENA_MAX_DECAY, HYENA_W, dtype=F32)).reshape(HYENA_W, 1)
    bands = jnp.arange(1, N_BANDS + 1, dtype=F32)

    def pos_features(length):
        t = _filter_positions(length)
        ang = 2 * math.pi * bands[:, None] * t[None, :]
        zt = jnp.concatenate([t[None, :], jnp.cos(ang), jnp.sin(ang),
                              jnp.zeros((7, 2 * length), F32)], axis=0)
        return zt, t.reshape(1, -1)

    zt_lat, tv_lat = pos_features(seq)
    zt_ctx, tv_ctx = pos_features(lc)

    h_lat = x.reshape(nb * seq, d)
    h_ctx = ctx.reshape(nb * lc, d)

    for i in range(depth):
        last = i == depth - 1
        mods = mods_all[i]

        wi = w_in[i]
        w_hy_t = wi[:, :OFF_Q].T.astype(BF16)
        kr_w = wi[:, OFF_KR:OFF_G]
        zl = jnp.zeros((d, ROPE_LANE), F32)
        zr = jnp.zeros((d, HEAD_PAD - ROPE_LANE - QK_ROPE), F32)
        w_qkv = jnp.concatenate([wi[:, OFF_Q:OFF_KR], zl, kr_w, zr, zl, _rot_cols(kr_w), zr], axis=1).astype(BF16)
        w_gate_in = wi[:, OFF_G:].astype(BF16)
        uq = w_uq[i].reshape(Q_LORA, MLA_HEADS, QK_NOPE + QK_ROPE)
        wq = jnp.concatenate([_pad_heads(uq[..., :QK_NOPE], uq[..., QK_NOPE:]),
                              _pad_heads(0 * uq[..., :QK_NOPE], _rot_cols(uq[..., QK_NOPE:]))], axis=1).astype(BF16)
        ukv = w_ukv[i].reshape(KV_LORA, MLA_HEADS, QK_NOPE + V_HEAD)
        wk = _pad_heads(ukv[..., :QK_NOPE], jnp.zeros((KV_LORA, MLA_HEADS, QK_ROPE), F32)).astype(BF16)
        wv = ukv[..., QK_NOPE:].reshape(KV_LORA, MLA_HEADS * V_HEAD).T.astype(BF16)
        qg = q_norm_g[i].reshape(1, -1)
        kvg = kv_norm_g[i].reshape(1, -1)
        hp, mp, wo = hy_proj[i].astype(BF16), mla_proj[i].astype(BF16), w_out[i].astype(BF16)
        w_ffg = ffn_w_in[i][:, :D_FF].astype(BF16)
        w_ffu = ffn_w_in[i][:, D_FF:].astype(BF16)
        w_ffo = ffn_w_out[i].astype(BF16)
        cw = jnp.transpose(hy_conv_w[i].reshape(3, 3, HYENA_W), (1, 0, 2))
        cb = hy_conv_b[i].reshape(3, HYENA_W)
        hb = hy_bias[i]

        filt = (hy_filt_w1[i], hy_filt_b1[i], hy_filt_w2[i], hy_filt_b2[i], hy_filt_freq[i])
        hfull = _filt_taps(_filt_mlp(zt_lat, *filt), hy_filt_w3[i], dabs, tv_lat, seq)
        hspec = _filt_spec(hfull.reshape(HYENA_ORDER * HYENA_W, 2 * seq), consts)
        hspec = hspec.reshape(HYENA_ORDER, HYENA_W, FFT_N1, 2 * FFT_N2)

        a_l = _norm_mod(h_lat, norm1_g[i], mods, tpb=tpb_lat, base=0, k_shift=0, k_scale=1)
        a_c = _norm_mod(h_ctx, norm1_g[i], mods, tpb=tpb_ctx, base=nb, k_shift=0, k_scale=1)
        q_l, k_l, v_l = _qkv_up(_matmul(a_l, w_qkv, F32), qg, kvg, wq, wk, wv, ctab, stab, rope=True)
        q_c, k_c, v_c = _qkv_up(_matmul(a_c, w_qkv, F32), qg, kvg, wq, wk, wv, one_tab, zero_tab, rope=False)
        att_l = _attention(q_l, [(k_l, v_l, seq), (k_c, v_c, lc)], nb=nb)
        ut_l = _matmul_nt(w_hy_t, a_l, F32)
        yt_l = _hyena_latent(ut_l, cw.reshape(3, 3, HYENA_W, 1, 1), cb.reshape(3, HYENA_W, 1, 1),
                             hb.reshape(HYENA_ORDER, HYENA_W, 1, 1), hspec, consts, nb=nb)
        ug_l = _matmul(a_l, w_gate_in, F32)
        h_lat = _merge(h_lat, ug_l, gate_b[i], yt_l, att_l, hp, mp, wo, mods, tpb=tpb_lat, base=0)

        h_lat = _ffn(h_lat, norm2_g[i], mods, w_ffg, w_ffu, w_ffo, final_norm_g,
                     tpb=tpb_lat, base=0, final=last)

        if not last:
            hfull_c = _filt_taps(_filt_mlp(zt_ctx, *filt), hy_filt_w3[i], dabs, tv_ctx, lc)
            hspec_c = _matmul_hi(hfull_c.reshape(HYENA_ORDER * HYENA_W, 2 * lc), cfull_c, tm=256)
            hspec_c = hspec_c.reshape(HYENA_ORDER, HYENA_W, 4 * lc)
            att_c = _attention(q_c, [(k_c, v_c, lc)], nb=nb)
            ut_c = _matmul_nt(w_hy_t, a_c, F32)
            yt_c = _hyena_ctx(ut_c, cw.reshape(3, 3, HYENA_W, 1), cb.reshape(3, HYENA_W, 1),
                              hb.reshape(HYENA_ORDER, HYENA_W, 1), hspec_c, cfwd_c, cinv_c, nb=nb)
            ug_c = _matmul(a_c, w_gate_in, F32)
            h_ctx = _merge(h_ctx, ug_c, gate_b[i], yt_c, att_c, hp, mp, wo, mods, tpb=tpb_ctx, base=nb)
            h_ctx = _ffn(h_ctx, norm2_g[i], mods, w_ffg, w_ffu, w_ffo, final_norm_g,
                         tpb=tpb_ctx, base=nb, final=False)

    return h_lat.reshape(nb, seq, d)
```

```python
import functools
import math

import numpy as np
import jax
import jax.numpy as jnp
from jax import lax
from jax.experimental import pallas as pl
from jax.experimental.pallas import tpu as pltpu

F32 = jnp.float32
BF16 = jnp.bfloat16
HIGHEST = lax.Precision.HIGHEST

D_MODEL = 1024
GRID_W = 64
NORM_EPS = 1e-6

HYENA_W = 512
HYENA_ORDER = 2
N_BANDS = 16
FILTER_HIDDEN = 64
HYENA_TARGET = 1e-2
HYENA_MAX_DECAY = math.log(HYENA_TARGET) / 0.3
HYENA_MIN_DECAY = math.log(HYENA_TARGET) / 1.5

MLA_HEADS = 16
QK_NOPE = 64
QK_ROPE = 32
V_HEAD = 64
Q_LORA = 384
KV_LORA = 256
ROPE_BASE = 10000.0
AXIS_ROPE = QK_ROPE // 2
ATTN_SCALE = (QK_NOPE + QK_ROPE) ** -0.5
Q_SCALE = ATTN_SCALE * math.log2(math.e)
D_FF = 2816

HY_COLS = (HYENA_ORDER + 1) * HYENA_W
OFF_Q = HY_COLS
OFF_KV = OFF_Q + Q_LORA
OFF_KR = OFF_KV + KV_LORA
OFF_G = OFF_KR + QK_ROPE

HEAD_PAD = 128
ROPE_LANE = QK_NOPE
QKV_COLS = Q_LORA + KV_LORA + 2 * HEAD_PAD
HEADS_PER_STEP = 4
KEY_SPLIT = 1024
NMOD = 16

FFT_N1 = 64
FFT_N2 = 128
VMEM_LIMIT = 56 * 1024 * 1024


def _cparams(*sem):
    return pltpu.CompilerParams(dimension_semantics=sem, vmem_limit_bytes=VMEM_LIMIT)


def _rms(x, g):
    return x * lax.rsqrt(jnp.mean(x * x, axis=-1, keepdims=True) + NORM_EPS) * g


def _silu(x):
    return x * (1.0 / (1.0 + jnp.exp(-x)))


def _sigmoid(x):
    return 1.0 / (1.0 + jnp.exp(-x))


def _mod_row(mods_ref, row, k):
    return mods_ref[pl.ds(row, 1), k * D_MODEL:(k + 1) * D_MODEL]


def _ada_kernel(c_ref, w_ref, b_ref, o_ref):
    s = _silu(c_ref[...])
    o_ref[...] = jnp.dot(s, w_ref[...], preferred_element_type=F32, precision=HIGHEST) + b_ref[...]


def _ada(cc, ada_w, ada_b):
    depth = ada_w.shape[0]
    tn = D_MODEL
    return pl.pallas_call(
        _ada_kernel,
        grid=(depth, 6 * D_MODEL // tn),
        in_specs=[pl.BlockSpec((NMOD, D_MODEL), lambda l, j: (0, 0)),
                  pl.BlockSpec((None, D_MODEL, tn), lambda l, j: (l, 0, j)),
                  pl.BlockSpec((None, 1, tn), lambda l, j: (l, 0, j))],
        out_specs=pl.BlockSpec((None, NMOD, tn), lambda l, j: (l, 0, j)),
        out_shape=jax.ShapeDtypeStruct((depth, NMOD, 6 * D_MODEL), F32),
        compiler_params=_cparams("arbitrary", "arbitrary"),
        name="ada",
    )(cc, ada_w, ada_b.reshape(depth, 1, 6 * D_MODEL))


def _in_proj_kernel(h_ref, g_ref, mods_ref, wqkv_ref, wg_ref, wht_ref, uq_ref, ug_ref, ut_ref, *, tpb, base):
    row = base + pl.program_id(0) // tpb
    y = _rms(h_ref[...], g_ref[...])
    a = (y * (1.0 + _mod_row(mods_ref, row, 1)) + _mod_row(mods_ref, row, 0)).astype(BF16)
    uq_ref[...] = jnp.dot(a, wqkv_ref[...], preferred_element_type=F32).astype(BF16)
    ug_ref[...] = jnp.dot(a, wg_ref[...], preferred_element_type=F32).astype(BF16)
    ut_ref[...] = lax.dot_general(wht_ref[...], a, (((1,), (1,)), ((), ())), preferred_element_type=F32)


def _in_proj(h, g, mods, w_qkv, w_gate, w_hy_t, layer, *, tpb, base, tm=512):
    t = h.shape[0]
    const = lambda i: (0, 0)
    wmap = lambda i: (layer, 0, 0)
    return pl.pallas_call(
        functools.partial(_in_proj_kernel, tpb=tpb, base=base),
        grid=(t // tm,),
        in_specs=[pl.BlockSpec((tm, D_MODEL), lambda i: (i, 0)),
                  pl.BlockSpec((1, D_MODEL), const),
                  pl.BlockSpec((NMOD, 6 * D_MODEL), const),
                  pl.BlockSpec((None, D_MODEL, QKV_COLS), wmap),
                  pl.BlockSpec((None, D_MODEL, 2 * D_MODEL), wmap),
                  pl.BlockSpec((None, HY_COLS, D_MODEL), wmap)],
        out_specs=[pl.BlockSpec((tm, QKV_COLS), lambda i: (i, 0)),
                   pl.BlockSpec((tm, 2 * D_MODEL), lambda i: (i, 0)),
                   pl.BlockSpec((HY_COLS, tm), lambda i: (0, i))],
        out_shape=[jax.ShapeDtypeStruct((t, QKV_COLS), BF16),
                   jax.ShapeDtypeStruct((t, 2 * D_MODEL), BF16),
                   jax.ShapeDtypeStruct((HY_COLS, t), F32)],
        compiler_params=_cparams("parallel"),
        name="in_proj",
    )(h, g.reshape(1, D_MODEL), mods, w_qkv, w_gate, w_hy_t)


def _mm_hi_kernel(a_ref, w_ref, o_ref):
    o_ref[...] = jnp.dot(a_ref[...], w_ref[...], preferred_element_type=F32, precision=HIGHEST)


def _matmul_hi(a, w, tm):
    m, k = a.shape
    n = w.shape[1]
    return pl.pallas_call(
        _mm_hi_kernel,
        grid=(m // tm,),
        in_specs=[pl.BlockSpec((tm, k), lambda i: (i, 0)),
                  pl.BlockSpec((k, n), lambda i: (0, 0))],
        out_specs=pl.BlockSpec((tm, n), lambda i: (i, 0)),
        out_shape=jax.ShapeDtypeStruct((m, n), F32),
        compiler_params=_cparams("parallel"),
        name="matmul_hi",
    )(a, w)


def _qkv_kernel(u_ref, qg_ref, kvg_ref, wq_ref, wk_ref, wv_ref, ct_ref, st_ref, q_ref, k_ref, v_ref):
    u = u_ref[...].astype(F32)
    nq = _rms(u[:, :Q_LORA], qg_ref[...]).astype(BF16)
    nkv = _rms(u[:, Q_LORA:Q_LORA + KV_LORA], kvg_ref[...]).astype(BF16)
    ka = u[:, Q_LORA + KV_LORA:Q_LORA + KV_LORA + HEAD_PAD]
    kb = u[:, Q_LORA + KV_LORA + HEAD_PAD:]
    ct = ct_ref[...]
    st = st_ref[...]
    kr = ka * ct + kb * st
    qq = jnp.dot(nq, wq_ref[...], preferred_element_type=F32)
    kn = jnp.dot(nkv, wk_ref[...], preferred_element_type=F32)
    nh = MLA_HEADS * HEAD_PAD
    for h in range(MLA_HEADS):
        lo, hi = h * HEAD_PAD, (h + 1) * HEAD_PAD
        qh = qq[:, lo:hi] * ct + qq[:, nh + lo:nh + hi] * st
        q_ref[:, lo:hi] = (qh * Q_SCALE).astype(BF16)
        k_ref[:, lo:hi] = (kn[:, lo:hi] + kr).astype(BF16)
    v_ref[...] = lax.dot_general(wv_ref[...], nkv, (((1,), (1,)), ((), ())),
                                 preferred_element_type=F32).astype(BF16)


def _qkv_up(u, qg, kvg, wq, wk, wvt, ctab, stab, layer, *, rope, tm=512):
    t = u.shape[0]
    ntab = ctab.shape[0] // tm
    tab_map = (lambda i: (i % ntab, 0)) if rope else (lambda i: (0, 0))
    nh = MLA_HEADS * HEAD_PAD
    nv = MLA_HEADS * V_HEAD
    wmap = lambda i: (layer, 0, 0)
    return pl.pallas_call(
        _qkv_kernel,
        grid=(t // tm,),
        in_specs=[pl.BlockSpec((tm, QKV_COLS), lambda i: (i, 0)),
                  pl.BlockSpec((None, 1, Q_LORA), wmap),
                  pl.BlockSpec((None, 1, KV_LORA), wmap),
                  pl.BlockSpec((None, Q_LORA, 2 * nh), wmap),
                  pl.BlockSpec((None, KV_LORA, nh), wmap),
                  pl.BlockSpec((None, nv, KV_LORA), wmap),
                  pl.BlockSpec((tm, HEAD_PAD), tab_map),
                  pl.BlockSpec((tm, HEAD_PAD), tab_map)],
        out_specs=[pl.BlockSpec((tm, nh), lambda i: (i, 0)),
                   pl.BlockSpec((tm, nh), lambda i: (i, 0)),
                   pl.BlockSpec((nv, tm), lambda i: (0, i))],
        out_shape=[jax.ShapeDtypeStruct((t, nh), BF16),
                   jax.ShapeDtypeStruct((t, nh), BF16),
                   jax.ShapeDtypeStruct((nv, t), BF16)],
        compiler_params=_cparams("parallel"),
        name="qkv_up",
    )(u, qg, kvg, wq, wk, wvt, ctab, stab)


def _attn_kernel(*refs, lks, n_sub, tq):
    n_kv = len(lks)
    q_ref = refs[0]
    k_refs = refs[1:1 + n_kv]
    v_refs = refs[1 + n_kv:1 + 2 * n_kv]
    o_ref = refs[1 + 2 * n_kv]
    nt = (((1,), (1,)), ((), ()))
    units = [(j, h) for j in range(n_sub) for h in range(HEADS_PER_STEP)]

    def scores(j, h):
        q = q_ref[j * tq:(j + 1) * tq, h * HEAD_PAD:(h + 1) * HEAD_PAD]
        out = []
        for k_ref, lk in zip(k_refs, lks):
            step = min(lk, KEY_SPLIT)
            for c in range(0, lk, step):
                out.append(lax.dot_general(k_ref[c:c + step, h * HEAD_PAD:(h + 1) * HEAD_PAD], q, nt,
                                           preferred_element_type=F32))
        return out

    s_next = scores(*units[0])
    for idx, (j, h) in enumerate(units):
        s = s_next
        if idx + 1 < len(units):
            s_next = scores(*units[idx + 1])
        m = s[0].max(axis=0, keepdims=True)
        for si in s[1:]:
            m = jnp.maximum(m, si.max(axis=0, keepdims=True))
        p = [jnp.exp2(si - m) for si in s]
        l = p[0].sum(axis=0, keepdims=True)
        for pi in p[1:]:
            l = l + pi.sum(axis=0, keepdims=True)
        o, i = None, 0
        for v_ref, lk in zip(v_refs, lks):
            step = min(lk, KEY_SPLIT)
            for c in range(0, lk, step):
                d = jnp.dot(v_ref[h * V_HEAD:(h + 1) * V_HEAD, c:c + step], p[i].astype(BF16),
                            preferred_element_type=F32)
                o = d if o is None else o + d
                i += 1
        o_ref[h * V_HEAD:(h + 1) * V_HEAD, j * tq:(j + 1) * tq] = (o / l).astype(BF16)


def _attention(q, kv_sets, *, nb, tq=256, n_sub=2):
    t = q.shape[0]
    lq = t // nb
    n_sub = min(n_sub, lq // tq)
    tb = tq * n_sub
    nq = lq // tb
    groups = MLA_HEADS // HEADS_PER_STEP
    qw = HEADS_PER_STEP * HEAD_PAD
    vw = HEADS_PER_STEP * V_HEAD
    in_specs = [pl.BlockSpec((tb, qw), lambda b, g, i: (b * nq + i, g))]
    k_specs, v_specs, ks, vs = [], [], [], []
    for k, vt, lk in kv_sets:
        k_specs.append(pl.BlockSpec((lk, qw), lambda b, g, i: (b, g)))
        v_specs.append(pl.BlockSpec((vw, lk), lambda b, g, i: (g, b)))
        ks.append(k)
        vs.append(vt)
    lks = tuple(lk for _, _, lk in kv_sets)
    return pl.pallas_call(
        functools.partial(_attn_kernel, lks=lks, n_sub=n_sub, tq=tq),
        grid=(nb, groups, nq),
        in_specs=in_specs + k_specs + v_specs,
        out_specs=pl.BlockSpec((vw, tb), lambda b, g, i: (g, b * nq + i)),
        out_shape=jax.ShapeDtypeStruct((MLA_HEADS * V_HEAD, t), BF16),
        compiler_params=_cparams("parallel", "parallel", "arbitrary"),
        name="attention",
    )(q, *ks, *vs)


def _filt_mlp_kernel(z_ref, w1_ref, b1_ref, w2_ref, b2_ref, f0_ref, f1_ref, o_ref):
    h = jnp.dot(w1_ref[...], z_ref[...], preferred_element_type=F32, precision=HIGHEST) + b1_ref[...]
    h = jnp.sin(f0_ref[...] * h)
    h = jnp.dot(w2_ref[...], h, preferred_element_type=F32, precision=HIGHEST) + b2_ref[...]
    o_ref[...] = jnp.sin(f1_ref[...] * h)


def _filt_mlp(zt, w1, b1, w2, b2, freq):
    kz, n = zt.shape
    w1t = jnp.zeros((FILTER_HIDDEN, kz), F32).at[:, :w1.shape[0]].set(w1.T)
    col = lambda a: a.reshape(FILTER_HIDDEN, 1)
    args = (zt, w1t, col(b1), w2.T, col(b2), col(freq[0]), col(freq[1]))
    return pl.pallas_call(
        _filt_mlp_kernel,
        grid=(1,),
        in_specs=[pl.BlockSpec(a.shape, lambda i: (0, 0)) for a in args],
        out_specs=pl.BlockSpec((FILTER_HIDDEN, n), lambda i: (0, 0)),
        out_shape=jax.ShapeDtypeStruct((FILTER_HIDDEN, n), F32),
        compiler_params=_cparams("arbitrary"),
        name="filt_mlp",
    )(*args)


def _filt_taps_kernel(h_ref, w3_ref, d_ref, t_ref, o_ref, *, seq):
    n = h_ref.shape[1]
    pos = lax.broadcasted_iota(jnp.int32, (1, n), 1)
    win = jnp.exp(-t_ref[...] * d_ref[...])
    h = h_ref[...]
    for o in range(HYENA_ORDER):
        fwd = jnp.dot(w3_ref[0, o], h, preferred_element_type=F32, precision=HIGHEST)
        bwd = jnp.dot(w3_ref[1, o], h, preferred_element_type=F32, precision=HIGHEST)
        full = jnp.where(pos < seq, fwd, jnp.where(pos > seq, bwd, 0.0)) * win
        o_ref[o] = full / jnp.sum(jnp.abs(full), axis=1, keepdims=True)


def _filt_taps(h2t, w3, dabs, tvec, seq, cf=64):
    n = h2t.shape[1]
    w3t = w3.T.reshape(2, HYENA_ORDER, HYENA_W, FILTER_HIDDEN)
    return pl.pallas_call(
        functools.partial(_filt_taps_kernel, seq=seq),
        grid=(HYENA_W // cf,),
        in_specs=[pl.BlockSpec((FILTER_HIDDEN, n), lambda c: (0, 0)),
                  pl.BlockSpec((2, HYENA_ORDER, cf, FILTER_HIDDEN), lambda c: (0, 0, c, 0)),
                  pl.BlockSpec((cf, 1), lambda c: (c, 0)),
                  pl.BlockSpec((1, n), lambda c: (0, 0))],
        out_specs=pl.BlockSpec((HYENA_ORDER, cf, n), lambda c: (0, c, 0)),
        out_shape=jax.ShapeDtypeStruct((HYENA_ORDER, HYENA_W, n), F32),
        compiler_params=_cparams("parallel"),
        name="filt_taps",
    )(h2t, w3t, dabs, tvec)


def _filt_spec_kernel(x_ref, f1_ref, tc_ref, ts_ref, f2_ref, o_ref):
    ns = x_ref.shape[0]
    x = jnp.concatenate([x_ref[s] for s in range(ns)], axis=1)
    a = jnp.dot(f1_ref[...], x, preferred_element_type=F32, precision=HIGHEST)
    tc = tc_ref[...]
    ts = ts_ref[...]
    rows = []
    for s in range(ns):
        ar = a[:FFT_N1, s * FFT_N2:(s + 1) * FFT_N2]
        ai = a[FFT_N1:, s * FFT_N2:(s + 1) * FFT_N2]
        rows.append(jnp.concatenate([ar * tc + ai * ts, ai * tc - ar * ts], axis=1))
    b = jnp.concatenate(rows, axis=0)
    z = jnp.dot(b, f2_ref[...], preferred_element_type=F32, precision=HIGHEST)
    o_ref[...] = z.reshape(ns, FFT_N1, 2 * FFT_N2)


def _filt_spec(hfull, consts, ns=16):
    r = hfull.shape[0]
    x = hfull.reshape(r, FFT_N1, FFT_N2)
    f1h, tc, ts, f2 = consts["f1h"], consts["tc"], consts["ts"], consts["f2_f32"]
    const = lambda i: (0, 0)
    return pl.pallas_call(
        _filt_spec_kernel,
        grid=(r // ns,),
        in_specs=[pl.BlockSpec((ns, FFT_N1, FFT_N2), lambda i: (i, 0, 0)),
                  pl.BlockSpec(f1h.shape, const), pl.BlockSpec(tc.shape, const),
                  pl.BlockSpec(ts.shape, const), pl.BlockSpec(f2.shape, const)],
        out_specs=pl.BlockSpec((ns, FFT_N1, 2 * FFT_N2), lambda i: (i, 0, 0)),
        out_shape=jax.ShapeDtypeStruct((r, FFT_N1, 2 * FFT_N2), F32),
        compiler_params=_cparams("parallel"),
        name="filt_spec",
    )(x, f1h, tc, ts, f2)


def _short_conv_tile(x, w, b):
    nrow = x.shape[1]
    lane = lax.broadcasted_iota(jnp.int32, x.shape, 2)
    row = lax.broadcasted_iota(jnp.int32, x.shape, 1)
    r1 = pltpu.roll(x, 1, 2)
    r2 = pltpu.roll(r1, 1, 1)
    prev = jnp.where(lane == 0, jnp.where(row == 0, 0.0, r2), r1)
    l1 = pltpu.roll(x, FFT_N2 - 1, 2)
    l2 = pltpu.roll(l1, nrow - 1, 1)
    nxt = jnp.where(lane == FFT_N2 - 1, jnp.where(row == nrow - 1, 0.0, l2), l1)
    return prev * w[0] + x * w[1] + nxt * w[2] + b


def _long_conv_pair(vr, vi, h, f1, f1i, f2, f2i, tc, ts):
    ns = vr.shape[0]
    half = FFT_N2
    x = jnp.concatenate([jnp.concatenate([vr[s] for s in range(ns)], axis=1),
                         jnp.concatenate([vi[s] for s in range(ns)], axis=1)], axis=0).astype(BF16)
    a = jnp.dot(f1, x, preferred_element_type=F32)
    rows = []
    for s in range(ns):
        ar = a[:FFT_N1, s * half:(s + 1) * half]
        ai = a[FFT_N1:, s * half:(s + 1) * half]
        rows.append(jnp.concatenate([ar * tc + ai * ts, ai * tc - ar * ts], axis=1))
    b = jnp.concatenate(rows, axis=0).astype(BF16)
    z = jnp.dot(b, f2, preferred_element_type=F32).reshape(ns, FFT_N1, 2 * half)
    zr, zi = z[:, :, :half], z[:, :, half:]
    hr, hi = h[:, :, :half], h[:, :, half:]
    w = jnp.concatenate([zr * hr - zi * hi, zr * hi + zi * hr], axis=2)
    w = w.reshape(ns * FFT_N1, 2 * half).astype(BF16)
    c = jnp.dot(w, f2i, preferred_element_type=F32).reshape(ns, FFT_N1, 2 * half)
    cr, ci = c[:, :, :half], c[:, :, half:]
    dr = cr * tc - ci * ts
    di = cr * ts + ci * tc
    d = jnp.concatenate([jnp.concatenate([dr[s] for s in range(ns)], axis=1),
                         jnp.concatenate([di[s] for s in range(ns)], axis=1)], axis=0).astype(BF16)
    y = jnp.dot(f1i, d, preferred_element_type=F32)
    nrow = y.shape[0] // 2
    yr = jnp.stack([y[:nrow, s * half:(s + 1) * half] for s in range(ns)], axis=0)
    yi = jnp.stack([y[nrow:, s * half:(s + 1) * half] for s in range(ns)], axis=0)
    return yr, yi


def _hyena_kernel(x0_ref, x1_ref, v_ref, cw_ref, cb_ref, hb_ref, h_ref,
                  f1_ref, f1i_ref, f2_ref, f2i_ref, tc_ref, ts_ref, o_ref):
    consts = (f1_ref[...], f1i_ref[...], f2_ref[...], f2i_ref[...], tc_ref[...], ts_ref[...])
    g0, g1, v = [], [], []
    for half in range(2):
        g0.append(_short_conv_tile(x0_ref[:, half], cw_ref[0], cb_ref[0]))
        g1.append(_short_conv_tile(x1_ref[:, half], cw_ref[1], cb_ref[1]))
        v.append(_short_conv_tile(v_ref[:, half], cw_ref[2], cb_ref[2]))
    gates = (g0, g1)
    for o in range(HYENA_ORDER):
        yr, yi = _long_conv_pair(v[0], v[1], h_ref[o], *consts)
        v = [gates[o][0] * (yr + hb_ref[o] * v[0]), gates[o][1] * (yi + hb_ref[o] * v[1])]
    o_ref[:, 0] = v[0].astype(BF16)
    o_ref[:, 1] = v[1].astype(BF16)


def _hyena_latent(ut, cw, cb, hb, hspec, consts, *, nb, ct=32):
    seq = ut.shape[1] // nb
    nrow = seq // FFT_N2
    npair = nb // 2
    nct = HYENA_W // ct
    u5 = ut.reshape(3 * HYENA_W, 2, npair, nrow, FFT_N2)
    part = lambda p: pl.BlockSpec((ct, 2, None, nrow, FFT_N2), lambda c, j: (p * nct + c, 0, j, 0, 0))
    const2 = lambda c, j: (0, 0)
    cnames = ("f1", "f1i", "f2", "f2i", "tc", "ts")
    out = pl.pallas_call(
        _hyena_kernel,
        grid=(nct, npair),
        in_specs=[part(0), part(1), part(2),
                  pl.BlockSpec((3, 3, ct, 1, 1), lambda c, j: (0, 0, c, 0, 0)),
                  pl.BlockSpec((3, ct, 1, 1), lambda c, j: (0, c, 0, 0)),
                  pl.BlockSpec((HYENA_ORDER, ct, 1, 1), lambda c, j: (0, c, 0, 0)),
                  pl.BlockSpec((HYENA_ORDER, ct, FFT_N1, 2 * FFT_N2), lambda c, j: (0, c, 0, 0))]
                 + [pl.BlockSpec(consts[k].shape, const2) for k in cnames],
        out_specs=pl.BlockSpec((ct, 2, None, nrow, FFT_N2), lambda c, j: (c, 0, j, 0, 0)),
        out_shape=jax.ShapeDtypeStruct((HYENA_W, 2, npair, nrow, FFT_N2), BF16),
        compiler_params=_cparams("parallel", "arbitrary"),
        name="hyena",
    )(u5, u5, u5, cw, cb, hb, hspec, *[consts[k] for k in cnames])
    return out.reshape(HYENA_W, nb * seq)


def _hyena_ctx_kernel(x0_ref, x1_ref, v_ref, cw_ref, cb_ref, hb_ref, h_ref, cf_ref, ci_ref, o_ref, *, nb):
    seq = x0_ref.shape[1] // nb
    cc = x0_ref.shape[0]
    half = h_ref.shape[2] // 2

    def rows(ref, p):
        x = jnp.concatenate([ref[:, b * seq:(b + 1) * seq] for b in range(nb)], axis=0)
        lane = lax.broadcasted_iota(jnp.int32, x.shape, 1)
        prev = jnp.where(lane == 0, 0.0, pltpu.roll(x, 1, 1))
        nxt = jnp.where(lane == seq - 1, 0.0, pltpu.roll(x, seq - 1, 1))
        w = [jnp.concatenate([cw_ref[p, j]] * nb, axis=0) for j in range(3)]
        return prev * w[0] + x * w[1] + nxt * w[2] + jnp.concatenate([cb_ref[p]] * nb, axis=0)

    g = (rows(x0_ref, 0), rows(x1_ref, 1))
    v = rows(v_ref, 2)
    for o in range(HYENA_ORDER):
        spec = jnp.dot(v.astype(BF16), cf_ref[...], preferred_element_type=F32)
        xr, xi = spec[:, :half], spec[:, half:]
        hh = jnp.concatenate([h_ref[o]] * nb, axis=0)
        hr, hi = hh[:, :half], hh[:, half:]
        y = jnp.concatenate([xr * hr - xi * hi, xr * hi + xi * hr], axis=1).astype(BF16)
        conv = jnp.dot(y, ci_ref[...], preferred_element_type=F32)
        v = g[o] * (conv + jnp.concatenate([hb_ref[o]] * nb, axis=0) * v)
    for b in range(nb):
        o_ref[:, b * seq:(b + 1) * seq] = v[b * cc:(b + 1) * cc].astype(BF16)


def _hyena_ctx(ut, cw, cb, hb, hspec, cfwd, cinv, *, nb, cc=64):
    n = ut.shape[1]
    ncc = HYENA_W // cc
    part = lambda p: pl.BlockSpec((cc, n), lambda c: (p * ncc + c, 0))
    return pl.pallas_call(
        functools.partial(_hyena_ctx_kernel, nb=nb),
        grid=(ncc,),
        in_specs=[part(0), part(1), part(2),
                  pl.BlockSpec((3, 3, cc, 1), lambda c: (0, 0, c, 0)),
                  pl.BlockSpec((3, cc, 1), lambda c: (0, c, 0)),
                  pl.BlockSpec((HYENA_ORDER, cc, 1), lambda c: (0, c, 0)),
                  pl.BlockSpec((HYENA_ORDER, cc, hspec.shape[2]), lambda c: (0, c, 0)),
                  pl.BlockSpec(cfwd.shape, lambda c: (0, 0)),
                  pl.BlockSpec(cinv.shape, lambda c: (0, 0))],
        out_specs=pl.BlockSpec((cc, n), lambda c: (c, 0)),
        out_shape=jax.ShapeDtypeStruct((HYENA_W, n), BF16),
        compiler_params=_cparams("parallel"),
        name="hyena_ctx",
    )(ut, ut, ut, cw, cb, hb, hspec, cfwd, cinv)


def _merge_kernel(h_ref, ug_ref, gb_ref, yt_ref, att_ref, hp_ref, mp_ref, wo_ref, mods_ref, o_ref, *, tpb, base):
    row = base + pl.program_id(0) // tpb
    g = _sigmoid(ug_ref[...].astype(F32) + gb_ref[...])
    yh = lax.dot_general(yt_ref[...], hp_ref[...], (((0,), (0,)), ((), ())), preferred_element_type=F32)
    ya = lax.dot_general(att_ref[...], mp_ref[...], (((0,), (0,)), ((), ())), preferred_element_type=F32)
    mix = (g[:, :D_MODEL] * yh + g[:, D_MODEL:] * ya).astype(BF16)
    out = jnp.dot(mix, wo_ref[...], preferred_element_type=F32)
    o_ref[...] = h_ref[...] + _mod_row(mods_ref, row, 2) * out


def _merge(h, ug, gate_b, yt, att, hy_proj, mla_proj, w_out, mods, layer, *, tpb, base, tm=512):
    t = h.shape[0]
    const = lambda i: (0, 0)
    wmap = lambda i: (layer, 0, 0)
    return pl.pallas_call(
        functools.partial(_merge_kernel, tpb=tpb, base=base),
        grid=(t // tm,),
        in_specs=[pl.BlockSpec((tm, D_MODEL), lambda i: (i, 0)),
                  pl.BlockSpec((tm, 2 * D_MODEL), lambda i: (i, 0)),
                  pl.BlockSpec((None, 1, 2 * D_MODEL), wmap),
                  pl.BlockSpec((HYENA_W, tm), lambda i: (0, i)),
                  pl.BlockSpec((MLA_HEADS * V_HEAD, tm), lambda i: (0, i)),
                  pl.BlockSpec((None,) + hy_proj.shape[1:], wmap),
                  pl.BlockSpec((None,) + mla_proj.shape[1:], wmap),
                  pl.BlockSpec((None,) + w_out.shape[1:], wmap),
                  pl.BlockSpec((NMOD, 6 * D_MODEL), const)],
        out_specs=pl.BlockSpec((tm, D_MODEL), lambda i: (i, 0)),
        out_shape=jax.ShapeDtypeStruct((t, D_MODEL), F32),
        compiler_params=_cparams("parallel"),
        name="merge",
    )(h, ug, gate_b, yt, att, hy_proj, mla_proj, w_out, mods)


def _ffn_kernel(h_ref, g_ref, mods_ref, wg_ref, wu_ref, wo_ref, fg_ref, o_ref, f_scr, *, tpb, base, final):
    j = pl.program_id(1)
    row = base + pl.program_id(0) // tpb

    @pl.when(j == 0)
    def _():
        h = h_ref[...]
        f = _rms(h, g_ref[...]) * (1.0 + _mod_row(mods_ref, row, 4)) + _mod_row(mods_ref, row, 3)
        f_scr[...] = f.astype(BF16)
        o_ref[...] = h

    f = f_scr[...]
    act = _silu(jnp.dot(f, wg_ref[...], preferred_element_type=F32)) * jnp.dot(f, wu_ref[...], preferred_element_type=F32)
    y = jnp.dot(act.astype(BF16), wo_ref[...], preferred_element_type=F32)
    o_ref[...] += _mod_row(mods_ref, row, 5) * y

    if final:
        @pl.when(j == pl.num_programs(1) - 1)
        def _():
            o_ref[...] = _rms(o_ref[...], fg_ref[...])


def _ffn(h, g, mods, w_in, w_down, final_g, layer, *, tpb, base, final, tm=512, nchunk=2):
    t = h.shape[0]
    fc = D_FF // nchunk
    const = lambda i, j: (0, 0)
    return pl.pallas_call(
        functools.partial(_ffn_kernel, tpb=tpb, base=base, final=final),
        grid=(t // tm, nchunk),
        in_specs=[pl.BlockSpec((tm, D_MODEL), lambda i, j: (i, 0)),
                  pl.BlockSpec((None, 1, D_MODEL), lambda i, j: (layer, 0, 0)),
                  pl.BlockSpec((NMOD, 6 * D_MODEL), const),
                  pl.BlockSpec((None, D_MODEL, fc), lambda i, j: (layer, 0, j)),
                  pl.BlockSpec((None, D_MODEL, fc), lambda i, j: (layer, 0, nchunk + j)),
                  pl.BlockSpec((None, fc, D_MODEL), lambda i, j: (layer, j, 0)),
                  pl.BlockSpec((1, D_MODEL), const)],
        out_specs=pl.BlockSpec((tm, D_MODEL), lambda i, j: (i, 0)),
        out_shape=jax.ShapeDtypeStruct((t, D_MODEL), F32),
        scratch_shapes=[pltpu.VMEM((tm, D_MODEL), BF16)],
        compiler_params=_cparams("parallel", "arbitrary"),
        name="ffn",
    )(h, g, mods, w_in, w_in, w_down, final_g.reshape(1, -1))


def _dft_consts():
    n = FFT_N1 * FFT_N2
    k1 = np.arange(FFT_N1)
    th = 2 * np.pi * np.outer(k1, np.arange(FFT_N1)) / FFT_N1
    c, s = np.cos(th), np.sin(th)
    hn = FFT_N1 // 2
    f1 = np.block([[c[:, :hn], s[:, :hn]], [-s[:, :hn], c[:, :hn]]])
    f1h = np.concatenate([c, -s], axis=0)
    f1i = np.block([[c[:, :hn].T, -s[:, :hn].T], [s[:, :hn].T, c[:, :hn].T]]) / n
    ph = 2 * np.pi * np.outer(k1, np.arange(FFT_N2)) / n
    th2 = 2 * np.pi * np.outer(np.arange(FFT_N2), np.arange(FFT_N2)) / FFT_N2
    c2, s2 = np.cos(th2), np.sin(th2)
    f2 = np.block([[c2, -s2], [s2, c2]])
    f2i = np.block([[c2, s2], [-s2, c2]])
    return {"f1": jnp.asarray(f1, BF16), "f1h": jnp.asarray(f1h, F32), "f1i": jnp.asarray(f1i, BF16),
            "f2": jnp.asarray(f2, BF16), "f2i": jnp.asarray(f2i, BF16), "f2_f32": jnp.asarray(f2, F32),
            "tc": jnp.asarray(np.cos(ph), F32), "ts": jnp.asarray(np.sin(ph), F32)}


def _dense_dft(seq):
    n = 2 * seq
    th = 2 * np.pi * np.outer(np.arange(n), np.arange(n)) / n
    c, s = np.cos(th), np.sin(th)
    full = np.concatenate([c, -s], axis=1)
    inv = np.concatenate([c[:, :seq], -s[:, :seq]], axis=0) / n
    return jnp.asarray(full, F32), jnp.asarray(full[:seq], BF16), jnp.asarray(inv, BF16)


def _pos_features(seq):
    n = np.arange(2 * seq)
    lag = np.minimum(np.where(n < seq, n, 2 * seq - n), seq - 1)
    t = np.linspace(0.0, 1.0, seq, dtype=np.float32)[lag]
    bands = np.arange(1, N_BANDS + 1, dtype=np.float32)
    ang = (np.float32(2 * math.pi) * bands)[:, None] * t[None, :]
    zt = np.concatenate([t[None, :], np.cos(ang), np.sin(ang), np.zeros((7, 2 * seq), np.float32)], axis=0)
    return jnp.asarray(zt, F32), jnp.asarray(t.reshape(1, -1), F32)


def _rope_tables(seq):
    rows = seq // GRID_W
    row = np.repeat(np.arange(rows, dtype=np.float32), GRID_W)
    col = np.tile(np.arange(GRID_W, dtype=np.float32), rows)
    inv = (np.float32(ROPE_BASE) ** (-np.arange(0, AXIS_ROPE, 2, dtype=np.float32) / AXIS_ROPE)).astype(np.float32)
    ang = np.concatenate([row[:, None] * inv, col[:, None] * inv], axis=-1)
    cos, sin = np.cos(ang), np.sin(ang)
    ones = np.ones((seq, QK_NOPE), np.float32)
    pad = HEAD_PAD - QK_NOPE - QK_ROPE
    ctab = np.concatenate([ones, cos, cos, np.ones((seq, pad), np.float32)], axis=1)
    stab = np.concatenate([0 * ones, sin, sin, np.zeros((seq, pad), np.float32)], axis=1)
    return jnp.asarray(ctab, F32), jnp.asarray(stab, F32)


def _rot_cols(w):
    half = QK_ROPE // 2
    return jnp.concatenate([-w[..., half:], w[..., :half]], axis=-1)


def _pad_heads(nope, rope):
    lead = nope.shape[:-2]
    z = jnp.zeros(lead + (MLA_HEADS, HEAD_PAD - QK_NOPE - QK_ROPE), nope.dtype)
    return jnp.concatenate([nope, rope, z], axis=-1).reshape(lead + (MLA_HEADS * HEAD_PAD,))


def _prep_weights(w_in, w_uq, w_ukv):
    depth, d = w_in.shape[0], w_in.shape[1]
    w_hy_t = jnp.swapaxes(w_in[:, :, :OFF_Q], 1, 2).astype(BF16)
    kr_w = w_in[:, :, OFF_KR:OFF_G]
    zl = jnp.zeros((depth, d, ROPE_LANE), F32)
    zr = jnp.zeros((depth, d, HEAD_PAD - ROPE_LANE - QK_ROPE), F32)
    w_qkv = jnp.concatenate([w_in[:, :, OFF_Q:OFF_KR], zl, kr_w, zr, zl, _rot_cols(kr_w), zr], axis=2).astype(BF16)
    w_gate = w_in[:, :, OFF_G:].astype(BF16)
    uq = w_uq.reshape(depth, Q_LORA, MLA_HEADS, QK_NOPE + QK_ROPE)
    wq = jnp.concatenate([_pad_heads(uq[..., :QK_NOPE], uq[..., QK_NOPE:]),
                          _pad_heads(0 * uq[..., :QK_NOPE], _rot_cols(uq[..., QK_NOPE:]))], axis=2).astype(BF16)
    ukv = w_ukv.reshape(depth, KV_LORA, MLA_HEADS, QK_NOPE + V_HEAD)
    wk = _pad_heads(ukv[..., :QK_NOPE], jnp.zeros((depth, KV_LORA, MLA_HEADS, QK_ROPE), F32)).astype(BF16)
    wvt = jnp.swapaxes(ukv[..., QK_NOPE:].reshape(depth, KV_LORA, MLA_HEADS * V_HEAD), 1, 2).astype(BF16)
    return w_hy_t, w_qkv, w_gate, wq, wk, wvt


def kernel(x, c, ctx, c_ctx, ada_w, ada_b, norm1_g, norm2_g, w_in, gate_b, hy_conv_w, hy_conv_b,
           hy_filt_w1, hy_filt_b1, hy_filt_w2, hy_filt_b2, hy_filt_w3, hy_filt_freq, hy_bias, hy_proj,
           q_norm_g, kv_norm_g, w_uq, w_ukv, mla_proj, w_out, ffn_w_in, ffn_w_out, final_norm_g):
    nb, seq, d = x.shape
    lc = ctx.shape[1]
    depth = ada_w.shape[0]
    assert d == D_MODEL and seq == FFT_N1 * FFT_N2 // 2 and nb % 2 == 0 and nb < NMOD
    tm = 512
    tpb_lat = seq // tm
    tpb_ctx = 1 << 30

    cc = jnp.zeros((NMOD, d), F32).at[:nb].set(c).at[nb].set(c_ctx)
    mods_all = _ada(cc, ada_w, ada_b)

    consts = _dft_consts()
    cfull_c, cfwd_c, cinv_c = _dense_dft(lc)
    ctab, stab = _rope_tables(seq)
    one_tab = jnp.ones((tm, HEAD_PAD), F32)
    zero_tab = jnp.zeros((tm, HEAD_PAD), F32)
    dabs = jnp.asarray(np.abs(np.linspace(HYENA_MIN_DECAY, HYENA_MAX_DECAY, HYENA_W, dtype=np.float32)).reshape(HYENA_W, 1))
    zt_lat, tv_lat = _pos_features(seq)
    zt_ctx, tv_ctx = _pos_features(lc)

    w_hy_t, w_qkv, w_gate, wq, wk, wvt = _prep_weights(w_in, w_uq, w_ukv)
    hp, mp, wo = hy_proj.astype(BF16), mla_proj.astype(BF16), w_out.astype(BF16)
    w_ffi, w_ffo = ffn_w_in.astype(BF16), ffn_w_out.astype(BF16)
    qg = q_norm_g.reshape(depth, 1, Q_LORA)
    kvg = kv_norm_g.reshape(depth, 1, KV_LORA)
    gb = gate_b.reshape(depth, 1, 2 * D_MODEL)
    n1 = norm1_g
    n2 = norm2_g.reshape(depth, 1, D_MODEL)
    cw_all = jnp.transpose(hy_conv_w.reshape(depth, 3, 3, HYENA_W), (0, 2, 1, 3))
    cb_all = hy_conv_b.reshape(depth, 3, HYENA_W)

    h_lat = x.reshape(nb * seq, d)
    h_ctx = ctx.reshape(nb * lc, d)

    for i in range(depth):
        last = i == depth - 1
        mods = mods_all[i]
        cw, cb, hb = cw_all[i], cb_all[i], hy_bias[i]

        filt = (hy_filt_w1[i], hy_filt_b1[i], hy_filt_w2[i], hy_filt_b2[i], hy_filt_freq[i])
        hfull = _filt_taps(_filt_mlp(zt_lat, *filt), hy_filt_w3[i], dabs, tv_lat, seq)
        hspec = _filt_spec(hfull.reshape(HYENA_ORDER * HYENA_W, 2 * seq), consts)
        hspec = hspec.reshape(HYENA_ORDER, HYENA_W, FFT_N1, 2 * FFT_N2)

        uq_l, ug_l, ut_l = _in_proj(h_lat, n1[i], mods, w_qkv, w_gate, w_hy_t, i, tpb=tpb_lat, base=0)
        uq_c, ug_c, ut_c = _in_proj(h_ctx, n1[i], mods, w_qkv, w_gate, w_hy_t, i, tpb=tpb_ctx, base=nb)
        q_l, k_l, v_l = _qkv_up(uq_l, qg, kvg, wq, wk, wvt, ctab, stab, i, rope=True)
        q_c, k_c, v_c = _qkv_up(uq_c, qg, kvg, wq, wk, wvt, one_tab, zero_tab, i, rope=False)
        att_l = _attention(q_l, [(k_l, v_l, seq), (k_c, v_c, lc)], nb=nb)
        yt_l = _hyena_latent(ut_l, cw.reshape(3, 3, HYENA_W, 1, 1), cb.reshape(3, HYENA_W, 1, 1),
                             hb.reshape(HYENA_ORDER, HYENA_W, 1, 1), hspec, consts, nb=nb)
        h_lat = _merge(h_lat, ug_l, gb, yt_l, att_l, hp, mp, wo, mods, i, tpb=tpb_lat, base=0)

        h_lat = _ffn(h_lat, n2, mods, w_ffi, w_ffo, final_norm_g, i, tpb=tpb_lat, base=0, final=last)

        if not last:
            hfull_c = _filt_taps(_filt_mlp(zt_ctx, *filt), hy_filt_w3[i], dabs, tv_ctx, lc)
            hspec_c = _matmul_hi(hfull_c.reshape(HYENA_ORDER * HYENA_W, 2 * lc), cfull_c, tm=256)
            hspec_c = hspec_c.reshape(HYENA_ORDER, HYENA_W, 4 * lc)
            att_c = _attention(q_c, [(k_c, v_c, lc)], nb=nb)
            yt_c = _hyena_ctx(ut_c, cw.reshape(3, 3, HYENA_W, 1), cb.reshape(3, HYENA_W, 1),
                              hb.reshape(HYENA_ORDER, HYENA_W, 1), hspec_c, cfwd_c, cinv_c, nb=nb)
            h_ctx = _merge(h_ctx, ug_c, gb, yt_c, att_c, hp, mp, wo, mods, i, tpb=tpb_ctx, base=nb)
            h_ctx = _ffn(h_ctx, n2, mods, w_ffi, w_ffo, final_norm_g, i, tpb=tpb_ctx, base=nb, final=False)

    return h_lat.reshape(nb, seq, d)
```

```python
import functools
import math

import numpy as np
import jax
import jax.numpy as jnp
from jax import lax
from jax.experimental import pallas as pl
from jax.experimental.pallas import tpu as pltpu

F32 = jnp.float32
BF16 = jnp.bfloat16
HIGHEST = lax.Precision.HIGHEST

D_MODEL = 1024
GRID_W = 64
NORM_EPS = 1e-6

HYENA_W = 512
HYENA_ORDER = 2
N_BANDS = 16
FILTER_HIDDEN = 64
HYENA_TARGET = 1e-2
HYENA_MAX_DECAY = math.log(HYENA_TARGET) / 0.3
HYENA_MIN_DECAY = math.log(HYENA_TARGET) / 1.5

MLA_HEADS = 16
QK_NOPE = 64
QK_ROPE = 32
V_HEAD = 64
Q_LORA = 384
KV_LORA = 256
ROPE_BASE = 10000.0
AXIS_ROPE = QK_ROPE // 2
ATTN_SCALE = (QK_NOPE + QK_ROPE) ** -0.5
Q_SCALE = ATTN_SCALE * math.log2(math.e)
D_FF = 2816

HY_COLS = (HYENA_ORDER + 1) * HYENA_W
OFF_Q = HY_COLS
OFF_KV = OFF_Q + Q_LORA
OFF_KR = OFF_KV + KV_LORA
OFF_G = OFF_KR + QK_ROPE

HEAD_PAD = 128
ROPE_LANE = QK_NOPE
QKV_COLS = Q_LORA + KV_LORA + 2 * HEAD_PAD
HEADS_PER_STEP = 4
KEY_SPLIT = 1024
NMOD = 16

FFT_N1 = 64
FFT_N2 = 128
VMEM_LIMIT = 56 * 1024 * 1024


def _cparams(*sem):
    return pltpu.CompilerParams(dimension_semantics=sem, vmem_limit_bytes=VMEM_LIMIT)


def _rms(x, g):
    return x * lax.rsqrt(jnp.mean(x * x, axis=-1, keepdims=True) + NORM_EPS) * g


def _silu(x):
    return x * (1.0 / (1.0 + jnp.exp(-x)))


def _sigmoid(x):
    return 1.0 / (1.0 + jnp.exp(-x))


def _mod_row(mods_ref, row, k):
    return mods_ref[pl.ds(row, 1), k * D_MODEL:(k + 1) * D_MODEL]


def _ada_kernel(c_ref, w_ref, b_ref, o_ref):
    s = _silu(c_ref[...])
    o_ref[...] = jnp.dot(s, w_ref[...], preferred_element_type=F32, precision=HIGHEST) + b_ref[...]


def _ada(cc, ada_w, ada_b):
    depth = ada_w.shape[0]
    tn = D_MODEL
    return pl.pallas_call(
        _ada_kernel,
        grid=(depth, 6 * D_MODEL // tn),
        in_specs=[pl.BlockSpec((NMOD, D_MODEL), lambda l, j: (0, 0)),
                  pl.BlockSpec((None, D_MODEL, tn), lambda l, j: (l, 0, j)),
                  pl.BlockSpec((None, 1, tn), lambda l, j: (l, 0, j))],
        out_specs=pl.BlockSpec((None, NMOD, tn), lambda l, j: (l, 0, j)),
        out_shape=jax.ShapeDtypeStruct((depth, NMOD, 6 * D_MODEL), F32),
        compiler_params=_cparams("arbitrary", "arbitrary"),
        name="ada",
    )(cc, ada_w, ada_b.reshape(depth, 1, 6 * D_MODEL))


def _in_proj_kernel(h_ref, g_ref, mods_ref, wqkv_ref, wg_ref, wht_ref, uq_ref, ug_ref, ut_ref, *, tpb, base):
    row = base + pl.program_id(0) // tpb
    y = _rms(h_ref[...], g_ref[...])
    a = (y * (1.0 + _mod_row(mods_ref, row, 1)) + _mod_row(mods_ref, row, 0)).astype(BF16)
    uq_ref[...] = jnp.dot(a, wqkv_ref[...], preferred_element_type=F32).astype(BF16)
    ug_ref[...] = jnp.dot(a, wg_ref[...], preferred_element_type=F32).astype(BF16)
    ut_ref[...] = lax.dot_general(wht_ref[...], a, (((1,), (1,)), ((), ())), preferred_element_type=F32)


def _in_proj(h, g, mods, w_qkv, w_gate, w_hy_t, layer, *, tpb, base, tm=512):
    t = h.shape[0]
    const = lambda i: (0, 0)
    wmap = lambda i: (layer, 0, 0)
    return pl.pallas_call(
        functools.partial(_in_proj_kernel, tpb=tpb, base=base),
        grid=(t // tm,),
        in_specs=[pl.BlockSpec((tm, D_MODEL), lambda i: (i, 0)),
                  pl.BlockSpec((1, D_MODEL), const),
                  pl.BlockSpec((NMOD, 6 * D_MODEL), const),
                  pl.BlockSpec((None, D_MODEL, QKV_COLS), wmap),
                  pl.BlockSpec((None, D_MODEL, 2 * D_MODEL), wmap),
                  pl.BlockSpec((None, HY_COLS, D_MODEL), wmap)],
        out_specs=[pl.BlockSpec((tm, QKV_COLS), lambda i: (i, 0)),
                   pl.BlockSpec((tm, 2 * D_MODEL), lambda i: (i, 0)),
                   pl.BlockSpec((HY_COLS, tm), lambda i: (0, i))],
        out_shape=[jax.ShapeDtypeStruct((t, QKV_COLS), BF16),
                   jax.ShapeDtypeStruct((t, 2 * D_MODEL), BF16),
                   jax.ShapeDtypeStruct((HY_COLS, t), F32)],
        compiler_params=_cparams("parallel"),
        name="in_proj",
    )(h, g.reshape(1, D_MODEL), mods, w_qkv, w_gate, w_hy_t)


def _mm_hi_kernel(a_ref, w_ref, o_ref):
    o_ref[...] = jnp.dot(a_ref[...], w_ref[...], preferred_element_type=F32, precision=HIGHEST)


def _matmul_hi(a, w, tm):
    m, k = a.shape
    n = w.shape[1]
    return pl.pallas_call(
        _mm_hi_kernel,
        grid=(m // tm,),
        in_specs=[pl.BlockSpec((tm, k), lambda i: (i, 0)),
                  pl.BlockSpec((k, n), lambda i: (0, 0))],
        out_specs=pl.BlockSpec((tm, n), lambda i: (i, 0)),
        out_shape=jax.ShapeDtypeStruct((m, n), F32),
        compiler_params=_cparams("parallel"),
        name="matmul_hi",
    )(a, w)


def _qkv_kernel(u_ref, qg_ref, kvg_ref, wq_ref, wk_ref, wv_ref, ct_ref, st_ref, q_ref, k_ref, v_ref):
    u = u_ref[...].astype(F32)
    nq = _rms(u[:, :Q_LORA], qg_ref[...]).astype(BF16)
    nkv = _rms(u[:, Q_LORA:Q_LORA + KV_LORA], kvg_ref[...]).astype(BF16)
    ka = u[:, Q_LORA + KV_LORA:Q_LORA + KV_LORA + HEAD_PAD]
    kb = u[:, Q_LORA + KV_LORA + HEAD_PAD:]
    ct = ct_ref[...]
    st = st_ref[...]
    kr = ka * ct + kb * st
    qq = jnp.dot(nq, wq_ref[...], preferred_element_type=F32)
    kn = jnp.dot(nkv, wk_ref[...], preferred_element_type=F32)
    nh = MLA_HEADS * HEAD_PAD
    for h in range(MLA_HEADS):
        lo, hi = h * HEAD_PAD, (h + 1) * HEAD_PAD
        qh = qq[:, lo:hi] * ct + qq[:, nh + lo:nh + hi] * st
        q_ref[:, lo:hi] = (qh * Q_SCALE).astype(BF16)
        k_ref[:, lo:hi] = (kn[:, lo:hi] + kr).astype(BF16)
    v_ref[...] = lax.dot_general(wv_ref[...], nkv, (((1,), (1,)), ((), ())),
                                 preferred_element_type=F32).astype(BF16)


def _qkv_up(u, qg, kvg, wq, wk, wvt, ctab, stab, layer, *, rope, tm=512):
    t = u.shape[0]
    ntab = ctab.shape[0] // tm
    tab_map = (lambda i: (i % ntab, 0)) if rope else (lambda i: (0, 0))
    nh = MLA_HEADS * HEAD_PAD
    nv = MLA_HEADS * V_HEAD
    wmap = lambda i: (layer, 0, 0)
    return pl.pallas_call(
        _qkv_kernel,
        grid=(t // tm,),
        in_specs=[pl.BlockSpec((tm, QKV_COLS), lambda i: (i, 0)),
                  pl.BlockSpec((None, 1, Q_LORA), wmap),
                  pl.BlockSpec((None, 1, KV_LORA), wmap),
                  pl.BlockSpec((None, Q_LORA, 2 * nh), wmap),
                  pl.BlockSpec((None, KV_LORA, nh), wmap),
                  pl.BlockSpec((None, nv, KV_LORA), wmap),
                  pl.BlockSpec((tm, HEAD_PAD), tab_map),
                  pl.BlockSpec((tm, HEAD_PAD), tab_map)],
        out_specs=[pl.BlockSpec((tm, nh), lambda i: (i, 0)),
                   pl.BlockSpec((tm, nh), lambda i: (i, 0)),
                   pl.BlockSpec((nv, tm), lambda i: (0, i))],
        out_shape=[jax.ShapeDtypeStruct((t, nh), BF16),
                   jax.ShapeDtypeStruct((t, nh), BF16),
                   jax.ShapeDtypeStruct((nv, t), BF16)],
        compiler_params=_cparams("parallel"),
        name="qkv_up",
    )(u, qg, kvg, wq, wk, wvt, ctab, stab)


def _attn_kernel(*refs, lks, n_sub, tq):
    n_kv = len(lks)
    q_ref = refs[0]
    k_refs = refs[1:1 + n_kv]
    v_refs = refs[1 + n_kv:1 + 2 * n_kv]
    o_ref = refs[1 + 2 * n_kv]
    nt = (((1,), (1,)), ((), ()))
    units = [(j, h) for j in range(n_sub) for h in range(HEADS_PER_STEP)]

    def scores(j, h):
        q = q_ref[j * tq:(j + 1) * tq, h * HEAD_PAD:(h + 1) * HEAD_PAD]
        out = []
        for k_ref, lk in zip(k_refs, lks):
            step = min(lk, KEY_SPLIT)
            for c in range(0, lk, step):
                out.append(lax.dot_general(k_ref[c:c + step, h * HEAD_PAD:(h + 1) * HEAD_PAD], q, nt,
                                           preferred_element_type=F32))
        return out

    s_next = scores(*units[0])
    for idx, (j, h) in enumerate(units):
        s = s_next
        if idx + 1 < len(units):
            s_next = scores(*units[idx + 1])
        m = s[0].max(axis=0, keepdims=True)
        for si in s[1:]:
            m = jnp.maximum(m, si.max(axis=0, keepdims=True))
        p = [jnp.exp2(si - m) for si in s]
        l = p[0].sum(axis=0, keepdims=True)
        for pi in p[1:]:
            l = l + pi.sum(axis=0, keepdims=True)
        o, i = None, 0
        for v_ref, lk in zip(v_refs, lks):
            step = min(lk, KEY_SPLIT)
            for c in range(0, lk, step):
                d = jnp.dot(v_ref[h * V_HEAD:(h + 1) * V_HEAD, c:c + step], p[i].astype(BF16),
                            preferred_element_type=F32)
                o = d if o is None else o + d
                i += 1
        o_ref[h * V_HEAD:(h + 1) * V_HEAD, j * tq:(j + 1) * tq] = (o / l).astype(BF16)


def _attention(q, kv_sets, *, nb, tq=256, n_sub=4):
    t = q.shape[0]
    lq = t // nb
    n_sub = min(n_sub, lq // tq)
    tb = tq * n_sub
    nq = lq // tb
    groups = MLA_HEADS // HEADS_PER_STEP
    qw = HEADS_PER_STEP * HEAD_PAD
    vw = HEADS_PER_STEP * V_HEAD
    in_specs = [pl.BlockSpec((tb, qw), lambda b, g, i: (b * nq + i, g))]
    k_specs, v_specs, ks, vs = [], [], [], []
    for k, vt, lk in kv_sets:
        k_specs.append(pl.BlockSpec((lk, qw), lambda b, g, i: (b, g)))
        v_specs.append(pl.BlockSpec((vw, lk), lambda b, g, i: (g, b)))
        ks.append(k)
        vs.append(vt)
    lks = tuple(lk for _, _, lk in kv_sets)
    return pl.pallas_call(
        functools.partial(_attn_kernel, lks=lks, n_sub=n_sub, tq=tq),
        grid=(nb, groups, nq),
        in_specs=in_specs + k_specs + v_specs,
        out_specs=pl.BlockSpec((vw, tb), lambda b, g, i: (g, b * nq + i)),
        out_shape=jax.ShapeDtypeStruct((MLA_HEADS * V_HEAD, t), BF16),
        compiler_params=_cparams("parallel", "parallel", "arbitrary"),
        name="attention",
    )(q, *ks, *vs)


def _filt_mlp_kernel(z_ref, w1_ref, b1_ref, w2_ref, b2_ref, f0_ref, f1_ref, o_ref):
    h = jnp.dot(w1_ref[...], z_ref[...], preferred_element_type=F32, precision=HIGHEST) + b1_ref[...]
    h = jnp.sin(f0_ref[...] * h)
    h = jnp.dot(w2_ref[...], h, preferred_element_type=F32, precision=HIGHEST) + b2_ref[...]
    o_ref[...] = jnp.sin(f1_ref[...] * h)


def _filt_mlp(zt, w1, b1, w2, b2, freq):
    kz, n = zt.shape
    w1t = jnp.zeros((FILTER_HIDDEN, kz), F32).at[:, :w1.shape[0]].set(w1.T)
    col = lambda a: a.reshape(FILTER_HIDDEN, 1)
    args = (zt, w1t, col(b1), w2.T, col(b2), col(freq[0]), col(freq[1]))
    return pl.pallas_call(
        _filt_mlp_kernel,
        grid=(1,),
        in_specs=[pl.BlockSpec(a.shape, lambda i: (0, 0)) for a in args],
        out_specs=pl.BlockSpec((FILTER_HIDDEN, n), lambda i: (0, 0)),
        out_shape=jax.ShapeDtypeStruct((FILTER_HIDDEN, n), F32),
        compiler_params=_cparams("arbitrary"),
        name="filt_mlp",
    )(*args)


def _filt_taps_kernel(h_ref, w3_ref, d_ref, t_ref, o_ref, *, seq):
    n = h_ref.shape[1]
    pos = lax.broadcasted_iota(jnp.int32, (1, n), 1)
    win = jnp.exp(-t_ref[...] * d_ref[...])
    h = h_ref[...]
    for o in range(HYENA_ORDER):
        fwd = jnp.dot(w3_ref[0, o], h, preferred_element_type=F32, precision=HIGHEST)
        bwd = jnp.dot(w3_ref[1, o], h, preferred_element_type=F32, precision=HIGHEST)
        full = jnp.where(pos < seq, fwd, jnp.where(pos > seq, bwd, 0.0)) * win
        o_ref[o] = full / jnp.sum(jnp.abs(full), axis=1, keepdims=True)


def _filt_taps(h2t, w3, dabs, tvec, seq, cf=64):
    n = h2t.shape[1]
    w3t = w3.T.reshape(2, HYENA_ORDER, HYENA_W, FILTER_HIDDEN)
    return pl.pallas_call(
        functools.partial(_filt_taps_kernel, seq=seq),
        grid=(HYENA_W // cf,),
        in_specs=[pl.BlockSpec((FILTER_HIDDEN, n), lambda c: (0, 0)),
                  pl.BlockSpec((2, HYENA_ORDER, cf, FILTER_HIDDEN), lambda c: (0, 0, c, 0)),
                  pl.BlockSpec((cf, 1), lambda c: (c, 0)),
                  pl.BlockSpec((1, n), lambda c: (0, 0))],
        out_specs=pl.BlockSpec((HYENA_ORDER, cf, n), lambda c: (0, c, 0)),
        out_shape=jax.ShapeDtypeStruct((HYENA_ORDER, HYENA_W, n), F32),
        compiler_params=_cparams("parallel"),
        name="filt_taps",
    )(h2t, w3t, dabs, tvec)


def _filt_spec_kernel(x_ref, f1_ref, tc_ref, ts_ref, f2_ref, o_ref):
    ns = x_ref.shape[0]
    x = jnp.concatenate([x_ref[s] for s in range(ns)], axis=1).astype(BF16)
    a = jnp.dot(f1_ref[...], x, preferred_element_type=F32)
    tc = tc_ref[...]
    ts = ts_ref[...]
    rows = []
    for s in range(ns):
        ar = a[:FFT_N1, s * FFT_N2:(s + 1) * FFT_N2]
        ai = a[FFT_N1:, s * FFT_N2:(s + 1) * FFT_N2]
        rows.append(jnp.concatenate([ar * tc + ai * ts, ai * tc - ar * ts], axis=1))
    b = jnp.concatenate(rows, axis=0).astype(BF16)
    z = jnp.dot(b, f2_ref[...], preferred_element_type=F32)
    o_ref[...] = z.reshape(ns, FFT_N1, 2 * FFT_N2)


def _filt_spec(hfull, consts, ns=16):
    r = hfull.shape[0]
    x = hfull.reshape(r, FFT_N1, FFT_N2)
    f1h, tc, ts, f2 = consts["f1h"], consts["tc"], consts["ts"], consts["f2"]
    const = lambda i: (0, 0)
    return pl.pallas_call(
        _filt_spec_kernel,
        grid=(r // ns,),
        in_specs=[pl.BlockSpec((ns, FFT_N1, FFT_N2), lambda i: (i, 0, 0)),
                  pl.BlockSpec(f1h.shape, const), pl.BlockSpec(tc.shape, const),
                  pl.BlockSpec(ts.shape, const), pl.BlockSpec(f2.shape, const)],
        out_specs=pl.BlockSpec((ns, FFT_N1, 2 * FFT_N2), lambda i: (i, 0, 0)),
        out_shape=jax.ShapeDtypeStruct((r, FFT_N1, 2 * FFT_N2), F32),
        compiler_params=_cparams("parallel"),
        name="filt_spec",
    )(x, f1h, tc, ts, f2)


def _short_conv_tile(x, w, b):
    nrow = x.shape[1]
    lane = lax.broadcasted_iota(jnp.int32, x.shape, 2)
    row = lax.broadcasted_iota(jnp.int32, x.shape, 1)
    r1 = pltpu.roll(x, 1, 2)
    r2 = pltpu.roll(r1, 1, 1)
    prev = jnp.where(lane == 0, jnp.where(row == 0, 0.0, r2), r1)
    l1 = pltpu.roll(x, FFT_N2 - 1, 2)
    l2 = pltpu.roll(l1, nrow - 1, 1)
    nxt = jnp.where(lane == FFT_N2 - 1, jnp.where(row == nrow - 1, 0.0, l2), l1)
    return prev * w[0] + x * w[1] + nxt * w[2] + b


def _long_conv_pair(vr, vi, h, f1, f1i, f2, f2i, tc, ts):
    ns = vr.shape[0]
    half = FFT_N2
    x = jnp.concatenate([jnp.concatenate([vr[s] for s in range(ns)], axis=1),
                         jnp.concatenate([vi[s] for s in range(ns)], axis=1)], axis=0).astype(BF16)
    a = jnp.dot(f1, x, preferred_element_type=F32)
    rows = []
    for s in range(ns):
        ar = a[:FFT_N1, s * half:(s + 1) * half]
        ai = a[FFT_N1:, s * half:(s + 1) * half]
        rows.append(jnp.concatenate([ar * tc + ai * ts, ai * tc - ar * ts], axis=1))
    b = jnp.concatenate(rows, axis=0).astype(BF16)
    z = jnp.dot(b, f2, preferred_element_type=F32).reshape(ns, FFT_N1, 2 * half)
    zr, zi = z[:, :, :half], z[:, :, half:]
    hr, hi = h[:, :, :half], h[:, :, half:]
    w = jnp.concatenate([zr * hr - zi * hi, zr * hi + zi * hr], axis=2)
    w = w.reshape(ns * FFT_N1, 2 * half).astype(BF16)
    c = jnp.dot(w, f2i, preferred_element_type=F32).reshape(ns, FFT_N1, 2 * half)
    cr, ci = c[:, :, :half], c[:, :, half:]
    dr = cr * tc - ci * ts
    di = cr * ts + ci * tc
    d = jnp.concatenate([jnp.concatenate([dr[s] for s in range(ns)], axis=1),
                         jnp.concatenate([di[s] for s in range(ns)], axis=1)], axis=0).astype(BF16)
    y = jnp.dot(f1i, d, preferred_element_type=F32)
    nrow = y.shape[0] // 2
    yr = jnp.stack([y[:nrow, s * half:(s + 1) * half] for s in range(ns)], axis=0)
    yi = jnp.stack([y[nrow:, s * half:(s + 1) * half] for s in range(ns)], axis=0)
    return yr, yi


def _hyena_kernel(x0_ref, x1_ref, v_ref, cw_ref, cb_ref, hb_ref, h_ref,
                  f1_ref, f1i_ref, f2_ref, f2i_ref, tc_ref, ts_ref, o_ref):
    consts = (f1_ref[...], f1i_ref[...], f2_ref[...], f2i_ref[...], tc_ref[...], ts_ref[...])
    g0, g1, v = [], [], []
    for half in range(2):
        g0.append(_short_conv_tile(x0_ref[:, half], cw_ref[0], cb_ref[0]))
        g1.append(_short_conv_tile(x1_ref[:, half], cw_ref[1], cb_ref[1]))
        v.append(_short_conv_tile(v_ref[:, half], cw_ref[2], cb_ref[2]))
    gates = (g0, g1)
    for o in range(HYENA_ORDER):
        yr, yi = _long_conv_pair(v[0], v[1], h_ref[o], *consts)
        v = [gates[o][0] * (yr + hb_ref[o] * v[0]), gates[o][1] * (yi + hb_ref[o] * v[1])]
    o_ref[:, 0] = v[0].astype(BF16)
    o_ref[:, 1] = v[1].astype(BF16)


def _hyena_latent(ut, cw, cb, hb, hspec, consts, *, nb, ct=32):
    seq = ut.shape[1] // nb
    nrow = seq // FFT_N2
    npair = nb // 2
    nct = HYENA_W // ct
    u5 = ut.reshape(3 * HYENA_W, 2, npair, nrow, FFT_N2)
    part = lambda p: pl.BlockSpec((ct, 2, None, nrow, FFT_N2), lambda c, j: (p * nct + c, 0, j, 0, 0))
    const2 = lambda c, j: (0, 0)
    cnames = ("f1", "f1i", "f2", "f2i", "tc", "ts")
    out = pl.pallas_call(
        _hyena_kernel,
        grid=(nct, npair),
        in_specs=[part(0), part(1), part(2),
                  pl.BlockSpec((3, 3, ct, 1, 1), lambda c, j: (0, 0, c, 0, 0)),
                  pl.BlockSpec((3, ct, 1, 1), lambda c, j: (0, c, 0, 0)),
                  pl.BlockSpec((HYENA_ORDER, ct, 1, 1), lambda c, j: (0, c, 0, 0)),
                  pl.BlockSpec((HYENA_ORDER, ct, FFT_N1, 2 * FFT_N2), lambda c, j: (0, c, 0, 0))]
                 + [pl.BlockSpec(consts[k].shape, const2) for k in cnames],
        out_specs=pl.BlockSpec((ct, 2, None, nrow, FFT_N2), lambda c, j: (c, 0, j, 0, 0)),
        out_shape=jax.ShapeDtypeStruct((HYENA_W, 2, npair, nrow, FFT_N2), BF16),
        compiler_params=_cparams("parallel", "arbitrary"),
        name="hyena",
    )(u5, u5, u5, cw, cb, hb, hspec, *[consts[k] for k in cnames])
    return out.reshape(HYENA_W, nb * seq)


def _hyena_ctx_kernel(x0_ref, x1_ref, v_ref, cw_ref, cb_ref, hb_ref, h_ref, cf_ref, ci_ref, o_ref, *, nb):
    seq = x0_ref.shape[1] // nb
    cc = x0_ref.shape[0]
    half = h_ref.shape[2] // 2

    def rows(ref, p):
        x = jnp.concatenate([ref[:, b * seq:(b + 1) * seq] for b in range(nb)], axis=0)
        lane = lax.broadcasted_iota(jnp.int32, x.shape, 1)
        prev = jnp.where(lane == 0, 0.0, pltpu.roll(x, 1, 1))
        nxt = jnp.where(lane == seq - 1, 0.0, pltpu.roll(x, seq - 1, 1))
        w = [jnp.concatenate([cw_ref[p, j]] * nb, axis=0) for j in range(3)]
        return prev * w[0] + x * w[1] + nxt * w[2] + jnp.concatenate([cb_ref[p]] * nb, axis=0)

    g = (rows(x0_ref, 0), rows(x1_ref, 1))
    v = rows(v_ref, 2)
    for o in range(HYENA_ORDER):
        spec = jnp.dot(v.astype(BF16), cf_ref[...], preferred_element_type=F32)
        xr, xi = spec[:, :half], spec[:, half:]
        hh = jnp.concatenate([h_ref[o]] * nb, axis=0)
        hr, hi = hh[:, :half], hh[:, half:]
        y = jnp.concatenate([xr * hr - xi * hi, xr * hi + xi * hr], axis=1).astype(BF16)
        conv = jnp.dot(y, ci_ref[...], preferred_element_type=F32)
        v = g[o] * (conv + jnp.concatenate([hb_ref[o]] * nb, axis=0) * v)
    for b in range(nb):
        o_ref[:, b * seq:(b + 1) * seq] = v[b * cc:(b + 1) * cc].astype(BF16)


def _hyena_ctx(ut, cw, cb, hb, hspec, cfwd, cinv, *, nb, cc=64):
    n = ut.shape[1]
    ncc = HYENA_W // cc
    part = lambda p: pl.BlockSpec((cc, n), lambda c: (p * ncc + c, 0))
    return pl.pallas_call(
        functools.partial(_hyena_ctx_kernel, nb=nb),
        grid=(ncc,),
        in_specs=[part(0), part(1), part(2),
                  pl.BlockSpec((3, 3, cc, 1), lambda c: (0, 0, c, 0)),
                  pl.BlockSpec((3, cc, 1), lambda c: (0, c, 0)),
                  pl.BlockSpec((HYENA_ORDER, cc, 1), lambda c: (0, c, 0)),
                  pl.BlockSpec((HYENA_ORDER, cc, hspec.shape[2]), lambda c: (0, c, 0)),
                  pl.BlockSpec(cfwd.shape, lambda c: (0, 0)),
                  pl.BlockSpec(cinv.shape, lambda c: (0, 0))],
        out_specs=pl.BlockSpec((cc, n), lambda c: (c, 0)),
        out_shape=jax.ShapeDtypeStruct((HYENA_W, n), BF16),
        compiler_params=_cparams("parallel"),
        name="hyena_ctx",
    )(ut, ut, ut, cw, cb, hb, hspec, cfwd, cinv)


def _merge_kernel(h_ref, ug_ref, gb_ref, yt_ref, att_ref, hp_ref, mp_ref, wo_ref, mods_ref, o_ref, *, tpb, base):
    row = base + pl.program_id(0) // tpb
    g = _sigmoid(ug_ref[...].astype(F32) + gb_ref[...])
    yh = lax.dot_general(yt_ref[...], hp_ref[...], (((0,), (0,)), ((), ())), preferred_element_type=F32)
    ya = lax.dot_general(att_ref[...], mp_ref[...], (((0,), (0,)), ((), ())), preferred_element_type=F32)
    mix = (g[:, :D_MODEL] * yh + g[:, D_MODEL:] * ya).astype(BF16)
    out = jnp.dot(mix, wo_ref[...], preferred_element_type=F32)
    o_ref[...] = h_ref[...] + _mod_row(mods_ref, row, 2) * out


def _merge(h, ug, gate_b, yt, att, hy_proj, mla_proj, w_out, mods, layer, *, tpb, base, tm=512):
    t = h.shape[0]
    const = lambda i: (0, 0)
    wmap = lambda i: (layer, 0, 0)
    return pl.pallas_call(
        functools.partial(_merge_kernel, tpb=tpb, base=base),
        grid=(t // tm,),
        in_specs=[pl.BlockSpec((tm, D_MODEL), lambda i: (i, 0)),
                  pl.BlockSpec((tm, 2 * D_MODEL), lambda i: (i, 0)),
                  pl.BlockSpec((None, 1, 2 * D_MODEL), wmap),
                  pl.BlockSpec((HYENA_W, tm), lambda i: (0, i)),
                  pl.BlockSpec((MLA_HEADS * V_HEAD, tm), lambda i: (0, i)),
                  pl.BlockSpec((None,) + hy_proj.shape[1:], wmap),
                  pl.BlockSpec((None,) + mla_proj.shape[1:], wmap),
                  pl.BlockSpec((None,) + w_out.shape[1:], wmap),
                  pl.BlockSpec((NMOD, 6 * D_MODEL), const)],
        out_specs=pl.BlockSpec((tm, D_MODEL), lambda i: (i, 0)),
        out_shape=jax.ShapeDtypeStruct((t, D_MODEL), F32),
        compiler_params=_cparams("parallel"),
        name="merge",
    )(h, ug, gate_b, yt, att, hy_proj, mla_proj, w_out, mods)


FF_CHUNK = 256


def _ffn_kernel(h_ref, g_ref, mods_ref, wi_ref, wo_ref, fg_ref, o_ref, *, tpb, base, final):
    row = base + pl.program_id(0) // tpb
    h = h_ref[...]
    f = (_rms(h, g_ref[...]) * (1.0 + _mod_row(mods_ref, row, 4)) + _mod_row(mods_ref, row, 3)).astype(BF16)
    y = None
    for c in range(0, D_FF, FF_CHUNK):
        gate = jnp.dot(f, wi_ref[:, c:c + FF_CHUNK], preferred_element_type=F32)
        up = jnp.dot(f, wi_ref[:, D_FF + c:D_FF + c + FF_CHUNK], preferred_element_type=F32)
        act = (_silu(gate) * up).astype(BF16)
        d = jnp.dot(act, wo_ref[c:c + FF_CHUNK, :], preferred_element_type=F32)
        y = d if y is None else y + d
    out = h + _mod_row(mods_ref, row, 5) * y
    o_ref[...] = _rms(out, fg_ref[...]) if final else out


def _ffn(h, g, mods, w_in, w_down, final_g, layer, *, tpb, base, final, tm=512):
    t = h.shape[0]
    const = lambda i: (0, 0)
    wmap = lambda i: (layer, 0, 0)
    return pl.pallas_call(
        functools.partial(_ffn_kernel, tpb=tpb, base=base, final=final),
        grid=(t // tm,),
        in_specs=[pl.BlockSpec((tm, D_MODEL), lambda i: (i, 0)),
                  pl.BlockSpec((None, 1, D_MODEL), wmap),
                  pl.BlockSpec((NMOD, 6 * D_MODEL), const),
                  pl.BlockSpec((None, D_MODEL, 2 * D_FF), wmap),
                  pl.BlockSpec((None, D_FF, D_MODEL), wmap),
                  pl.BlockSpec((1, D_MODEL), const)],
        out_specs=pl.BlockSpec((tm, D_MODEL), lambda i: (i, 0)),
        out_shape=jax.ShapeDtypeStruct((t, D_MODEL), F32),
        compiler_params=_cparams("parallel"),
        name="ffn",
    )(h, g, mods, w_in, w_down, final_g.reshape(1, -1))


def _dft_consts():
    n = FFT_N1 * FFT_N2
    k1 = np.arange(FFT_N1)
    th = 2 * np.pi * np.outer(k1, np.arange(FFT_N1)) / FFT_N1
    c, s = np.cos(th), np.sin(th)
    hn = FFT_N1 // 2
    f1 = np.block([[c[:, :hn], s[:, :hn]], [-s[:, :hn], c[:, :hn]]])
    f1h = np.concatenate([c, -s], axis=0)
    f1i = np.block([[c[:, :hn].T, -s[:, :hn].T], [s[:, :hn].T, c[:, :hn].T]]) / n
    ph = 2 * np.pi * np.outer(k1, np.arange(FFT_N2)) / n
    th2 = 2 * np.pi * np.outer(np.arange(FFT_N2), np.arange(FFT_N2)) / FFT_N2
    c2, s2 = np.cos(th2), np.sin(th2)
    f2 = np.block([[c2, -s2], [s2, c2]])
    f2i = np.block([[c2, s2], [-s2, c2]])
    return {"f1": jnp.asarray(f1, BF16), "f1h": jnp.asarray(f1h, BF16), "f1i": jnp.asarray(f1i, BF16),
            "f2": jnp.asarray(f2, BF16), "f2i": jnp.asarray(f2i, BF16),
            "tc": jnp.asarray(np.cos(ph), F32), "ts": jnp.asarray(np.sin(ph), F32)}


def _dense_dft(seq):
    n = 2 * seq
    th = 2 * np.pi * np.outer(np.arange(n), np.arange(n)) / n
    c, s = np.cos(th), np.sin(th)
    full = np.concatenate([c, -s], axis=1)
    inv = np.concatenate([c[:, :seq], -s[:, :seq]], axis=0) / n
    return jnp.asarray(full, F32), jnp.asarray(full[:seq], BF16), jnp.asarray(inv, BF16)


def _pos_features(seq):
    n = np.arange(2 * seq)
    lag = np.minimum(np.where(n < seq, n, 2 * seq - n), seq - 1)
    t = np.linspace(0.0, 1.0, seq, dtype=np.float32)[lag]
    bands = np.arange(1, N_BANDS + 1, dtype=np.float32)
    ang = (np.float32(2 * math.pi) * bands)[:, None] * t[None, :]
    zt = np.concatenate([t[None, :], np.cos(ang), np.sin(ang), np.zeros((7, 2 * seq), np.float32)], axis=0)
    return jnp.asarray(zt, F32), jnp.asarray(t.reshape(1, -1), F32)


def _rope_tables(seq):
    rows = seq // GRID_W
    row = np.repeat(np.arange(rows, dtype=np.float32), GRID_W)
    col = np.tile(np.arange(GRID_W, dtype=np.float32), rows)
    inv = (np.float32(ROPE_BASE) ** (-np.arange(0, AXIS_ROPE, 2, dtype=np.float32) / AXIS_ROPE)).astype(np.float32)
    ang = np.concatenate([row[:, None] * inv, col[:, None] * inv], axis=-1)
    cos, sin = np.cos(ang), np.sin(ang)
    ones = np.ones((seq, QK_NOPE), np.float32)
    pad = HEAD_PAD - QK_NOPE - QK_ROPE
    ctab = np.concatenate([ones, cos, cos, np.ones((seq, pad), np.float32)], axis=1)
    stab = np.concatenate([0 * ones, sin, sin, np.zeros((seq, pad), np.float32)], axis=1)
    return jnp.asarray(ctab, F32), jnp.asarray(stab, F32)


def _rot_cols(w):
    half = QK_ROPE // 2
    return jnp.concatenate([-w[..., half:], w[..., :half]], axis=-1)


def _pad_heads(nope, rope):
    lead = nope.shape[:-2]
    z = jnp.zeros(lead + (MLA_HEADS, HEAD_PAD - QK_NOPE - QK_ROPE), nope.dtype)
    return jnp.concatenate([nope, rope, z], axis=-1).reshape(lead + (MLA_HEADS * HEAD_PAD,))


def _prep_weights(w_in, w_uq, w_ukv):
    depth, d = w_in.shape[0], w_in.shape[1]
    w_hy_t = jnp.swapaxes(w_in[:, :, :OFF_Q], 1, 2).astype(BF16)
    kr_w = w_in[:, :, OFF_KR:OFF_G]
    zl = jnp.zeros((depth, d, ROPE_LANE), F32)
    zr = jnp.zeros((depth, d, HEAD_PAD - ROPE_LANE - QK_ROPE), F32)
    w_qkv = jnp.concatenate([w_in[:, :, OFF_Q:OFF_KR], zl, kr_w, zr, zl, _rot_cols(kr_w), zr], axis=2).astype(BF16)
    w_gate = w_in[:, :, OFF_G:].astype(BF16)
    uq = w_uq.reshape(depth, Q_LORA, MLA_HEADS, QK_NOPE + QK_ROPE)
    wq = jnp.concatenate([_pad_heads(uq[..., :QK_NOPE], uq[..., QK_NOPE:]),
                          _pad_heads(0 * uq[..., :QK_NOPE], _rot_cols(uq[..., QK_NOPE:]))], axis=2).astype(BF16)
    ukv = w_ukv.reshape(depth, KV_LORA, MLA_HEADS, QK_NOPE + V_HEAD)
    wk = _pad_heads(ukv[..., :QK_NOPE], jnp.zeros((depth, KV_LORA, MLA_HEADS, QK_ROPE), F32)).astype(BF16)
    wvt = jnp.swapaxes(ukv[..., QK_NOPE:].reshape(depth, KV_LORA, MLA_HEADS * V_HEAD), 1, 2).astype(BF16)
    return w_hy_t, w_qkv, w_gate, wq, wk, wvt


def kernel(x, c, ctx, c_ctx, ada_w, ada_b, norm1_g, norm2_g, w_in, gate_b, hy_conv_w, hy_conv_b,
           hy_filt_w1, hy_filt_b1, hy_filt_w2, hy_filt_b2, hy_filt_w3, hy_filt_freq, hy_bias, hy_proj,
           q_norm_g, kv_norm_g, w_uq, w_ukv, mla_proj, w_out, ffn_w_in, ffn_w_out, final_norm_g):
    nb, seq, d = x.shape
    lc = ctx.shape[1]
    depth = ada_w.shape[0]
    assert d == D_MODEL and seq == FFT_N1 * FFT_N2 // 2 and nb % 2 == 0 and nb < NMOD
    tm = 512
    tpb_lat = seq // tm
    tpb_ctx = 1 << 30

    cc = jnp.zeros((NMOD, d), F32).at[:nb].set(c).at[nb].set(c_ctx)
    mods_all = _ada(cc, ada_w, ada_b)

    consts = _dft_consts()
    cfull_c, cfwd_c, cinv_c = _dense_dft(lc)
    ctab, stab = _rope_tables(seq)
    one_tab = jnp.ones((tm, HEAD_PAD), F32)
    zero_tab = jnp.zeros((tm, HEAD_PAD), F32)
    dabs = jnp.asarray(np.abs(np.linspace(HYENA_MIN_DECAY, HYENA_MAX_DECAY, HYENA_W, dtype=np.float32)).reshape(HYENA_W, 1))
    zt_lat, tv_lat = _pos_features(seq)
    zt_ctx, tv_ctx = _pos_features(lc)

    w_hy_t, w_qkv, w_gate, wq, wk, wvt = _prep_weights(w_in, w_uq, w_ukv)
    hp, mp, wo = hy_proj.astype(BF16), mla_proj.astype(BF16), w_out.astype(BF16)
    w_ffi, w_ffo = ffn_w_in.astype(BF16), ffn_w_out.astype(BF16)
    qg = q_norm_g.reshape(depth, 1, Q_LORA)
    kvg = kv_norm_g.reshape(depth, 1, KV_LORA)
    gb = gate_b.reshape(depth, 1, 2 * D_MODEL)
    n1 = norm1_g
    n2 = norm2_g.reshape(depth, 1, D_MODEL)
    cw_all = jnp.transpose(hy_conv_w.reshape(depth, 3, 3, HYENA_W), (0, 2, 1, 3))
    cb_all = hy_conv_b.reshape(depth, 3, HYENA_W)

    h_lat = x.reshape(nb * seq, d)
    h_ctx = ctx.reshape(nb * lc, d)

    for i in range(depth):
        last = i == depth - 1
        mods = mods_all[i]
        cw, cb, hb = cw_all[i], cb_all[i], hy_bias[i]

        filt = (hy_filt_w1[i], hy_filt_b1[i], hy_filt_w2[i], hy_filt_b2[i], hy_filt_freq[i])
        hfull = _filt_taps(_filt_mlp(zt_lat, *filt), hy_filt_w3[i], dabs, tv_lat, seq)
        hspec = _filt_spec(hfull.reshape(HYENA_ORDER * HYENA_W, 2 * seq), consts)
        hspec = hspec.reshape(HYENA_ORDER, HYENA_W, FFT_N1, 2 * FFT_N2)

        uq_l, ug_l, ut_l = _in_proj(h_lat, n1[i], mods, w_qkv, w_gate, w_hy_t, i, tpb=tpb_lat, base=0)
        uq_c, ug_c, ut_c = _in_proj(h_ctx, n1[i], mods, w_qkv, w_gate, w_hy_t, i, tpb=tpb_ctx, base=nb)
        q_l, k_l, v_l = _qkv_up(uq_l, qg, kvg, wq, wk, wvt, ctab, stab, i, rope=True)
        q_c, k_c, v_c = _qkv_up(uq_c, qg, kvg, wq, wk, wvt, one_tab, zero_tab, i, rope=False)
        att_l = _attention(q_l, [(k_l, v_l, seq), (k_c, v_c, lc)], nb=nb)
        yt_l = _hyena_latent(ut_l, cw.reshape(3, 3, HYENA_W, 1, 1), cb.reshape(3, HYENA_W, 1, 1),
                             hb.reshape(HYENA_ORDER, HYENA_W, 1, 1), hspec, consts, nb=nb)
        h_lat = _merge(h_lat, ug_l, gb, yt_l, att_l, hp, mp, wo, mods, i, tpb=tpb_lat, base=0)

        h_lat = _ffn(h_lat, n2, mods, w_ffi, w_ffo, final_norm_g, i, tpb=tpb_lat, base=0, final=last)

        if not last:
            hfull_c = _filt_taps(_filt_mlp(zt_ctx, *filt), hy_filt_w3[i], dabs, tv_ctx, lc)
            hspec_c = _matmul_hi(hfull_c.reshape(HYENA_ORDER * HYENA_W, 2 * lc), cfull_c, tm=256)
            hspec_c = hspec_c.reshape(HYENA_ORDER, HYENA_W, 4 * lc)
            att_c = _attention(q_c, [(k_c, v_c, lc)], nb=nb)
            yt_c = _hyena_ctx(ut_c, cw.reshape(3, 3, HYENA_W, 1), cb.reshape(3, HYENA_W, 1),
                              hb.reshape(HYENA_ORDER, HYENA_W, 1), hspec_c, cfwd_c, cinv_c, nb=nb)
            h_ctx = _merge(h_ctx, ug_c, gb, yt_c, att_c, hp, mp, wo, mods, i, tpb=tpb_ctx, base=nb)
            h_ctx = _ffn(h_ctx, n2, mods, w_ffi, w_ffo, final_norm_g, i, tpb=tpb_ctx, base=nb, final=False)

    return h_lat.reshape(nb, seq, d)
```

```python
import functools
import math

import numpy as np
import jax
import jax.numpy as jnp
from jax import lax
from jax.experimental import pallas as pl
from jax.experimental.pallas import tpu as pltpu

F32 = jnp.float32
BF16 = jnp.bfloat16
HIGHEST = lax.Precision.HIGHEST

D_MODEL = 1024
GRID_W = 64
NORM_EPS = 1e-6

HYENA_W = 512
HYENA_ORDER = 2
N_BANDS = 16
FILTER_HIDDEN = 64
HYENA_TARGET = 1e-2
HYENA_MAX_DECAY = math.log(HYENA_TARGET) / 0.3
HYENA_MIN_DECAY = math.log(HYENA_TARGET) / 1.5

MLA_HEADS = 16
QK_NOPE = 64
QK_ROPE = 32
V_HEAD = 64
Q_LORA = 384
KV_LORA = 256
ROPE_BASE = 10000.0
AXIS_ROPE = QK_ROPE // 2
ATTN_SCALE = (QK_NOPE + QK_ROPE) ** -0.5
Q_SCALE = ATTN_SCALE * math.log2(math.e)
D_FF = 2816

HY_COLS = (HYENA_ORDER + 1) * HYENA_W
OFF_Q = HY_COLS
OFF_KV = OFF_Q + Q_LORA
OFF_KR = OFF_KV + KV_LORA
OFF_G = OFF_KR + QK_ROPE

HEAD_PAD = 128
ROPE_LANE = QK_NOPE
QKV_COLS = Q_LORA + KV_LORA + 2 * HEAD_PAD
HEADS_PER_STEP = 4
KEY_SPLIT = 1024
NMOD = 16

FFT_N1 = 64
FFT_N2 = 128
VMEM_LIMIT = 56 * 1024 * 1024


def _cparams(*sem):
    return pltpu.CompilerParams(dimension_semantics=sem, vmem_limit_bytes=VMEM_LIMIT)


def _rms(x, g):
    return x * lax.rsqrt(jnp.mean(x * x, axis=-1, keepdims=True) + NORM_EPS) * g


def _silu(x):
    return x * (1.0 / (1.0 + jnp.exp(-x)))


def _sigmoid(x):
    return 1.0 / (1.0 + jnp.exp(-x))


def _mod_row(mods_ref, row, k):
    return mods_ref[pl.ds(row, 1), k * D_MODEL:(k + 1) * D_MODEL]


def _ada_kernel(c_ref, w_ref, b_ref, o_ref):
    s = _silu(c_ref[...])
    o_ref[...] = jnp.dot(s, w_ref[...], preferred_element_type=F32, precision=HIGHEST) + b_ref[...]


def _ada(cc, ada_w, ada_b):
    depth = ada_w.shape[0]
    tn = D_MODEL
    return pl.pallas_call(
        _ada_kernel,
        grid=(depth, 6 * D_MODEL // tn),
        in_specs=[pl.BlockSpec((NMOD, D_MODEL), lambda l, j: (0, 0)),
                  pl.BlockSpec((None, D_MODEL, tn), lambda l, j: (l, 0, j)),
                  pl.BlockSpec((None, 1, tn), lambda l, j: (l, 0, j))],
        out_specs=pl.BlockSpec((None, NMOD, tn), lambda l, j: (l, 0, j)),
        out_shape=jax.ShapeDtypeStruct((depth, NMOD, 6 * D_MODEL), F32),
        compiler_params=_cparams("arbitrary", "arbitrary"),
        name="ada",
    )(cc, ada_w, ada_b.reshape(depth, 1, 6 * D_MODEL))


def _in_proj_kernel(h_ref, g_ref, mods_ref, wqkv_ref, wg_ref, wht_ref, uq_ref, ug_ref, ut_ref, *, tpb, base):
    row = base + pl.program_id(0) // tpb
    y = _rms(h_ref[...], g_ref[...])
    a = (y * (1.0 + _mod_row(mods_ref, row, 1)) + _mod_row(mods_ref, row, 0)).astype(BF16)
    uq_ref[...] = jnp.dot(a, wqkv_ref[...], preferred_element_type=F32).astype(BF16)
    ug_ref[...] = jnp.dot(a, wg_ref[...], preferred_element_type=F32).astype(BF16)
    ut_ref[...] = lax.dot_general(wht_ref[...], a, (((1,), (1,)), ((), ())),
                                  preferred_element_type=F32).astype(BF16)


def _in_proj(h, g, mods, w_qkv, w_gate, w_hy_t, layer, *, tpb, base, tm=512):
    t = h.shape[0]
    const = lambda i: (0, 0)
    wmap = lambda i: (layer, 0, 0)
    return pl.pallas_call(
        functools.partial(_in_proj_kernel, tpb=tpb, base=base),
        grid=(t // tm,),
        in_specs=[pl.BlockSpec((tm, D_MODEL), lambda i: (i, 0)),
                  pl.BlockSpec((1, D_MODEL), const),
                  pl.BlockSpec((NMOD, 6 * D_MODEL), const),
                  pl.BlockSpec((None, D_MODEL, QKV_COLS), wmap),
                  pl.BlockSpec((None, D_MODEL, 2 * D_MODEL), wmap),
                  pl.BlockSpec((None, HY_COLS, D_MODEL), wmap)],
        out_specs=[pl.BlockSpec((tm, QKV_COLS), lambda i: (i, 0)),
                   pl.BlockSpec((tm, 2 * D_MODEL), lambda i: (i, 0)),
                   pl.BlockSpec((HY_COLS, tm), lambda i: (0, i))],
        out_shape=[jax.ShapeDtypeStruct((t, QKV_COLS), BF16),
                   jax.ShapeDtypeStruct((t, 2 * D_MODEL), BF16),
                   jax.ShapeDtypeStruct((HY_COLS, t), BF16)],
        compiler_params=_cparams("parallel"),
        name="in_proj",
    )(h, g.reshape(1, D_MODEL), mods, w_qkv, w_gate, w_hy_t)


def _mm_hi_kernel(a_ref, w_ref, o_ref):
    o_ref[...] = jnp.dot(a_ref[...], w_ref[...], preferred_element_type=F32, precision=HIGHEST)


def _matmul_hi(a, w, tm):
    m, k = a.shape
    n = w.shape[1]
    return pl.pallas_call(
        _mm_hi_kernel,
        grid=(m // tm,),
        in_specs=[pl.BlockSpec((tm, k), lambda i: (i, 0)),
                  pl.BlockSpec((k, n), lambda i: (0, 0))],
        out_specs=pl.BlockSpec((tm, n), lambda i: (i, 0)),
        out_shape=jax.ShapeDtypeStruct((m, n), F32),
        compiler_params=_cparams("parallel"),
        name="matmul_hi",
    )(a, w)


def _qkv_kernel(u_ref, qg_ref, kvg_ref, wq_ref, wk_ref, wv_ref, ct_ref, st_ref, q_ref, k_ref, v_ref):
    u = u_ref[...].astype(F32)
    nq = _rms(u[:, :Q_LORA], qg_ref[...]).astype(BF16)
    nkv = _rms(u[:, Q_LORA:Q_LORA + KV_LORA], kvg_ref[...]).astype(BF16)
    ka = u[:, Q_LORA + KV_LORA:Q_LORA + KV_LORA + HEAD_PAD]
    kb = u[:, Q_LORA + KV_LORA + HEAD_PAD:]
    ct = ct_ref[...]
    st = st_ref[...]
    kr = ka * ct + kb * st
    qq = jnp.dot(nq, wq_ref[...], preferred_element_type=F32)
    kn = jnp.dot(nkv, wk_ref[...], preferred_element_type=F32)
    nh = MLA_HEADS * HEAD_PAD
    for h in range(MLA_HEADS):
        lo, hi = h * HEAD_PAD, (h + 1) * HEAD_PAD
        qh = qq[:, lo:hi] * ct + qq[:, nh + lo:nh + hi] * st
        q_ref[:, lo:hi] = (qh * Q_SCALE).astype(BF16)
        k_ref[:, lo:hi] = (kn[:, lo:hi] + kr).astype(BF16)
    v_ref[...] = lax.dot_general(wv_ref[...], nkv, (((1,), (1,)), ((), ())),
                                 preferred_element_type=F32).astype(BF16)


def _qkv_up(u, qg, kvg, wq, wk, wvt, ctab, stab, layer, *, rope, tm=512):
    t = u.shape[0]
    ntab = ctab.shape[0] // tm
    tab_map = (lambda i: (i % ntab, 0)) if rope else (lambda i: (0, 0))
    nh = MLA_HEADS * HEAD_PAD
    nv = MLA_HEADS * V_HEAD
    wmap = lambda i: (layer, 0, 0)
    return pl.pallas_call(
        _qkv_kernel,
        grid=(t // tm,),
        in_specs=[pl.BlockSpec((tm, QKV_COLS), lambda i: (i, 0)),
                  pl.BlockSpec((None, 1, Q_LORA), wmap),
                  pl.BlockSpec((None, 1, KV_LORA), wmap),
                  pl.BlockSpec((None, Q_LORA, 2 * nh), wmap),
                  pl.BlockSpec((None, KV_LORA, nh), wmap),
                  pl.BlockSpec((None, nv, KV_LORA), wmap),
                  pl.BlockSpec((tm, HEAD_PAD), tab_map),
                  pl.BlockSpec((tm, HEAD_PAD), tab_map)],
        out_specs=[pl.BlockSpec((tm, nh), lambda i: (i, 0)),
                   pl.BlockSpec((tm, nh), lambda i: (i, 0)),
                   pl.BlockSpec((nv, tm), lambda i: (0, i))],
        out_shape=[jax.ShapeDtypeStruct((t, nh), BF16),
                   jax.ShapeDtypeStruct((t, nh), BF16),
                   jax.ShapeDtypeStruct((nv, t), BF16)],
        compiler_params=_cparams("parallel"),
        name="qkv_up",
    )(u, qg, kvg, wq, wk, wvt, ctab, stab)


def _attn_kernel(*refs, lks, n_sub, tq):
    n_kv = len(lks)
    q_ref = refs[0]
    k_refs = refs[1:1 + n_kv]
    v_refs = refs[1 + n_kv:1 + 2 * n_kv]
    o_ref = refs[1 + 2 * n_kv]
    nt = (((1,), (1,)), ((), ()))
    units = [(j, h) for j in range(n_sub) for h in range(HEADS_PER_STEP)]

    def scores(j, h):
        q = q_ref[j * tq:(j + 1) * tq, h * HEAD_PAD:(h + 1) * HEAD_PAD]
        out = []
        for k_ref, lk in zip(k_refs, lks):
            step = min(lk, KEY_SPLIT)
            for c in range(0, lk, step):
                out.append(lax.dot_general(k_ref[c:c + step, h * HEAD_PAD:(h + 1) * HEAD_PAD], q, nt,
                                           preferred_element_type=F32))
        return out

    s_next = scores(*units[0])
    for idx, (j, h) in enumerate(units):
        s = s_next
        if idx + 1 < len(units):
            s_next = scores(*units[idx + 1])
        m = s[0].max(axis=0, keepdims=True)
        for si in s[1:]:
            m = jnp.maximum(m, si.max(axis=0, keepdims=True))
        p = [jnp.exp2(si - m) for si in s]
        l = p[0].sum(axis=0, keepdims=True)
        for pi in p[1:]:
            l = l + pi.sum(axis=0, keepdims=True)
        o, i = None, 0
        for v_ref, lk in zip(v_refs, lks):
            step = min(lk, KEY_SPLIT)
            for c in range(0, lk, step):
                d = jnp.dot(v_ref[h * V_HEAD:(h + 1) * V_HEAD, c:c + step], p[i].astype(BF16),
                            preferred_element_type=F32)
                o = d if o is None else o + d
                i += 1
        o_ref[h * V_HEAD:(h + 1) * V_HEAD, j * tq:(j + 1) * tq] = (o / l).astype(BF16)


def _attention(q, kv_sets, *, nb, tq=256, n_sub=4):
    t = q.shape[0]
    lq = t // nb
    n_sub = min(n_sub, lq // tq)
    tb = tq * n_sub
    nq = lq // tb
    groups = MLA_HEADS // HEADS_PER_STEP
    qw = HEADS_PER_STEP * HEAD_PAD
    vw = HEADS_PER_STEP * V_HEAD
    in_specs = [pl.BlockSpec((tb, qw), lambda b, g, i: (b * nq + i, g))]
    k_specs, v_specs, ks, vs = [], [], [], []
    for k, vt, lk in kv_sets:
        k_specs.append(pl.BlockSpec((lk, qw), lambda b, g, i: (b, g)))
        v_specs.append(pl.BlockSpec((vw, lk), lambda b, g, i: (g, b)))
        ks.append(k)
        vs.append(vt)
    lks = tuple(lk for _, _, lk in kv_sets)
    return pl.pallas_call(
        functools.partial(_attn_kernel, lks=lks, n_sub=n_sub, tq=tq),
        grid=(nb, groups, nq),
        in_specs=in_specs + k_specs + v_specs,
        out_specs=pl.BlockSpec((vw, tb), lambda b, g, i: (g, b * nq + i)),
        out_shape=jax.ShapeDtypeStruct((MLA_HEADS * V_HEAD, t), BF16),
        compiler_params=_cparams("parallel", "parallel", "arbitrary"),
        name="attention",
    )(q, *ks, *vs)


def _filt_mlp_kernel(z_ref, w1_ref, b1_ref, w2_ref, b2_ref, f0_ref, f1_ref, o_ref):
    h = jnp.dot(w1_ref[...], z_ref[...], preferred_element_type=F32, precision=HIGHEST) + b1_ref[...]
    h = jnp.sin(f0_ref[...] * h)
    h = jnp.dot(w2_ref[...], h, preferred_element_type=F32, precision=HIGHEST) + b2_ref[...]
    o_ref[...] = jnp.sin(f1_ref[...] * h)


def _filt_mlp(zt, w1, b1, w2, b2, freq):
    kz, n = zt.shape
    w1t = jnp.zeros((FILTER_HIDDEN, kz), F32).at[:, :w1.shape[0]].set(w1.T)
    col = lambda a: a.reshape(FILTER_HIDDEN, 1)
    args = (zt, w1t, col(b1), w2.T, col(b2), col(freq[0]), col(freq[1]))
    return pl.pallas_call(
        _filt_mlp_kernel,
        grid=(1,),
        in_specs=[pl.BlockSpec(a.shape, lambda i: (0, 0)) for a in args],
        out_specs=pl.BlockSpec((FILTER_HIDDEN, n), lambda i: (0, 0)),
        out_shape=jax.ShapeDtypeStruct((FILTER_HIDDEN, n), F32),
        compiler_params=_cparams("arbitrary"),
        name="filt_mlp",
    )(*args)


def _filt_taps_kernel(h_ref, w3_ref, d_ref, t_ref, o_ref, *, seq):
    n = h_ref.shape[1]
    pos = lax.broadcasted_iota(jnp.int32, (1, n), 1)
    win = jnp.exp(-t_ref[...] * d_ref[...])
    h = h_ref[...]
    for o in range(HYENA_ORDER):
        fwd = jnp.dot(w3_ref[0, o], h, preferred_element_type=F32, precision=HIGHEST)
        bwd = jnp.dot(w3_ref[1, o], h, preferred_element_type=F32, precision=HIGHEST)
        full = jnp.where(pos < seq, fwd, jnp.where(pos > seq, bwd, 0.0)) * win
        o_ref[o] = full / jnp.sum(jnp.abs(full), axis=1, keepdims=True)


def _filt_taps(h2t, w3, dabs, tvec, seq, cf=64):
    n = h2t.shape[1]
    w3t = w3.T.reshape(2, HYENA_ORDER, HYENA_W, FILTER_HIDDEN)
    return pl.pallas_call(
        functools.partial(_filt_taps_kernel, seq=seq),
        grid=(HYENA_W // cf,),
        in_specs=[pl.BlockSpec((FILTER_HIDDEN, n), lambda c: (0, 0)),
                  pl.BlockSpec((2, HYENA_ORDER, cf, FILTER_HIDDEN), lambda c: (0, 0, c, 0)),
                  pl.BlockSpec((cf, 1), lambda c: (c, 0)),
                  pl.BlockSpec((1, n), lambda c: (0, 0))],
        out_specs=pl.BlockSpec((HYENA_ORDER, cf, n), lambda c: (0, c, 0)),
        out_shape=jax.ShapeDtypeStruct((HYENA_ORDER, HYENA_W, n), F32),
        compiler_params=_cparams("parallel"),
        name="filt_taps",
    )(h2t, w3t, dabs, tvec)


def _filt_spec_kernel(x_ref, f1_ref, tc_ref, ts_ref, f2_ref, o_ref):
    ns = x_ref.shape[0]
    x = jnp.concatenate([x_ref[s] for s in range(ns)], axis=1).astype(BF16)
    a = jnp.dot(f1_ref[...], x, preferred_element_type=F32)
    tc = tc_ref[...]
    ts = ts_ref[...]
    rows = []
    for s in range(ns):
        ar = a[:FFT_N1, s * FFT_N2:(s + 1) * FFT_N2]
        ai = a[FFT_N1:, s * FFT_N2:(s + 1) * FFT_N2]
        rows.append(jnp.concatenate([ar * tc + ai * ts, ai * tc - ar * ts], axis=1))
    b = jnp.concatenate(rows, axis=0).astype(BF16)
    z = jnp.dot(b, f2_ref[...], preferred_element_type=F32)
    o_ref[...] = z.reshape(ns, FFT_N1, 2 * FFT_N2)


def _filt_spec(hfull, consts, ns=16):
    r = hfull.shape[0]
    x = hfull.reshape(r, FFT_N1, FFT_N2)
    f1h, tc, ts, f2 = consts["f1h"], consts["tc"], consts["ts"], consts["f2"]
    const = lambda i: (0, 0)
    return pl.pallas_call(
        _filt_spec_kernel,
        grid=(r // ns,),
        in_specs=[pl.BlockSpec((ns, FFT_N1, FFT_N2), lambda i: (i, 0, 0)),
                  pl.BlockSpec(f1h.shape, const), pl.BlockSpec(tc.shape, const),
                  pl.BlockSpec(ts.shape, const), pl.BlockSpec(f2.shape, const)],
        out_specs=pl.BlockSpec((ns, FFT_N1, 2 * FFT_N2), lambda i: (i, 0, 0)),
        out_shape=jax.ShapeDtypeStruct((r, FFT_N1, 2 * FFT_N2), F32),
        compiler_params=_cparams("parallel"),
        name="filt_spec",
    )(x, f1h, tc, ts, f2)


def _short_conv_tile(x, w, b):
    nrow = x.shape[1]
    lane = lax.broadcasted_iota(jnp.int32, x.shape, 2)
    row = lax.broadcasted_iota(jnp.int32, x.shape, 1)
    r1 = pltpu.roll(x, 1, 2)
    r2 = pltpu.roll(r1, 1, 1)
    prev = jnp.where(lane == 0, jnp.where(row == 0, 0.0, r2), r1)
    l1 = pltpu.roll(x, FFT_N2 - 1, 2)
    l2 = pltpu.roll(l1, nrow - 1, 1)
    nxt = jnp.where(lane == FFT_N2 - 1, jnp.where(row == nrow - 1, 0.0, l2), l1)
    return prev * w[0] + x * w[1] + nxt * w[2] + b


def _long_conv_pair(vr, vi, h, f1, f1i, f2, f2i, tc, ts):
    ns = vr.shape[0]
    half = FFT_N2
    x = jnp.concatenate([jnp.concatenate([vr[s] for s in range(ns)], axis=1),
                         jnp.concatenate([vi[s] for s in range(ns)], axis=1)], axis=0).astype(BF16)
    a = jnp.dot(f1, x, preferred_element_type=F32)
    rows = []
    for s in range(ns):
        ar = a[:FFT_N1, s * half:(s + 1) * half]
        ai = a[FFT_N1:, s * half:(s + 1) * half]
        rows.append(jnp.concatenate([ar * tc + ai * ts, ai * tc - ar * ts], axis=1))
    b = jnp.concatenate(rows, axis=0).astype(BF16)
    z = jnp.dot(b, f2, preferred_element_type=F32).reshape(ns, FFT_N1, 2 * half)
    zr, zi = z[:, :, :half], z[:, :, half:]
    hr, hi = h[:, :, :half], h[:, :, half:]
    w = jnp.concatenate([zr * hr - zi * hi, zr * hi + zi * hr], axis=2)
    w = w.reshape(ns * FFT_N1, 2 * half).astype(BF16)
    c = jnp.dot(w, f2i, preferred_element_type=F32).reshape(ns, FFT_N1, 2 * half)
    cr, ci = c[:, :, :half], c[:, :, half:]
    dr = cr * tc - ci * ts
    di = cr * ts + ci * tc
    d = jnp.concatenate([jnp.concatenate([dr[s] for s in range(ns)], axis=1),
                         jnp.concatenate([di[s] for s in range(ns)], axis=1)], axis=0).astype(BF16)
    y = jnp.dot(f1i, d, preferred_element_type=F32)
    nrow = y.shape[0] // 2
    yr = jnp.stack([y[:nrow, s * half:(s + 1) * half] for s in range(ns)], axis=0)
    yi = jnp.stack([y[nrow:, s * half:(s + 1) * half] for s in range(ns)], axis=0)
    return yr, yi


def _hyena_kernel(x0_ref, x1_ref, v_ref, cw_ref, cb_ref, hb_ref, h_ref,
                  f1_ref, f1i_ref, f2_ref, f2i_ref, tc_ref, ts_ref, o_ref):
    consts = (f1_ref[...], f1i_ref[...], f2_ref[...], f2i_ref[...], tc_ref[...], ts_ref[...])
    g0, g1, v = [], [], []
    for half in range(2):
        g0.append(_short_conv_tile(x0_ref[:, half].astype(F32), cw_ref[0], cb_ref[0]))
        g1.append(_short_conv_tile(x1_ref[:, half].astype(F32), cw_ref[1], cb_ref[1]))
        v.append(_short_conv_tile(v_ref[:, half].astype(F32), cw_ref[2], cb_ref[2]))
    gates = (g0, g1)
    for o in range(HYENA_ORDER):
        yr, yi = _long_conv_pair(v[0], v[1], h_ref[o], *consts)
        v = [gates[o][0] * (yr + hb_ref[o] * v[0]), gates[o][1] * (yi + hb_ref[o] * v[1])]
    o_ref[:, 0] = v[0].astype(BF16)
    o_ref[:, 1] = v[1].astype(BF16)


def _hyena_latent(ut, cw, cb, hb, hspec, consts, *, nb, ct=32):
    seq = ut.shape[1] // nb
    nrow = seq // FFT_N2
    npair = nb // 2
    nct = HYENA_W // ct
    u5 = ut.reshape(3 * HYENA_W, 2, npair, nrow, FFT_N2)
    part = lambda p: pl.BlockSpec((ct, 2, None, nrow, FFT_N2), lambda c, j: (p * nct + c, 0, j, 0, 0))
    const2 = lambda c, j: (0, 0)
    cnames = ("f1", "f1i", "f2", "f2i", "tc", "ts")
    out = pl.pallas_call(
        _hyena_kernel,
        grid=(nct, npair),
        in_specs=[part(0), part(1), part(2),
                  pl.BlockSpec((3, 3, ct, 1, 1), lambda c, j: (0, 0, c, 0, 0)),
                  pl.BlockSpec((3, ct, 1, 1), lambda c, j: (0, c, 0, 0)),
                  pl.BlockSpec((HYENA_ORDER, ct, 1, 1), lambda c, j: (0, c, 0, 0)),
                  pl.BlockSpec((HYENA_ORDER, ct, FFT_N1, 2 * FFT_N2), lambda c, j: (0, c, 0, 0))]
                 + [pl.BlockSpec(consts[k].shape, const2) for k in cnames],
        out_specs=pl.BlockSpec((ct, 2, None, nrow, FFT_N2), lambda c, j: (c, 0, j, 0, 0)),
        out_shape=jax.ShapeDtypeStruct((HYENA_W, 2, npair, nrow, FFT_N2), BF16),
        compiler_params=_cparams("parallel", "arbitrary"),
        name="hyena",
    )(u5, u5, u5, cw, cb, hb, hspec, *[consts[k] for k in cnames])
    return out.reshape(HYENA_W, nb * seq)


def _hyena_ctx_kernel(x0_ref, x1_ref, v_ref, cw_ref, cb_ref, hb_ref, h_ref, cf_ref, ci_ref, o_ref, *, nb):
    seq = x0_ref.shape[1] // nb
    cc = x0_ref.shape[0]
    half = h_ref.shape[2] // 2

    def rows(ref, p):
        x = jnp.concatenate([ref[:, b * seq:(b + 1) * seq] for b in range(nb)], axis=0).astype(F32)
        lane = lax.broadcasted_iota(jnp.int32, x.shape, 1)
        prev = jnp.where(lane == 0, 0.0, pltpu.roll(x, 1, 1))
        nxt = jnp.where(lane == seq - 1, 0.0, pltpu.roll(x, seq - 1, 1))
        w = [jnp.concatenate([cw_ref[p, j]] * nb, axis=0) for j in range(3)]
        return prev * w[0] + x * w[1] + nxt * w[2] + jnp.concatenate([cb_ref[p]] * nb, axis=0)

    g = (rows(x0_ref, 0), rows(x1_ref, 1))
    v = rows(v_ref, 2)
    for o in range(HYENA_ORDER):
        spec = jnp.dot(v.astype(BF16), cf_ref[...], preferred_element_type=F32)
        xr, xi = spec[:, :half], spec[:, half:]
        hh = jnp.concatenate([h_ref[o]] * nb, axis=0)
        hr, hi = hh[:, :half], hh[:, half:]
        y = jnp.concatenate([xr * hr - xi * hi, xr * hi + xi * hr], axis=1).astype(BF16)
        conv = jnp.dot(y, ci_ref[...], preferred_element_type=F32)
        v = g[o] * (conv + jnp.concatenate([hb_ref[o]] * nb, axis=0) * v)
    for b in range(nb):
        o_ref[:, b * seq:(b + 1) * seq] = v[b * cc:(b + 1) * cc].astype(BF16)


def _hyena_ctx(ut, cw, cb, hb, hspec, cfwd, cinv, *, nb, cc=64):
    n = ut.shape[1]
    ncc = HYENA_W // cc
    part = lambda p: pl.BlockSpec((cc, n), lambda c: (p * ncc + c, 0))
    return pl.pallas_call(
        functools.partial(_hyena_ctx_kernel, nb=nb),
        grid=(ncc,),
        in_specs=[part(0), part(1), part(2),
                  pl.BlockSpec((3, 3, cc, 1), lambda c: (0, 0, c, 0)),
                  pl.BlockSpec((3, cc, 1), lambda c: (0, c, 0)),
                  pl.BlockSpec((HYENA_ORDER, cc, 1), lambda c: (0, c, 0)),
                  pl.BlockSpec((HYENA_ORDER, cc, hspec.shape[2]), lambda c: (0, c, 0)),
                  pl.BlockSpec(cfwd.shape, lambda c: (0, 0)),
                  pl.BlockSpec(cinv.shape, lambda c: (0, 0))],
        out_specs=pl.BlockSpec((cc, n), lambda c: (c, 0)),
        out_shape=jax.ShapeDtypeStruct((HYENA_W, n), BF16),
        compiler_params=_cparams("parallel"),
        name="hyena_ctx",
    )(ut, ut, ut, cw, cb, hb, hspec, cfwd, cinv)


def _merge_kernel(h_ref, ug_ref, gb_ref, yt_ref, att_ref, hp_ref, mp_ref, wo_ref, mods_ref, o_ref, *, tpb, base):
    row = base + pl.program_id(0) // tpb
    g = _sigmoid(ug_ref[...].astype(F32) + gb_ref[...])
    yh = lax.dot_general(yt_ref[...], hp_ref[...], (((0,), (0,)), ((), ())), preferred_element_type=F32)
    ya = lax.dot_general(att_ref[...], mp_ref[...], (((0,), (0,)), ((), ())), preferred_element_type=F32)
    mix = (g[:, :D_MODEL] * yh + g[:, D_MODEL:] * ya).astype(BF16)
    out = jnp.dot(mix, wo_ref[...], preferred_element_type=F32)
    o_ref[...] = h_ref[...] + _mod_row(mods_ref, row, 2) * out


def _merge(h, ug, gate_b, yt, att, hy_proj, mla_proj, w_out, mods, layer, *, tpb, base, tm=512):
    t = h.shape[0]
    const = lambda i: (0, 0)
    wmap = lambda i: (layer, 0, 0)
    return pl.pallas_call(
        functools.partial(_merge_kernel, tpb=tpb, base=base),
        grid=(t // tm,),
        in_specs=[pl.BlockSpec((tm, D_MODEL), lambda i: (i, 0)),
                  pl.BlockSpec((tm, 2 * D_MODEL), lambda i: (i, 0)),
                  pl.BlockSpec((None, 1, 2 * D_MODEL), wmap),
                  pl.BlockSpec((HYENA_W, tm), lambda i: (0, i)),
                  pl.BlockSpec((MLA_HEADS * V_HEAD, tm), lambda i: (0, i)),
                  pl.BlockSpec((None,) + hy_proj.shape[1:], wmap),
                  pl.BlockSpec((None,) + mla_proj.shape[1:], wmap),
                  pl.BlockSpec((None,) + w_out.shape[1:], wmap),
                  pl.BlockSpec((NMOD, 6 * D_MODEL), const)],
        out_specs=pl.BlockSpec((tm, D_MODEL), lambda i: (i, 0)),
        out_shape=jax.ShapeDtypeStruct((t, D_MODEL), F32),
        compiler_params=_cparams("parallel"),
        name="merge",
    )(h, ug, gate_b, yt, att, hy_proj, mla_proj, w_out, mods)


FF_CHUNK = 256


def _ffn_kernel(h_ref, g_ref, mods_ref, wi_ref, wo_ref, fg_ref, o_ref, *, tpb, base, final):
    row = base + pl.program_id(0) // tpb
    h = h_ref[...]
    f = (_rms(h, g_ref[...]) * (1.0 + _mod_row(mods_ref, row, 4)) + _mod_row(mods_ref, row, 3)).astype(BF16)
    y = None
    for c in range(0, D_FF, FF_CHUNK):
        gate = jnp.dot(f, wi_ref[:, c:c + FF_CHUNK], preferred_element_type=F32)
        up = jnp.dot(f, wi_ref[:, D_FF + c:D_FF + c + FF_CHUNK], preferred_element_type=F32)
        act = (_silu(gate) * up).astype(BF16)
        d = jnp.dot(act, wo_ref[c:c + FF_CHUNK, :], preferred_element_type=F32)
        y = d if y is None else y + d
    out = h + _mod_row(mods_ref, row, 5) * y
    o_ref[...] = _rms(out, fg_ref[...]) if final else out


def _ffn(h, g, mods, w_in, w_down, final_g, layer, *, tpb, base, final, tm=512):
    t = h.shape[0]
    const = lambda i: (0, 0)
    wmap = lambda i: (layer, 0, 0)
    return pl.pallas_call(
        functools.partial(_ffn_kernel, tpb=tpb, base=base, final=final),
        grid=(t // tm,),
        in_specs=[pl.BlockSpec((tm, D_MODEL), lambda i: (i, 0)),
                  pl.BlockSpec((None, 1, D_MODEL), wmap),
                  pl.BlockSpec((NMOD, 6 * D_MODEL), const),
                  pl.BlockSpec((None, D_MODEL, 2 * D_FF), wmap),
                  pl.BlockSpec((None, D_FF, D_MODEL), wmap),
                  pl.BlockSpec((1, D_MODEL), const)],
        out_specs=pl.BlockSpec((tm, D_MODEL), lambda i: (i, 0)),
        out_shape=jax.ShapeDtypeStruct((t, D_MODEL), F32),
        compiler_params=_cparams("parallel"),
        name="ffn",
    )(h, g, mods, w_in, w_down, final_g.reshape(1, -1))


def _dft_consts():
    n = FFT_N1 * FFT_N2
    k1 = np.arange(FFT_N1)
    th = 2 * np.pi * np.outer(k1, np.arange(FFT_N1)) / FFT_N1
    c, s = np.cos(th), np.sin(th)
    hn = FFT_N1 // 2
    f1 = np.block([[c[:, :hn], s[:, :hn]], [-s[:, :hn], c[:, :hn]]])
    f1h = np.concatenate([c, -s], axis=0)
    f1i = np.block([[c[:, :hn].T, -s[:, :hn].T], [s[:, :hn].T, c[:, :hn].T]]) / n
    ph = 2 * np.pi * np.outer(k1, np.arange(FFT_N2)) / n
    th2 = 2 * np.pi * np.outer(np.arange(FFT_N2), np.arange(FFT_N2)) / FFT_N2
    c2, s2 = np.cos(th2), np.sin(th2)
    f2 = np.block([[c2, -s2], [s2, c2]])
    f2i = np.block([[c2, s2], [-s2, c2]])
    return {"f1": jnp.asarray(f1, BF16), "f1h": jnp.asarray(f1h, BF16), "f1i": jnp.asarray(f1i, BF16),
            "f2": jnp.asarray(f2, BF16), "f2i": jnp.asarray(f2i, BF16),
            "tc": jnp.asarray(np.cos(ph), F32), "ts": jnp.asarray(np.sin(ph), F32)}


def _dense_dft(seq):
    n = 2 * seq
    th = 2 * np.pi * np.outer(np.arange(n), np.arange(n)) / n
    c, s = np.cos(th), np.sin(th)
    full = np.concatenate([c, -s], axis=1)
    inv = np.concatenate([c[:, :seq], -s[:, :seq]], axis=0) / n
    return jnp.asarray(full, F32), jnp.asarray(full[:seq], BF16), jnp.asarray(inv, BF16)


def _pos_features(seq):
    n = np.arange(2 * seq)
    lag = np.minimum(np.where(n < seq, n, 2 * seq - n), seq - 1)
    t = np.linspace(0.0, 1.0, seq, dtype=np.float32)[lag]
    bands = np.arange(1, N_BANDS + 1, dtype=np.float32)
    ang = (np.float32(2 * math.pi) * bands)[:, None] * t[None, :]
    zt = np.concatenate([t[None, :], np.cos(ang), np.sin(ang), np.zeros((7, 2 * seq), np.float32)], axis=0)
    return jnp.asarray(zt, F32), jnp.asarray(t.reshape(1, -1), F32)


def _rope_tables(seq):
    rows = seq // GRID_W
    row = np.repeat(np.arange(rows, dtype=np.float32), GRID_W)
    col = np.tile(np.arange(GRID_W, dtype=np.float32), rows)
    inv = (np.float32(ROPE_BASE) ** (-np.arange(0, AXIS_ROPE, 2, dtype=np.float32) / AXIS_ROPE)).astype(np.float32)
    ang = np.concatenate([row[:, None] * inv, col[:, None] * inv], axis=-1)
    cos, sin = np.cos(ang), np.sin(ang)
    ones = np.ones((seq, QK_NOPE), np.float32)
    pad = HEAD_PAD - QK_NOPE - QK_ROPE
    ctab = np.concatenate([ones, cos, cos, np.ones((seq, pad), np.float32)], axis=1)
    stab = np.concatenate([0 * ones, sin, sin, np.zeros((seq, pad), np.float32)], axis=1)
    return jnp.asarray(ctab, F32), jnp.asarray(stab, F32)


def _rot_cols(w):
    half = QK_ROPE // 2
    return jnp.concatenate([-w[..., half:], w[..., :half]], axis=-1)


def _pad_heads(nope, rope):
    lead = nope.shape[:-2]
    z = jnp.zeros(lead + (MLA_HEADS, HEAD_PAD - QK_NOPE - QK_ROPE), nope.dtype)
    return jnp.concatenate([nope, rope, z], axis=-1).reshape(lead + (MLA_HEADS * HEAD_PAD,))


def _prep_weights(w_in, w_uq, w_ukv):
    depth, d = w_in.shape[0], w_in.shape[1]
    w_hy_t = jnp.swapaxes(w_in[:, :, :OFF_Q], 1, 2).astype(BF16)
    kr_w = w_in[:, :, OFF_KR:OFF_G]
    zl = jnp.zeros((depth, d, ROPE_LANE), F32)
    zr = jnp.zeros((depth, d, HEAD_PAD - ROPE_LANE - QK_ROPE), F32)
    w_qkv = jnp.concatenate([w_in[:, :, OFF_Q:OFF_KR], zl, kr_w, zr, zl, _rot_cols(kr_w), zr], axis=2).astype(BF16)
    w_gate = w_in[:, :, OFF_G:].astype(BF16)
    uq = w_uq.reshape(depth, Q_LORA, MLA_HEADS, QK_NOPE + QK_ROPE)
    wq = jnp.concatenate([_pad_heads(uq[..., :QK_NOPE], uq[..., QK_NOPE:]),
                          _pad_heads(0 * uq[..., :QK_NOPE], _rot_cols(uq[..., QK_NOPE:]))], axis=2).astype(BF16)
    ukv = w_ukv.reshape(depth, KV_LORA, MLA_HEADS, QK_NOPE + V_HEAD)
    wk = _pad_heads(ukv[..., :QK_NOPE], jnp.zeros((depth, KV_LORA, MLA_HEADS, QK_ROPE), F32)).astype(BF16)
    wvt = jnp.swapaxes(ukv[..., QK_NOPE:].reshape(depth, KV_LORA, MLA_HEADS * V_HEAD), 1, 2).astype(BF16)
    return w_hy_t, w_qkv, w_gate, wq, wk, wvt


def kernel(x, c, ctx, c_ctx, ada_w, ada_b, norm1_g, norm2_g, w_in, gate_b, hy_conv_w, hy_conv_b,
           hy_filt_w1, hy_filt_b1, hy_filt_w2, hy_filt_b2, hy_filt_w3, hy_filt_freq, hy_bias, hy_proj,
           q_norm_g, kv_norm_g, w_uq, w_ukv, mla_proj, w_out, ffn_w_in, ffn_w_out, final_norm_g):
    nb, seq, d = x.shape
    lc = ctx.shape[1]
    depth = ada_w.shape[0]
    assert d == D_MODEL and seq == FFT_N1 * FFT_N2 // 2 and nb % 2 == 0 and nb < NMOD
    tm = 512
    tpb_lat = seq // tm
    tpb_ctx = 1 << 30

    cc = jnp.zeros((NMOD, d), F32).at[:nb].set(c).at[nb].set(c_ctx)
    mods_all = _ada(cc, ada_w, ada_b)

    consts = _dft_consts()
    cfull_c, cfwd_c, cinv_c = _dense_dft(lc)
    ctab, stab = _rope_tables(seq)
    one_tab = jnp.ones((tm, HEAD_PAD), F32)
    zero_tab = jnp.zeros((tm, HEAD_PAD), F32)
    dabs = jnp.asarray(np.abs(np.linspace(HYENA_MIN_DECAY, HYENA_MAX_DECAY, HYENA_W, dtype=np.float32)).reshape(HYENA_W, 1))
    zt_lat, tv_lat = _pos_features(seq)
    zt_ctx, tv_ctx = _pos_features(lc)

    w_hy_t, w_qkv, w_gate, wq, wk, wvt = _prep_weights(w_in, w_uq, w_ukv)
    hp, mp, wo = hy_proj.astype(BF16), mla_proj.astype(BF16), w_out.astype(BF16)
    w_ffi, w_ffo = ffn_w_in.astype(BF16), ffn_w_out.astype(BF16)
    qg = q_norm_g.reshape(depth, 1, Q_LORA)
    kvg = kv_norm_g.reshape(depth, 1, KV_LORA)
    gb = gate_b.reshape(depth, 1, 2 * D_MODEL)
    n1 = norm1_g
    n2 = norm2_g.reshape(depth, 1, D_MODEL)
    cw_all = jnp.transpose(hy_conv_w.reshape(depth, 3, 3, HYENA_W), (0, 2, 1, 3))
    cb_all = hy_conv_b.reshape(depth, 3, HYENA_W)

    h_lat = x.reshape(nb * seq, d)
    h_ctx = ctx.reshape(nb * lc, d)

    for i in range(depth):
        last = i == depth - 1
        mods = mods_all[i]
        cw, cb, hb = cw_all[i], cb_all[i], hy_bias[i]

        filt = (hy_filt_w1[i], hy_filt_b1[i], hy_filt_w2[i], hy_filt_b2[i], hy_filt_freq[i])
        hfull = _filt_taps(_filt_mlp(zt_lat, *filt), hy_filt_w3[i], dabs, tv_lat, seq)
        hspec = _filt_spec(hfull.reshape(HYENA_ORDER * HYENA_W, 2 * seq), consts)
        hspec = hspec.reshape(HYENA_ORDER, HYENA_W, FFT_N1, 2 * FFT_N2)

        uq_l, ug_l, ut_l = _in_proj(h_lat, n1[i], mods, w_qkv, w_gate, w_hy_t, i, tpb=tpb_lat, base=0)
        uq_c, ug_c, ut_c = _in_proj(h_ctx, n1[i], mods, w_qkv, w_gate, w_hy_t, i, tpb=tpb_ctx, base=nb)
        q_l, k_l, v_l = _qkv_up(uq_l, qg, kvg, wq, wk, wvt, ctab, stab, i, rope=True)
        q_c, k_c, v_c = _qkv_up(uq_c, qg, kvg, wq, wk, wvt, one_tab, zero_tab, i, rope=False)
        att_l = _attention(q_l, [(k_l, v_l, seq), (k_c, v_c, lc)], nb=nb)
        yt_l = _hyena_latent(ut_l, cw.reshape(3, 3, HYENA_W, 1, 1), cb.reshape(3, HYENA_W, 1, 1),
                             hb.reshape(HYENA_ORDER, HYENA_W, 1, 1), hspec, consts, nb=nb)
        h_lat = _merge(h_lat, ug_l, gb, yt_l, att_l, hp, mp, wo, mods, i, tpb=tpb_lat, base=0)

        h_lat = _ffn(h_lat, n2, mods, w_ffi, w_ffo, final_norm_g, i, tpb=tpb_lat, base=0, final=last)

        if not last:
            hfull_c = _filt_taps(_filt_mlp(zt_ctx, *filt), hy_filt_w3[i], dabs, tv_ctx, lc)
            hspec_c = _matmul_hi(hfull_c.reshape(HYENA_ORDER * HYENA_W, 2 * lc), cfull_c, tm=256)
            hspec_c = hspec_c.reshape(HYENA_ORDER, HYENA_W, 4 * lc)
            att_c = _attention(q_c, [(k_c, v_c, lc)], nb=nb)
            yt_c = _hyena_ctx(ut_c, cw.reshape(3, 3, HYENA_W, 1), cb.reshape(3, HYENA_W, 1),
                              hb.reshape(HYENA_ORDER, HYENA_W, 1), hspec_c, cfwd_c, cinv_c, nb=nb)
            h_ctx = _merge(h_ctx, ug_c, gb, yt_c, att_c, hp, mp, wo, mods, i, tpb=tpb_ctx, base=nb)
            h_ctx = _ffn(h_ctx, n2, mods, w_ffi, w_ffo, final_norm_g, i, tpb=tpb_ctx, base=nb, final=False)

    return h_lat.reshape(nb, seq, d)
```

```python
import functools
import math

import numpy as np
import jax
import jax.numpy as jnp
from jax import lax
from jax.experimental import pallas as pl
from jax.experimental.pallas import tpu as pltpu

F32 = jnp.float32
BF16 = jnp.bfloat16
HIGHEST = lax.Precision.HIGHEST

D_MODEL = 1024
GRID_W = 64
NORM_EPS = 1e-6

HYENA_W = 512
HYENA_ORDER = 2
N_BANDS = 16
FILTER_HIDDEN = 64
HYENA_TARGET = 1e-2
HYENA_MAX_DECAY = math.log(HYENA_TARGET) / 0.3
HYENA_MIN_DECAY = math.log(HYENA_TARGET) / 1.5

MLA_HEADS = 16
QK_NOPE = 64
QK_ROPE = 32
V_HEAD = 64
Q_LORA = 384
KV_LORA = 256
ROPE_BASE = 10000.0
AXIS_ROPE = QK_ROPE // 2
ATTN_SCALE = (QK_NOPE + QK_ROPE) ** -0.5
Q_SCALE = ATTN_SCALE * math.log2(math.e)
D_FF = 2816

HY_COLS = (HYENA_ORDER + 1) * HYENA_W
OFF_Q = HY_COLS
OFF_KV = OFF_Q + Q_LORA
OFF_KR = OFF_KV + KV_LORA
OFF_G = OFF_KR + QK_ROPE

HEAD_PAD = 128
ROPE_LANE = QK_NOPE
QKV_COLS = Q_LORA + KV_LORA + 2 * HEAD_PAD
HEADS_PER_STEP = 4
KEY_SPLIT = 1024
NMOD = 16

FFT_N1 = 64
FFT_N2 = 128
VMEM_LIMIT = 56 * 1024 * 1024


def _cparams(*sem):
    return pltpu.CompilerParams(dimension_semantics=sem, vmem_limit_bytes=VMEM_LIMIT)


def _rms(x, g):
    return x * lax.rsqrt(jnp.mean(x * x, axis=-1, keepdims=True) + NORM_EPS) * g


def _silu(x):
    return x * (1.0 / (1.0 + jnp.exp(-x)))


def _sigmoid(x):
    return 1.0 / (1.0 + jnp.exp(-x))


def _mod_row(mods_ref, row, k):
    return mods_ref[pl.ds(row, 1), k * D_MODEL:(k + 1) * D_MODEL]


def _ada_kernel(c_ref, w_ref, b_ref, o_ref):
    s = _silu(c_ref[...])
    o_ref[...] = jnp.dot(s, w_ref[...], preferred_element_type=F32, precision=HIGHEST) + b_ref[...]


def _ada(cc, ada_w, ada_b):
    depth = ada_w.shape[0]
    tn = D_MODEL
    return pl.pallas_call(
        _ada_kernel,
        grid=(depth, 6 * D_MODEL // tn),
        in_specs=[pl.BlockSpec((NMOD, D_MODEL), lambda l, j: (0, 0)),
                  pl.BlockSpec((None, D_MODEL, tn), lambda l, j: (l, 0, j)),
                  pl.BlockSpec((None, 1, tn), lambda l, j: (l, 0, j))],
        out_specs=pl.BlockSpec((None, NMOD, tn), lambda l, j: (l, 0, j)),
        out_shape=jax.ShapeDtypeStruct((depth, NMOD, 6 * D_MODEL), F32),
        compiler_params=_cparams("arbitrary", "arbitrary"),
        name="ada",
    )(cc, ada_w, ada_b.reshape(depth, 1, 6 * D_MODEL))


def _in_proj_kernel(h_ref, g_ref, mods_ref, wqkv_ref, wg_ref, wht_ref, uq_ref, ug_ref, ut_ref, *, tpb, base):
    row = base + pl.program_id(0) // tpb
    y = _rms(h_ref[...], g_ref[...])
    a = (y * (1.0 + _mod_row(mods_ref, row, 1)) + _mod_row(mods_ref, row, 0)).astype(BF16)
    uq_ref[...] = jnp.dot(a, wqkv_ref[...], preferred_element_type=F32).astype(BF16)
    ug_ref[...] = jnp.dot(a, wg_ref[...], preferred_element_type=F32).astype(BF16)
    ut_ref[...] = lax.dot_general(wht_ref[...], a, (((1,), (1,)), ((), ())), preferred_element_type=F32)


def _in_proj(h, g, mods, w_qkv, w_gate, w_hy_t, layer, *, tpb, base, tm=512):
    t = h.shape[0]
    const = lambda i: (0, 0)
    wmap = lambda i: (layer, 0, 0)
    return pl.pallas_call(
        functools.partial(_in_proj_kernel, tpb=tpb, base=base),
        grid=(t // tm,),
        in_specs=[pl.BlockSpec((tm, D_MODEL), lambda i: (i, 0)),
                  pl.BlockSpec((1, D_MODEL), const),
                  pl.BlockSpec((NMOD, 6 * D_MODEL), const),
                  pl.BlockSpec((None, D_MODEL, QKV_COLS), wmap),
                  pl.BlockSpec((None, D_MODEL, 2 * D_MODEL), wmap),
                  pl.BlockSpec((None, HY_COLS, D_MODEL), wmap)],
        out_specs=[pl.BlockSpec((tm, QKV_COLS), lambda i: (i, 0)),
                   pl.BlockSpec((tm, 2 * D_MODEL), lambda i: (i, 0)),
                   pl.BlockSpec((HY_COLS, tm), lambda i: (0, i))],
        out_shape=[jax.ShapeDtypeStruct((t, QKV_COLS), BF16),
                   jax.ShapeDtypeStruct((t, 2 * D_MODEL), BF16),
                   jax.ShapeDtypeStruct((HY_COLS, t), F32)],
        compiler_params=_cparams("parallel"),
        name="in_proj",
    )(h, g.reshape(1, D_MODEL), mods, w_qkv, w_gate, w_hy_t)


def _mm_hi_kernel(a_ref, w_ref, o_ref):
    o_ref[...] = jnp.dot(a_ref[...], w_ref[...], preferred_element_type=F32, precision=HIGHEST)


def _matmul_hi(a, w, tm):
    m, k = a.shape
    n = w.shape[1]
    return pl.pallas_call(
        _mm_hi_kernel,
        grid=(m // tm,),
        in_specs=[pl.BlockSpec((tm, k), lambda i: (i, 0)),
                  pl.BlockSpec((k, n), lambda i: (0, 0))],
        out_specs=pl.BlockSpec((tm, n), lambda i: (i, 0)),
        out_shape=jax.ShapeDtypeStruct((m, n), F32),
        compiler_params=_cparams("parallel"),
        name="matmul_hi",
    )(a, w)


def _qkv_kernel(u_ref, qg_ref, kvg_ref, wq_ref, wk_ref, wv_ref, ct_ref, st_ref, q_ref, k_ref, v_ref):
    u = u_ref[...].astype(F32)
    nq = _rms(u[:, :Q_LORA], qg_ref[...]).astype(BF16)
    nkv = _rms(u[:, Q_LORA:Q_LORA + KV_LORA], kvg_ref[...]).astype(BF16)
    ka = u[:, Q_LORA + KV_LORA:Q_LORA + KV_LORA + HEAD_PAD]
    kb = u[:, Q_LORA + KV_LORA + HEAD_PAD:]
    ct = ct_ref[...]
    st = st_ref[...]
    kr = ka * ct + kb * st
    qq = jnp.dot(nq, wq_ref[...], preferred_element_type=F32)
    kn = jnp.dot(nkv, wk_ref[...], preferred_element_type=F32)
    nh = MLA_HEADS * HEAD_PAD
    for h in range(MLA_HEADS):
        lo, hi = h * HEAD_PAD, (h + 1) * HEAD_PAD
        qh = qq[:, lo:hi] * ct + qq[:, nh + lo:nh + hi] * st
        q_ref[:, lo:hi] = (qh * Q_SCALE).astype(BF16)
        k_ref[:, lo:hi] = (kn[:, lo:hi] + kr).astype(BF16)
    v_ref[...] = lax.dot_general(wv_ref[...], nkv, (((1,), (1,)), ((), ())),
                                 preferred_element_type=F32).astype(BF16)


def _qkv_up(u, qg, kvg, wq, wk, wvt, ctab, stab, layer, *, rope, tm=512):
    t = u.shape[0]
    ntab = ctab.shape[0] // tm
    tab_map = (lambda i: (i % ntab, 0)) if rope else (lambda i: (0, 0))
    nh = MLA_HEADS * HEAD_PAD
    nv = MLA_HEADS * V_HEAD
    wmap = lambda i: (layer, 0, 0)
    return pl.pallas_call(
        _qkv_kernel,
        grid=(t // tm,),
        in_specs=[pl.BlockSpec((tm, QKV_COLS), lambda i: (i, 0)),
                  pl.BlockSpec((None, 1, Q_LORA), wmap),
                  pl.BlockSpec((None, 1, KV_LORA), wmap),
                  pl.BlockSpec((None, Q_LORA, 2 * nh), wmap),
                  pl.BlockSpec((None, KV_LORA, nh), wmap),
                  pl.BlockSpec((None, nv, KV_LORA), wmap),
                  pl.BlockSpec((tm, HEAD_PAD), tab_map),
                  pl.BlockSpec((tm, HEAD_PAD), tab_map)],
        out_specs=[pl.BlockSpec((tm, nh), lambda i: (i, 0)),
                   pl.BlockSpec((tm, nh), lambda i: (i, 0)),
                   pl.BlockSpec((nv, tm), lambda i: (0, i))],
        out_shape=[jax.ShapeDtypeStruct((t, nh), BF16),
                   jax.ShapeDtypeStruct((t, nh), BF16),
                   jax.ShapeDtypeStruct((nv, t), BF16)],
        compiler_params=_cparams("parallel"),
        name="qkv_up",
    )(u, qg, kvg, wq, wk, wvt, ctab, stab)


def _attn_kernel(*refs, lks, n_sub, tq):
    n_kv = len(lks)
    q_ref = refs[0]
    k_refs = refs[1:1 + n_kv]
    v_refs = refs[1 + n_kv:1 + 2 * n_kv]
    o_ref = refs[1 + 2 * n_kv]
    nt = (((1,), (1,)), ((), ()))
    units = [(j, h) for j in range(n_sub) for h in range(HEADS_PER_STEP)]

    def scores(j, h):
        q = q_ref[j * tq:(j + 1) * tq, h * HEAD_PAD:(h + 1) * HEAD_PAD]
        out = []
        for k_ref, lk in zip(k_refs, lks):
            step = min(lk, KEY_SPLIT)
            for c in range(0, lk, step):
                out.append(lax.dot_general(k_ref[c:c + step, h * HEAD_PAD:(h + 1) * HEAD_PAD], q, nt,
                                           preferred_element_type=F32))
        return out

    s_next = scores(*units[0])
    for idx, (j, h) in enumerate(units):
        s = s_next
        if idx + 1 < len(units):
            s_next = scores(*units[idx + 1])
        m = s[0].max(axis=0, keepdims=True)
        for si in s[1:]:
            m = jnp.maximum(m, si.max(axis=0, keepdims=True))
        p = [jnp.exp2(si - m) for si in s]
        l = p[0].sum(axis=0, keepdims=True)
        for pi in p[1:]:
            l = l + pi.sum(axis=0, keepdims=True)
        o, i = None, 0
        for v_ref, lk in zip(v_refs, lks):
            step = min(lk, KEY_SPLIT)
            for c in range(0, lk, step):
                d = jnp.dot(v_ref[h * V_HEAD:(h + 1) * V_HEAD, c:c + step], p[i].astype(BF16),
                            preferred_element_type=F32)
                o = d if o is None else o + d
                i += 1
        o_ref[h * V_HEAD:(h + 1) * V_HEAD, j * tq:(j + 1) * tq] = (o / l).astype(BF16)


def _attention(q, kv_sets, *, nb, tq=256, n_sub=4):
    t = q.shape[0]
    lq = t // nb
    n_sub = min(n_sub, lq // tq)
    tb = tq * n_sub
    nq = lq // tb
    groups = MLA_HEADS // HEADS_PER_STEP
    qw = HEADS_PER_STEP * HEAD_PAD
    vw = HEADS_PER_STEP * V_HEAD
    in_specs = [pl.BlockSpec((tb, qw), lambda b, g, i: (b * nq + i, g))]
    k_specs, v_specs, ks, vs = [], [], [], []
    for k, vt, lk in kv_sets:
        k_specs.append(pl.BlockSpec((lk, qw), lambda b, g, i: (b, g)))
        v_specs.append(pl.BlockSpec((vw, lk), lambda b, g, i: (g, b)))
        ks.append(k)
        vs.append(vt)
    lks = tuple(lk for _, _, lk in kv_sets)
    return pl.pallas_call(
        functools.partial(_attn_kernel, lks=lks, n_sub=n_sub, tq=tq),
        grid=(nb, groups, nq),
        in_specs=in_specs + k_specs + v_specs,
        out_specs=pl.BlockSpec((vw, tb), lambda b, g, i: (g, b * nq + i)),
        out_shape=jax.ShapeDtypeStruct((MLA_HEADS * V_HEAD, t), BF16),
        compiler_params=_cparams("parallel", "parallel", "arbitrary"),
        name="attention",
    )(q, *ks, *vs)


def _filt_mlp_kernel(z_ref, w1_ref, b1_ref, w2_ref, b2_ref, f0_ref, f1_ref, o_ref):
    h = jnp.dot(w1_ref[...], z_ref[...], preferred_element_type=F32, precision=HIGHEST) + b1_ref[...]
    h = jnp.sin(f0_ref[...] * h)
    h = jnp.dot(w2_ref[...], h, preferred_element_type=F32, precision=HIGHEST) + b2_ref[...]
    o_ref[...] = jnp.sin(f1_ref[...] * h)


def _filt_mlp(zt, w1, b1, w2, b2, freq):
    kz, n = zt.shape
    w1t = jnp.zeros((FILTER_HIDDEN, kz), F32).at[:, :w1.shape[0]].set(w1.T)
    col = lambda a: a.reshape(FILTER_HIDDEN, 1)
    args = (zt, w1t, col(b1), w2.T, col(b2), col(freq[0]), col(freq[1]))
    return pl.pallas_call(
        _filt_mlp_kernel,
        grid=(1,),
        in_specs=[pl.BlockSpec(a.shape, lambda i: (0, 0)) for a in args],
        out_specs=pl.BlockSpec((FILTER_HIDDEN, n), lambda i: (0, 0)),
        out_shape=jax.ShapeDtypeStruct((FILTER_HIDDEN, n), F32),
        compiler_params=_cparams("arbitrary"),
        name="filt_mlp",
    )(*args)


def _filt_taps_kernel(h_ref, w3_ref, d_ref, t_ref, o_ref, *, seq):
    n = h_ref.shape[1]
    pos = lax.broadcasted_iota(jnp.int32, (1, n), 1)
    win = jnp.exp(-t_ref[...] * d_ref[...])
    h = h_ref[...]
    for o in range(HYENA_ORDER):
        fwd = jnp.dot(w3_ref[0, o], h, preferred_element_type=F32, precision=HIGHEST)
        bwd = jnp.dot(w3_ref[1, o], h, preferred_element_type=F32, precision=HIGHEST)
        full = jnp.where(pos < seq, fwd, jnp.where(pos > seq, bwd, 0.0)) * win
        o_ref[o] = full / jnp.sum(jnp.abs(full), axis=1, keepdims=True)


def _filt_taps(h2t, w3, dabs, tvec, seq, cf=64):
    n = h2t.shape[1]
    w3t = w3.T.reshape(2, HYENA_ORDER, HYENA_W, FILTER_HIDDEN)
    return pl.pallas_call(
        functools.partial(_filt_taps_kernel, seq=seq),
        grid=(HYENA_W // cf,),
        in_specs=[pl.BlockSpec((FILTER_HIDDEN, n), lambda c: (0, 0)),
                  pl.BlockSpec((2, HYENA_ORDER, cf, FILTER_HIDDEN), lambda c: (0, 0, c, 0)),
                  pl.BlockSpec((cf, 1), lambda c: (c, 0)),
                  pl.BlockSpec((1, n), lambda c: (0, 0))],
        out_specs=pl.BlockSpec((HYENA_ORDER, cf, n), lambda c: (0, c, 0)),
        out_shape=jax.ShapeDtypeStruct((HYENA_ORDER, HYENA_W, n), F32),
        compiler_params=_cparams("parallel"),
        name="filt_taps",
    )(h2t, w3t, dabs, tvec)


def _filt_spec_kernel(x_ref, f1_ref, tc_ref, ts_ref, f2_ref, o_ref):
    ns = x_ref.shape[0]
    x = jnp.concatenate([x_ref[s:s + 1, :].reshape(FFT_N1, FFT_N2) for s in range(ns)], axis=1).astype(BF16)
    a = jnp.dot(f1_ref[...], x, preferred_element_type=F32)
    tc = tc_ref[...]
    ts = ts_ref[...]
    rows = []
    for s in range(ns):
        ar = a[:FFT_N1, s * FFT_N2:(s + 1) * FFT_N2]
        ai = a[FFT_N1:, s * FFT_N2:(s + 1) * FFT_N2]
        rows.append(jnp.concatenate([ar * tc + ai * ts, ai * tc - ar * ts], axis=1))
    b = jnp.concatenate(rows, axis=0).astype(BF16)
    z = jnp.dot(b, f2_ref[...], preferred_element_type=F32)
    o_ref[...] = z.reshape(ns, FFT_N1, 2 * FFT_N2)


def _filt_spec(hfull, consts, ns=16):
    r = hfull.shape[0]
    f1h, tc, ts, f2 = consts["f1h"], consts["tc"], consts["ts"], consts["f2"]
    const = lambda i: (0, 0)
    return pl.pallas_call(
        _filt_spec_kernel,
        grid=(r // ns,),
        in_specs=[pl.BlockSpec((ns, FFT_N1 * FFT_N2), lambda i: (i, 0)),
                  pl.BlockSpec(f1h.shape, const), pl.BlockSpec(tc.shape, const),
                  pl.BlockSpec(ts.shape, const), pl.BlockSpec(f2.shape, const)],
        out_specs=pl.BlockSpec((ns, FFT_N1, 2 * FFT_N2), lambda i: (i, 0, 0)),
        out_shape=jax.ShapeDtypeStruct((r, FFT_N1, 2 * FFT_N2), F32),
        compiler_params=_cparams("parallel"),
        name="filt_spec",
    )(hfull, f1h, tc, ts, f2)


def _short_conv_tile(x, w, b):
    nrow = x.shape[1]
    lane = lax.broadcasted_iota(jnp.int32, x.shape, 2)
    row = lax.broadcasted_iota(jnp.int32, x.shape, 1)
    r1 = pltpu.roll(x, 1, 2)
    r2 = pltpu.roll(r1, 1, 1)
    prev = jnp.where(lane == 0, jnp.where(row == 0, 0.0, r2), r1)
    l1 = pltpu.roll(x, FFT_N2 - 1, 2)
    l2 = pltpu.roll(l1, nrow - 1, 1)
    nxt = jnp.where(lane == FFT_N2 - 1, jnp.where(row == nrow - 1, 0.0, l2), l1)
    return prev * w[0] + x * w[1] + nxt * w[2] + b


def _long_conv_pair(vr, vi, h, f1, f1i, f2, f2i, tc, ts):
    ns = vr.shape[0]
    half = FFT_N2
    x = jnp.concatenate([jnp.concatenate([vr[s] for s in range(ns)], axis=1),
                         jnp.concatenate([vi[s] for s in range(ns)], axis=1)], axis=0).astype(BF16)
    a = jnp.dot(f1, x, preferred_element_type=F32)
    rows = []
    for s in range(ns):
        ar = a[:FFT_N1, s * half:(s + 1) * half]
        ai = a[FFT_N1:, s * half:(s + 1) * half]
        rows.append(jnp.concatenate([ar * tc + ai * ts, ai * tc - ar * ts], axis=1))
    b = jnp.concatenate(rows, axis=0).astype(BF16)
    z = jnp.dot(b, f2, preferred_element_type=F32).reshape(ns, FFT_N1, 2 * half)
    zr, zi = z[:, :, :half], z[:, :, half:]
    hr, hi = h[:, :, :half], h[:, :, half:]
    w = jnp.concatenate([zr * hr - zi * hi, zr * hi + zi * hr], axis=2)
    w = w.reshape(ns * FFT_N1, 2 * half).astype(BF16)
    c = jnp.dot(w, f2i, preferred_element_type=F32).reshape(ns, FFT_N1, 2 * half)
    cr, ci = c[:, :, :half], c[:, :, half:]
    dr = cr * tc - ci * ts
    di = cr * ts + ci * tc
    d = jnp.concatenate([jnp.concatenate([dr[s] for s in range(ns)], axis=1),
                         jnp.concatenate([di[s] for s in range(ns)], axis=1)], axis=0).astype(BF16)
    y = jnp.dot(f1i, d, preferred_element_type=F32)
    nrow = y.shape[0] // 2
    yr = jnp.stack([y[:nrow, s * half:(s + 1) * half] for s in range(ns)], axis=0)
    yi = jnp.stack([y[nrow:, s * half:(s + 1) * half] for s in range(ns)], axis=0)
    return yr, yi


def _hyena_kernel(x0_ref, x1_ref, v_ref, cw_ref, cb_ref, hb_ref, h_ref,
                  f1_ref, f1i_ref, f2_ref, f2i_ref, tc_ref, ts_ref, o_ref):
    consts = (f1_ref[...], f1i_ref[...], f2_ref[...], f2i_ref[...], tc_ref[...], ts_ref[...])
    g0, g1, v = [], [], []
    for half in range(2):
        g0.append(_short_conv_tile(x0_ref[:, half], cw_ref[0], cb_ref[0]))
        g1.append(_short_conv_tile(x1_ref[:, half], cw_ref[1], cb_ref[1]))
        v.append(_short_conv_tile(v_ref[:, half], cw_ref[2], cb_ref[2]))
    gates = (g0, g1)
    for o in range(HYENA_ORDER):
        yr, yi = _long_conv_pair(v[0], v[1], h_ref[o], *consts)
        v = [gates[o][0] * (yr + hb_ref[o] * v[0]), gates[o][1] * (yi + hb_ref[o] * v[1])]
    o_ref[:, 0] = v[0].astype(BF16)
    o_ref[:, 1] = v[1].astype(BF16)


def _hyena_latent(ut, cw, cb, hb, hspec, consts, *, nb, ct=32):
    seq = ut.shape[1] // nb
    nrow = seq // FFT_N2
    npair = nb // 2
    nct = HYENA_W // ct
    u5 = ut.reshape(3 * HYENA_W, 2, npair, nrow, FFT_N2)
    part = lambda p: pl.BlockSpec((ct, 2, None, nrow, FFT_N2), lambda c, j: (p * nct + c, 0, j, 0, 0))
    const2 = lambda c, j: (0, 0)
    cnames = ("f1", "f1i", "f2", "f2i", "tc", "ts")
    out = pl.pallas_call(
        _hyena_kernel,
        grid=(nct, npair),
        in_specs=[part(0), part(1), part(2),
                  pl.BlockSpec((3, 3, ct, 1, 1), lambda c, j: (0, 0, c, 0, 0)),
                  pl.BlockSpec((3, ct, 1, 1), lambda c, j: (0, c, 0, 0)),
                  pl.BlockSpec((HYENA_ORDER, ct, 1, 1), lambda c, j: (0, c, 0, 0)),
                  pl.BlockSpec((HYENA_ORDER, ct, FFT_N1, 2 * FFT_N2), lambda c, j: (0, c, 0, 0))]
                 + [pl.BlockSpec(consts[k].shape, const2) for k in cnames],
        out_specs=pl.BlockSpec((ct, 2, None, nrow, FFT_N2), lambda c, j: (c, 0, j, 0, 0)),
        out_shape=jax.ShapeDtypeStruct((HYENA_W, 2, npair, nrow, FFT_N2), BF16),
        compiler_params=_cparams("parallel", "arbitrary"),
        name="hyena",
    )(u5, u5, u5, cw, cb, hb, hspec, *[consts[k] for k in cnames])
    return out.reshape(HYENA_W, nb * seq)


def _hyena_ctx_kernel(x0_ref, x1_ref, v_ref, cw_ref, cb_ref, hb_ref, h_ref, cf_ref, ci_ref, o_ref, *, nb):
    seq = x0_ref.shape[1] // nb
    cc = x0_ref.shape[0]
    half = h_ref.shape[2] // 2

    def rows(ref, p):
        x = jnp.concatenate([ref[:, b * seq:(b + 1) * seq] for b in range(nb)], axis=0)
        lane = lax.broadcasted_iota(jnp.int32, x.shape, 1)
        prev = jnp.where(lane == 0, 0.0, pltpu.roll(x, 1, 1))
        nxt = jnp.where(lane == seq - 1, 0.0, pltpu.roll(x, seq - 1, 1))
        w = [jnp.concatenate([cw_ref[p, j]] * nb, axis=0) for j in range(3)]
        return prev * w[0] + x * w[1] + nxt * w[2] + jnp.concatenate([cb_ref[p]] * nb, axis=0)

    g = (rows(x0_ref, 0), rows(x1_ref, 1))
    v = rows(v_ref, 2)
    for o in range(HYENA_ORDER):
        spec = jnp.dot(v.astype(BF16), cf_ref[...], preferred_element_type=F32)
        xr, xi = spec[:, :half], spec[:, half:]
        hh = jnp.concatenate([h_ref[o]] * nb, axis=0)
        hr, hi = hh[:, :half], hh[:, half:]
        y = jnp.concatenate([xr * hr - xi * hi, xr * hi + xi * hr], axis=1).astype(BF16)
        conv = jnp.dot(y, ci_ref[...], preferred_element_type=F32)
        v = g[o] * (conv + jnp.concatenate([hb_ref[o]] * nb, axis=0) * v)
    for b in range(nb):
        o_ref[:, b * seq:(b + 1) * seq] = v[b * cc:(b + 1) * cc].astype(BF16)


def _hyena_ctx(ut, cw, cb, hb, hspec, cfwd, cinv, *, nb, cc=64):
    n = ut.shape[1]
    ncc = HYENA_W // cc
    part = lambda p: pl.BlockSpec((cc, n), lambda c: (p * ncc + c, 0))
    return pl.pallas_call(
        functools.partial(_hyena_ctx_kernel, nb=nb),
        grid=(ncc,),
        in_specs=[part(0), part(1), part(2),
                  pl.BlockSpec((3, 3, cc, 1), lambda c: (0, 0, c, 0)),
                  pl.BlockSpec((3, cc, 1), lambda c: (0, c, 0)),
                  pl.BlockSpec((HYENA_ORDER, cc, 1), lambda c: (0, c, 0)),
                  pl.BlockSpec((HYENA_ORDER, cc, hspec.shape[2]), lambda c: (0, c, 0)),
                  pl.BlockSpec(cfwd.shape, lambda c: (0, 0)),
                  pl.BlockSpec(cinv.shape, lambda c: (0, 0))],
        out_specs=pl.BlockSpec((cc, n), lambda c: (c, 0)),
        out_shape=jax.ShapeDtypeStruct((HYENA_W, n), BF16),
        compiler_params=_cparams("parallel"),
        name="hyena_ctx",
    )(ut, ut, ut, cw, cb, hb, hspec, cfwd, cinv)


def _merge_kernel(h_ref, ug_ref, gb_ref, yt_ref, att_ref, hp_ref, mp_ref, wo_ref, mods_ref, o_ref, *, tpb, base):
    row = base + pl.program_id(0) // tpb
    g = _sigmoid(ug_ref[...].astype(F32) + gb_ref[...])
    yh = lax.dot_general(yt_ref[...], hp_ref[...], (((0,), (0,)), ((), ())), preferred_element_type=F32)
    ya = lax.dot_general(att_ref[...], mp_ref[...], (((0,), (0,)), ((), ())), preferred_element_type=F32)
    mix = (g[:, :D_MODEL] * yh + g[:, D_MODEL:] * ya).astype(BF16)
    out = jnp.dot(mix, wo_ref[...], preferred_element_type=F32)
    o_ref[...] = h_ref[...] + _mod_row(mods_ref, row, 2) * out


def _merge(h, ug, gate_b, yt, att, hy_proj, mla_proj, w_out, mods, layer, *, tpb, base, tm=512):
    t = h.shape[0]
    const = lambda i: (0, 0)
    wmap = lambda i: (layer, 0, 0)
    return pl.pallas_call(
        functools.partial(_merge_kernel, tpb=tpb, base=base),
        grid=(t // tm,),
        in_specs=[pl.BlockSpec((tm, D_MODEL), lambda i: (i, 0)),
                  pl.BlockSpec((tm, 2 * D_MODEL), lambda i: (i, 0)),
                  pl.BlockSpec((None, 1, 2 * D_MODEL), wmap),
                  pl.BlockSpec((HYENA_W, tm), lambda i: (0, i)),
                  pl.BlockSpec((MLA_HEADS * V_HEAD, tm), lambda i: (0, i)),
                  pl.BlockSpec((None,) + hy_proj.shape[1:], wmap),
                  pl.BlockSpec((None,) + mla_proj.shape[1:], wmap),
                  pl.BlockSpec((None,) + w_out.shape[1:], wmap),
                  pl.BlockSpec((NMOD, 6 * D_MODEL), const)],
        out_specs=pl.BlockSpec((tm, D_MODEL), lambda i: (i, 0)),
        out_shape=jax.ShapeDtypeStruct((t, D_MODEL), F32),
        compiler_params=_cparams("parallel"),
        name="merge",
    )(h, ug, gate_b, yt, att, hy_proj, mla_proj, w_out, mods)


FF_CHUNK = 256


def _ffn_kernel(h_ref, g_ref, mods_ref, wi_ref, wo_ref, fg_ref, o_ref, *, tpb, base, final):
    row = base + pl.program_id(0) // tpb
    h = h_ref[...]
    f = (_rms(h, g_ref[...]) * (1.0 + _mod_row(mods_ref, row, 4)) + _mod_row(mods_ref, row, 3)).astype(BF16)
    y = None
    for c in range(0, D_FF, FF_CHUNK):
        gate = jnp.dot(f, wi_ref[:, c:c + FF_CHUNK], preferred_element_type=F32)
        up = jnp.dot(f, wi_ref[:, D_FF + c:D_FF + c + FF_CHUNK], preferred_element_type=F32)
        act = (_silu(gate) * up).astype(BF16)
        d = jnp.dot(act, wo_ref[c:c + FF_CHUNK, :], preferred_element_type=F32)
        y = d if y is None else y + d
    out = h + _mod_row(mods_ref, row, 5) * y
    o_ref[...] = _rms(out, fg_ref[...]) if final else out


def _ffn(h, g, mods, w_in, w_down, final_g, layer, *, tpb, base, final, tm=512):
    t = h.shape[0]
    const = lambda i: (0, 0)
    wmap = lambda i: (layer, 0, 0)
    return pl.pallas_call(
        functools.partial(_ffn_kernel, tpb=tpb, base=base, final=final),
        grid=(t // tm,),
        in_specs=[pl.BlockSpec((tm, D_MODEL), lambda i: (i, 0)),
                  pl.BlockSpec((None, 1, D_MODEL), wmap),
                  pl.BlockSpec((NMOD, 6 * D_MODEL), const),
                  pl.BlockSpec((None, D_MODEL, 2 * D_FF), wmap),
                  pl.BlockSpec((None, D_FF, D_MODEL), wmap),
                  pl.BlockSpec((1, D_MODEL), const)],
        out_specs=pl.BlockSpec((tm, D_MODEL), lambda i: (i, 0)),
        out_shape=jax.ShapeDtypeStruct((t, D_MODEL), F32),
        compiler_params=_cparams("parallel"),
        name="ffn",
    )(h, g, mods, w_in, w_down, final_g.reshape(1, -1))


def _dft_consts():
    n = FFT_N1 * FFT_N2
    k1 = np.arange(FFT_N1)
    th = 2 * np.pi * np.outer(k1, np.arange(FFT_N1)) / FFT_N1
    c, s = np.cos(th), np.sin(th)
    hn = FFT_N1 // 2
    f1 = np.block([[c[:, :hn], s[:, :hn]], [-s[:, :hn], c[:, :hn]]])
    f1h = np.concatenate([c, -s], axis=0)
    f1i = np.block([[c[:, :hn].T, -s[:, :hn].T], [s[:, :hn].T, c[:, :hn].T]]) / n
    ph = 2 * np.pi * np.outer(k1, np.arange(FFT_N2)) / n
    th2 = 2 * np.pi * np.outer(np.arange(FFT_N2), np.arange(FFT_N2)) / FFT_N2
    c2, s2 = np.cos(th2), np.sin(th2)
    f2 = np.block([[c2, -s2], [s2, c2]])
    f2i = np.block([[c2, s2], [-s2, c2]])
    return {"f1": jnp.asarray(f1, BF16), "f1h": jnp.asarray(f1h, BF16), "f1i": jnp.asarray(f1i, BF16),
            "f2": jnp.asarray(f2, BF16), "f2i": jnp.asarray(f2i, BF16),
            "tc": jnp.asarray(np.cos(ph), F32), "ts": jnp.asarray(np.sin(ph), F32)}


def _dense_dft(seq):
    n = 2 * seq
    th = 2 * np.pi * np.outer(np.arange(n), np.arange(n)) / n
    c, s = np.cos(th), np.sin(th)
    full = np.concatenate([c, -s], axis=1)
    inv = np.concatenate([c[:, :seq], -s[:, :seq]], axis=0) / n
    return jnp.asarray(full, F32), jnp.asarray(full[:seq], BF16), jnp.asarray(inv, BF16)


def _pos_features(seq):
    n = np.arange(2 * seq)
    lag = np.minimum(np.where(n < seq, n, 2 * seq - n), seq - 1)
    t = np.linspace(0.0, 1.0, seq, dtype=np.float32)[lag]
    bands = np.arange(1, N_BANDS + 1, dtype=np.float32)
    ang = (np.float32(2 * math.pi) * bands)[:, None] * t[None, :]
    zt = np.concatenate([t[None, :], np.cos(ang), np.sin(ang), np.zeros((7, 2 * seq), np.float32)], axis=0)
    return jnp.asarray(zt, F32), jnp.asarray(t.reshape(1, -1), F32)


def _rope_tables(seq):
    rows = seq // GRID_W
    row = np.repeat(np.arange(rows, dtype=np.float32), GRID_W)
    col = np.tile(np.arange(GRID_W, dtype=np.float32), rows)
    inv = (np.float32(ROPE_BASE) ** (-np.arange(0, AXIS_ROPE, 2, dtype=np.float32) / AXIS_ROPE)).astype(np.float32)
    ang = np.concatenate([row[:, None] * inv, col[:, None] * inv], axis=-1)
    cos, sin = np.cos(ang), np.sin(ang)
    ones = np.ones((seq, QK_NOPE), np.float32)
    pad = HEAD_PAD - QK_NOPE - QK_ROPE
    ctab = np.concatenate([ones, cos, cos, np.ones((seq, pad), np.float32)], axis=1)
    stab = np.concatenate([0 * ones, sin, sin, np.zeros((seq, pad), np.float32)], axis=1)
    return jnp.asarray(ctab, F32), jnp.asarray(stab, F32)


def _rot_cols(w):
    half = QK_ROPE // 2
    return jnp.concatenate([-w[..., half:], w[..., :half]], axis=-1)


def _pad_heads(nope, rope):
    lead = nope.shape[:-2]
    z = jnp.zeros(lead + (MLA_HEADS, HEAD_PAD - QK_NOPE - QK_ROPE), nope.dtype)
    return jnp.concatenate([nope, rope, z], axis=-1).reshape(lead + (MLA_HEADS * HEAD_PAD,))


def _prep_weights(w_in, w_uq, w_ukv):
    depth, d = w_in.shape[0], w_in.shape[1]
    w_hy_t = jnp.swapaxes(w_in[:, :, :OFF_Q], 1, 2).astype(BF16)
    kr_w = w_in[:, :, OFF_KR:OFF_G]
    zl = jnp.zeros((depth, d, ROPE_LANE), F32)
    zr = jnp.zeros((depth, d, HEAD_PAD - ROPE_LANE - QK_ROPE), F32)
    w_qkv = jnp.concatenate([w_in[:, :, OFF_Q:OFF_KR], zl, kr_w, zr, zl, _rot_cols(kr_w), zr], axis=2).astype(BF16)
    w_gate = w_in[:, :, OFF_G:].astype(BF16)
    uq = w_uq.reshape(depth, Q_LORA, MLA_HEADS, QK_NOPE + QK_ROPE)
    wq = jnp.concatenate([_pad_heads(uq[..., :QK_NOPE], uq[..., QK_NOPE:]),
                          _pad_heads(0 * uq[..., :QK_NOPE], _rot_cols(uq[..., QK_NOPE:]))], axis=2).astype(BF16)
    ukv = w_ukv.reshape(depth, KV_LORA, MLA_HEADS, QK_NOPE + V_HEAD)
    wk = _pad_heads(ukv[..., :QK_NOPE], jnp.zeros((depth, KV_LORA, MLA_HEADS, QK_ROPE), F32)).astype(BF16)
    wvt = jnp.swapaxes(ukv[..., QK_NOPE:].reshape(depth, KV_LORA, MLA_HEADS * V_HEAD), 1, 2).astype(BF16)
    return w_hy_t, w_qkv, w_gate, wq, wk, wvt


def kernel(x, c, ctx, c_ctx, ada_w, ada_b, norm1_g, norm2_g, w_in, gate_b, hy_conv_w, hy_conv_b,
           hy_filt_w1, hy_filt_b1, hy_filt_w2, hy_filt_b2, hy_filt_w3, hy_filt_freq, hy_bias, hy_proj,
           q_norm_g, kv_norm_g, w_uq, w_ukv, mla_proj, w_out, ffn_w_in, ffn_w_out, final_norm_g):
    nb, seq, d = x.shape
    lc = ctx.shape[1]
    depth = ada_w.shape[0]
    assert d == D_MODEL and seq == FFT_N1 * FFT_N2 // 2 and nb % 2 == 0 and nb < NMOD
    tm = 512
    tpb_lat = seq // tm
    tpb_ctx = 1 << 30

    cc = jnp.zeros((NMOD, d), F32).at[:nb].set(c).at[nb].set(c_ctx)
    mods_all = _ada(cc, ada_w, ada_b)

    consts = _dft_consts()
    cfull_c, cfwd_c, cinv_c = _dense_dft(lc)
    ctab, stab = _rope_tables(seq)
    one_tab = jnp.ones((tm, HEAD_PAD), F32)
    zero_tab = jnp.zeros((tm, HEAD_PAD), F32)
    dabs = jnp.asarray(np.abs(np.linspace(HYENA_MIN_DECAY, HYENA_MAX_DECAY, HYENA_W, dtype=np.float32)).reshape(HYENA_W, 1))
    zt_lat, tv_lat = _pos_features(seq)
    zt_ctx, tv_ctx = _pos_features(lc)

    w_hy_t, w_qkv, w_gate, wq, wk, wvt = _prep_weights(w_in, w_uq, w_ukv)
    hp, mp, wo = hy_proj.astype(BF16), mla_proj.astype(BF16), w_out.astype(BF16)
    w_ffi, w_ffo = ffn_w_in.astype(BF16), ffn_w_out.astype(BF16)
    qg = q_norm_g.reshape(depth, 1, Q_LORA)
    kvg = kv_norm_g.reshape(depth, 1, KV_LORA)
    gb = gate_b.reshape(depth, 1, 2 * D_MODEL)
    n1 = norm1_g
    n2 = norm2_g.reshape(depth, 1, D_MODEL)
    cw_all = jnp.transpose(hy_conv_w.reshape(depth, 3, 3, HYENA_W), (0, 2, 1, 3))
    cb_all = hy_conv_b.reshape(depth, 3, HYENA_W)

    h_lat = x.reshape(nb * seq, d)
    h_ctx = ctx.reshape(nb * lc, d)

    for i in range(depth):
        last = i == depth - 1
        mods = mods_all[i]
        cw, cb, hb = cw_all[i], cb_all[i], hy_bias[i]

        filt = (hy_filt_w1[i], hy_filt_b1[i], hy_filt_w2[i], hy_filt_b2[i], hy_filt_freq[i])
        hfull = _filt_taps(_filt_mlp(zt_lat, *filt), hy_filt_w3[i], dabs, tv_lat, seq)
        hspec = _filt_spec(hfull.reshape(HYENA_ORDER * HYENA_W, 2 * seq), consts)
        hspec = hspec.reshape(HYENA_ORDER, HYENA_W, FFT_N1, 2 * FFT_N2)

        uq_l, ug_l, ut_l = _in_proj(h_lat, n1[i], mods, w_qkv, w_gate, w_hy_t, i, tpb=tpb_lat, base=0)
        uq_c, ug_c, ut_c = _in_proj(h_ctx, n1[i], mods, w_qkv, w_gate, w_hy_t, i, tpb=tpb_ctx, base=nb)
        q_l, k_l, v_l = _qkv_up(uq_l, qg, kvg, wq, wk, wvt, ctab, stab, i, rope=True)
        q_c, k_c, v_c = _qkv_up(uq_c, qg, kvg, wq, wk, wvt, one_tab, zero_tab, i, rope=False)
        att_l = _attention(q_l, [(k_l, v_l, seq), (k_c, v_c, lc)], nb=nb)
        yt_l = _hyena_latent(ut_l, cw.reshape(3, 3, HYENA_W, 1, 1), cb.reshape(3, HYENA_W, 1, 1),
                             hb.reshape(HYENA_ORDER, HYENA_W, 1, 1), hspec, consts, nb=nb)
        h_lat = _merge(h_lat, ug_l, gb, yt_l, att_l, hp, mp, wo, mods, i, tpb=tpb_lat, base=0)

        h_lat = _ffn(h_lat, n2, mods, w_ffi, w_ffo, final_norm_g, i, tpb=tpb_lat, base=0, final=last)

        if not last:
            hfull_c = _filt_taps(_filt_mlp(zt_ctx, *filt), hy_filt_w3[i], dabs, tv_ctx, lc)
            hspec_c = _matmul_hi(hfull_c.reshape(HYENA_ORDER * HYENA_W, 2 * lc), cfull_c, tm=256)
            hspec_c = hspec_c.reshape(HYENA_ORDER, HYENA_W, 4 * lc)
            att_c = _attention(q_c, [(k_c, v_c, lc)], nb=nb)
            yt_c = _hyena_ctx(ut_c, cw.reshape(3, 3, HYENA_W, 1), cb.reshape(3, HYENA_W, 1),
                              hb.reshape(HYENA_ORDER, HYENA_W, 1), hspec_c, cfwd_c, cinv_c, nb=nb)
            h_ctx = _merge(h_ctx, ug_c, gb, yt_c, att_c, hp, mp, wo, mods, i, tpb=tpb_ctx, base=nb)
            h_ctx = _ffn(h_ctx, n2, mods, w_ffi, w_ffo, final_norm_g, i, tpb=tpb_ctx, base=nb, final=False)

    return h_lat.reshape(nb, seq, d)
```

```python
import functools
import math

import numpy as np
import jax
import jax.numpy as jnp
from jax import lax
from jax.experimental import pallas as pl
from jax.experimental.pallas import tpu as pltpu

F32 = jnp.float32
BF16 = jnp.bfloat16
HIGHEST = lax.Precision.HIGHEST

D_MODEL = 1024
GRID_W = 64
NORM_EPS = 1e-6

HYENA_W = 512
HYENA_ORDER = 2
N_BANDS = 16
FILTER_HIDDEN = 64
HYENA_TARGET = 1e-2
HYENA_MAX_DECAY = math.log(HYENA_TARGET) / 0.3
HYENA_MIN_DECAY = math.log(HYENA_TARGET) / 1.5

MLA_HEADS = 16
QK_NOPE = 64
QK_ROPE = 32
V_HEAD = 64
Q_LORA = 384
KV_LORA = 256
ROPE_BASE = 10000.0
AXIS_ROPE = QK_ROPE // 2
ATTN_SCALE = (QK_NOPE + QK_ROPE) ** -0.5
Q_SCALE = ATTN_SCALE * math.log2(math.e)
D_FF = 2816

HY_COLS = (HYENA_ORDER + 1) * HYENA_W
OFF_Q = HY_COLS
OFF_KV = OFF_Q + Q_LORA
OFF_KR = OFF_KV + KV_LORA
OFF_G = OFF_KR + QK_ROPE

HEAD_PAD = 128
ROPE_LANE = QK_NOPE
QKV_COLS = Q_LORA + KV_LORA + 2 * HEAD_PAD
HEADS_PER_STEP = 4
KEY_SPLIT = 1024
NMOD = 16

FFT_N1 = 64
FFT_N2 = 128
VMEM_LIMIT = 56 * 1024 * 1024


def _cparams(*sem):
    return pltpu.CompilerParams(dimension_semantics=sem, vmem_limit_bytes=VMEM_LIMIT)


def _rms(x, g):
    return x * lax.rsqrt(jnp.mean(x * x, axis=-1, keepdims=True) + NORM_EPS) * g


def _silu(x):
    return x * (1.0 / (1.0 + jnp.exp(-x)))


def _sigmoid(x):
    return 1.0 / (1.0 + jnp.exp(-x))


def _mod_row(mods_ref, row, k):
    return mods_ref[pl.ds(row, 1), k * D_MODEL:(k + 1) * D_MODEL]


def _ada_kernel(c_ref, w_ref, b_ref, o_ref):
    s = _silu(c_ref[...])
    o_ref[...] = jnp.dot(s, w_ref[...], preferred_element_type=F32, precision=HIGHEST) + b_ref[...]


def _ada(cc, ada_w, ada_b):
    depth = ada_w.shape[0]
    tn = D_MODEL
    return pl.pallas_call(
        _ada_kernel,
        grid=(depth, 6 * D_MODEL // tn),
        in_specs=[pl.BlockSpec((NMOD, D_MODEL), lambda l, j: (0, 0)),
                  pl.BlockSpec((None, D_MODEL, tn), lambda l, j: (l, 0, j)),
                  pl.BlockSpec((None, 1, tn), lambda l, j: (l, 0, j))],
        out_specs=pl.BlockSpec((None, NMOD, tn), lambda l, j: (l, 0, j)),
        out_shape=jax.ShapeDtypeStruct((depth, NMOD, 6 * D_MODEL), F32),
        compiler_params=_cparams("arbitrary", "arbitrary"),
        name="ada",
    )(cc, ada_w, ada_b.reshape(depth, 1, 6 * D_MODEL))


def _in_proj_kernel(h_ref, g_ref, mods_ref, wqkv_ref, wg_ref, wht_ref, uq_ref, ug_ref, ut_ref, *, tpb, base):
    row = base + pl.program_id(0) // tpb
    y = _rms(h_ref[...], g_ref[...])
    a = (y * (1.0 + _mod_row(mods_ref, row, 1)) + _mod_row(mods_ref, row, 0)).astype(BF16)
    uq_ref[...] = jnp.dot(a, wqkv_ref[...], preferred_element_type=F32).astype(BF16)
    ug_ref[...] = jnp.dot(a, wg_ref[...], preferred_element_type=F32).astype(BF16)
    ut_ref[...] = lax.dot_general(wht_ref[...], a, (((1,), (1,)), ((), ())), preferred_element_type=F32)


def _in_proj(h, g, mods, w_qkv, w_gate, w_hy_t, layer, *, tpb, base, tm=512):
    t = h.shape[0]
    const = lambda i: (0, 0)
    wmap = lambda i: (layer, 0, 0)
    return pl.pallas_call(
        functools.partial(_in_proj_kernel, tpb=tpb, base=base),
        grid=(t // tm,),
        in_specs=[pl.BlockSpec((tm, D_MODEL), lambda i: (i, 0)),
                  pl.BlockSpec((1, D_MODEL), const),
                  pl.BlockSpec((NMOD, 6 * D_MODEL), const),
                  pl.BlockSpec((None, D_MODEL, QKV_COLS), wmap),
                  pl.BlockSpec((None, D_MODEL, 2 * D_MODEL), wmap),
                  pl.BlockSpec((None, HY_COLS, D_MODEL), wmap)],
        out_specs=[pl.BlockSpec((tm, QKV_COLS), lambda i: (i, 0)),
                   pl.BlockSpec((tm, 2 * D_MODEL), lambda i: (i, 0)),
                   pl.BlockSpec((HY_COLS, tm), lambda i: (0, i))],
        out_shape=[jax.ShapeDtypeStruct((t, QKV_COLS), BF16),
                   jax.ShapeDtypeStruct((t, 2 * D_MODEL), BF16),
                   jax.ShapeDtypeStruct((HY_COLS, t), F32)],
        compiler_params=_cparams("parallel"),
        name="in_proj",
    )(h, g.reshape(1, D_MODEL), mods, w_qkv, w_gate, w_hy_t)


def _mm_hi_kernel(a_ref, w_ref, o_ref):
    o_ref[...] = jnp.dot(a_ref[...], w_ref[...], preferred_element_type=F32, precision=HIGHEST)


def _matmul_hi(a, w, tm):
    m, k = a.shape
    n = w.shape[1]
    return pl.pallas_call(
        _mm_hi_kernel,
        grid=(m // tm,),
        in_specs=[pl.BlockSpec((tm, k), lambda i: (i, 0)),
                  pl.BlockSpec((k, n), lambda i: (0, 0))],
        out_specs=pl.BlockSpec((tm, n), lambda i: (i, 0)),
        out_shape=jax.ShapeDtypeStruct((m, n), F32),
        compiler_params=_cparams("parallel"),
        name="matmul_hi",
    )(a, w)


def _qkv_kernel(u_ref, qg_ref, kvg_ref, wq_ref, wk_ref, wv_ref, ct_ref, st_ref, q_ref, k_ref, v_ref):
    u = u_ref[...].astype(F32)
    nq = _rms(u[:, :Q_LORA], qg_ref[...]).astype(BF16)
    nkv = _rms(u[:, Q_LORA:Q_LORA + KV_LORA], kvg_ref[...]).astype(BF16)
    ka = u[:, Q_LORA + KV_LORA:Q_LORA + KV_LORA + HEAD_PAD]
    kb = u[:, Q_LORA + KV_LORA + HEAD_PAD:]
    ct = ct_ref[...]
    st = st_ref[...]
    kr = ka * ct + kb * st
    qq = jnp.dot(nq, wq_ref[...], preferred_element_type=F32)
    kn = jnp.dot(nkv, wk_ref[...], preferred_element_type=F32)
    nh = MLA_HEADS * HEAD_PAD
    for h in range(MLA_HEADS):
        lo, hi = h * HEAD_PAD, (h + 1) * HEAD_PAD
        qh = qq[:, lo:hi] * ct + qq[:, nh + lo:nh + hi] * st
        q_ref[:, lo:hi] = (qh * Q_SCALE).astype(BF16)
        k_ref[:, lo:hi] = (kn[:, lo:hi] + kr).astype(BF16)
    v_ref[...] = lax.dot_general(wv_ref[...], nkv, (((1,), (1,)), ((), ())),
                                 preferred_element_type=F32).astype(BF16)


def _qkv_up(u, qg, kvg, wq, wk, wvt, ctab, stab, layer, *, rope, tm=512):
    t = u.shape[0]
    ntab = ctab.shape[0] // tm
    tab_map = (lambda i: (i % ntab, 0)) if rope else (lambda i: (0, 0))
    nh = MLA_HEADS * HEAD_PAD
    nv = MLA_HEADS * V_HEAD
    wmap = lambda i: (layer, 0, 0)
    return pl.pallas_call(
        _qkv_kernel,
        grid=(t // tm,),
        in_specs=[pl.BlockSpec((tm, QKV_COLS), lambda i: (i, 0)),
                  pl.BlockSpec((None, 1, Q_LORA), wmap),
                  pl.BlockSpec((None, 1, KV_LORA), wmap),
                  pl.BlockSpec((None, Q_LORA, 2 * nh), wmap),
                  pl.BlockSpec((None, KV_LORA, nh), wmap),
                  pl.BlockSpec((None, nv, KV_LORA), wmap),
                  pl.BlockSpec((tm, HEAD_PAD), tab_map),
                  pl.BlockSpec((tm, HEAD_PAD), tab_map)],
        out_specs=[pl.BlockSpec((tm, nh), lambda i: (i, 0)),
                   pl.BlockSpec((tm, nh), lambda i: (i, 0)),
                   pl.BlockSpec((nv, tm), lambda i: (0, i))],
        out_shape=[jax.ShapeDtypeStruct((t, nh), BF16),
                   jax.ShapeDtypeStruct((t, nh), BF16),
                   jax.ShapeDtypeStruct((nv, t), BF16)],
        compiler_params=_cparams("parallel"),
        name="qkv_up",
    )(u, qg, kvg, wq, wk, wvt, ctab, stab)


def _attn_kernel(*refs, lks, n_sub, tq):
    n_kv = len(lks)
    q_ref = refs[0]
    k_refs = refs[1:1 + n_kv]
    v_refs = refs[1 + n_kv:1 + 2 * n_kv]
    o_ref = refs[1 + 2 * n_kv]
    nt = (((1,), (1,)), ((), ()))
    units = [(j, h) for j in range(n_sub) for h in range(HEADS_PER_STEP)]

    def scores(j, h):
        q = q_ref[j * tq:(j + 1) * tq, h * HEAD_PAD:(h + 1) * HEAD_PAD]
        out = []
        for k_ref, lk in zip(k_refs, lks):
            step = min(lk, KEY_SPLIT)
            for c in range(0, lk, step):
                out.append(lax.dot_general(k_ref[c:c + step, h * HEAD_PAD:(h + 1) * HEAD_PAD], q, nt,
                                           preferred_element_type=F32))
        return out

    s_next = scores(*units[0])
    for idx, (j, h) in enumerate(units):
        s = s_next
        if idx + 1 < len(units):
            s_next = scores(*units[idx + 1])
        m = s[0].max(axis=0, keepdims=True)
        for si in s[1:]:
            m = jnp.maximum(m, si.max(axis=0, keepdims=True))
        p = [jnp.exp2(si - m) for si in s]
        l = p[0].sum(axis=0, keepdims=True)
        for pi in p[1:]:
            l = l + pi.sum(axis=0, keepdims=True)
        o, i = None, 0
        for v_ref, lk in zip(v_refs, lks):
            step = min(lk, KEY_SPLIT)
            for c in range(0, lk, step):
                d = jnp.dot(v_ref[h * V_HEAD:(h + 1) * V_HEAD, c:c + step], p[i].astype(BF16),
                            preferred_element_type=F32)
                o = d if o is None else o + d
                i += 1
        o_ref[h * V_HEAD:(h + 1) * V_HEAD, j * tq:(j + 1) * tq] = (o / l).astype(BF16)


def _attention(q, kv_sets, *, nb, tq=256, n_sub=4):
    t = q.shape[0]
    lq = t // nb
    n_sub = min(n_sub, lq // tq)
    tb = tq * n_sub
    nq = lq // tb
    groups = MLA_HEADS // HEADS_PER_STEP
    qw = HEADS_PER_STEP * HEAD_PAD
    vw = HEADS_PER_STEP * V_HEAD
    in_specs = [pl.BlockSpec((tb, qw), lambda b, g, i: (b * nq + i, g))]
    k_specs, v_specs, ks, vs = [], [], [], []
    for k, vt, lk in kv_sets:
        k_specs.append(pl.BlockSpec((lk, qw), lambda b, g, i: (b, g)))
        v_specs.append(pl.BlockSpec((vw, lk), lambda b, g, i: (g, b)))
        ks.append(k)
        vs.append(vt)
    lks = tuple(lk for _, _, lk in kv_sets)
    return pl.pallas_call(
        functools.partial(_attn_kernel, lks=lks, n_sub=n_sub, tq=tq),
        grid=(nb, groups, nq),
        in_specs=in_specs + k_specs + v_specs,
        out_specs=pl.BlockSpec((vw, tb), lambda b, g, i: (g, b * nq + i)),
        out_shape=jax.ShapeDtypeStruct((MLA_HEADS * V_HEAD, t), BF16),
        compiler_params=_cparams("parallel", "parallel", "arbitrary"),
        name="attention",
    )(q, *ks, *vs)


def _filt_mlp_kernel(z_ref, w1_ref, b1_ref, w2_ref, b2_ref, f0_ref, f1_ref, o_ref):
    h = jnp.dot(w1_ref[...], z_ref[...], preferred_element_type=F32, precision=HIGHEST) + b1_ref[...]
    h = jnp.sin(f0_ref[...] * h)
    h = jnp.dot(w2_ref[...], h, preferred_element_type=F32, precision=HIGHEST) + b2_ref[...]
    o_ref[...] = jnp.sin(f1_ref[...] * h)


def _filt_mlp(zt, w1, b1, w2, b2, freq):
    kz, n = zt.shape
    w1t = jnp.zeros((FILTER_HIDDEN, kz), F32).at[:, :w1.shape[0]].set(w1.T)
    col = lambda a: a.reshape(FILTER_HIDDEN, 1)
    args = (zt, w1t, col(b1), w2.T, col(b2), col(freq[0]), col(freq[1]))
    return pl.pallas_call(
        _filt_mlp_kernel,
        grid=(1,),
        in_specs=[pl.BlockSpec(a.shape, lambda i: (0, 0)) for a in args],
        out_specs=pl.BlockSpec((FILTER_HIDDEN, n), lambda i: (0, 0)),
        out_shape=jax.ShapeDtypeStruct((FILTER_HIDDEN, n), F32),
        compiler_params=_cparams("arbitrary"),
        name="filt_mlp",
    )(*args)


def _filt_taps_kernel(h_ref, w3_ref, d_ref, t_ref, o_ref, *, seq):
    n = h_ref.shape[1]
    pos = lax.broadcasted_iota(jnp.int32, (1, n), 1)
    win = jnp.exp(-t_ref[...] * d_ref[...])
    h = h_ref[...]
    for o in range(HYENA_ORDER):
        fwd = jnp.dot(w3_ref[0, o], h, preferred_element_type=F32, precision=HIGHEST)
        bwd = jnp.dot(w3_ref[1, o], h, preferred_element_type=F32, precision=HIGHEST)
        full = jnp.where(pos < seq, fwd, jnp.where(pos > seq, bwd, 0.0)) * win
        o_ref[o] = full / jnp.sum(jnp.abs(full), axis=1, keepdims=True)


def _filt_taps(h2t, w3, dabs, tvec, seq, cf=64):
    n = h2t.shape[1]
    w3t = w3.T.reshape(2, HYENA_ORDER, HYENA_W, FILTER_HIDDEN)
    return pl.pallas_call(
        functools.partial(_filt_taps_kernel, seq=seq),
        grid=(HYENA_W // cf,),
        in_specs=[pl.BlockSpec((FILTER_HIDDEN, n), lambda c: (0, 0)),
                  pl.BlockSpec((2, HYENA_ORDER, cf, FILTER_HIDDEN), lambda c: (0, 0, c, 0)),
                  pl.BlockSpec((cf, 1), lambda c: (c, 0)),
                  pl.BlockSpec((1, n), lambda c: (0, 0))],
        out_specs=pl.BlockSpec((HYENA_ORDER, cf, n), lambda c: (0, c, 0)),
        out_shape=jax.ShapeDtypeStruct((HYENA_ORDER, HYENA_W, n), F32),
        compiler_params=_cparams("parallel"),
        name="filt_taps",
    )(h2t, w3t, dabs, tvec)


def _filt_spec_kernel(x_ref, f1_ref, tc_ref, ts_ref, f2_ref, o_ref):
    ns = x_ref.shape[0]
    x = jnp.concatenate([x_ref[s:s + 1, :].reshape(FFT_N1, FFT_N2) for s in range(ns)], axis=1).astype(BF16)
    a = jnp.dot(f1_ref[...], x, preferred_element_type=F32)
    tc = tc_ref[...]
    ts = ts_ref[...]
    rows = []
    for s in range(ns):
        ar = a[:FFT_N1, s * FFT_N2:(s + 1) * FFT_N2]
        ai = a[FFT_N1:, s * FFT_N2:(s + 1) * FFT_N2]
        rows.append(jnp.concatenate([ar * tc + ai * ts, ai * tc - ar * ts], axis=1))
    b = jnp.concatenate(rows, axis=0).astype(BF16)
    z = jnp.dot(b, f2_ref[...], preferred_element_type=F32)
    o_ref[...] = z.reshape(ns, FFT_N1, 2 * FFT_N2)


def _filt_spec(hfull, consts, ns=16):
    r = hfull.shape[0]
    f1h, tc, ts, f2 = consts["f1h"], consts["tc"], consts["ts"], consts["f2"]
    const = lambda i: (0, 0)
    return pl.pallas_call(
        _filt_spec_kernel,
        grid=(r // ns,),
        in_specs=[pl.BlockSpec((ns, FFT_N1 * FFT_N2), lambda i: (i, 0)),
                  pl.BlockSpec(f1h.shape, const), pl.BlockSpec(tc.shape, const),
                  pl.BlockSpec(ts.shape, const), pl.BlockSpec(f2.shape, const)],
        out_specs=pl.BlockSpec((ns, FFT_N1, 2 * FFT_N2), lambda i: (i, 0, 0)),
        out_shape=jax.ShapeDtypeStruct((r, FFT_N1, 2 * FFT_N2), F32),
        compiler_params=_cparams("parallel"),
        name="filt_spec",
    )(hfull, f1h, tc, ts, f2)


def _short_conv_tile(x, w, b):
    nrow = x.shape[1]
    lane = lax.broadcasted_iota(jnp.int32, x.shape, 2)
    row = lax.broadcasted_iota(jnp.int32, x.shape, 1)
    r1 = pltpu.roll(x, 1, 2)
    r2 = pltpu.roll(r1, 1, 1)
    prev = jnp.where(lane == 0, jnp.where(row == 0, 0.0, r2), r1)
    l1 = pltpu.roll(x, FFT_N2 - 1, 2)
    l2 = pltpu.roll(l1, nrow - 1, 1)
    nxt = jnp.where(lane == FFT_N2 - 1, jnp.where(row == nrow - 1, 0.0, l2), l1)
    return prev * w[0] + x * w[1] + nxt * w[2] + b


def _long_conv_pair(vr, vi, h, f1, f1i, f2, f2i, tc, ts):
    ns = vr.shape[0]
    half = FFT_N2
    x = jnp.concatenate([jnp.concatenate([vr[s] for s in range(ns)], axis=1),
                         jnp.concatenate([vi[s] for s in range(ns)], axis=1)], axis=0).astype(BF16)
    a = jnp.dot(f1, x, preferred_element_type=F32)
    rows = []
    for s in range(ns):
        ar = a[:FFT_N1, s * half:(s + 1) * half]
        ai = a[FFT_N1:, s * half:(s + 1) * half]
        rows.append(jnp.concatenate([ar * tc + ai * ts, ai * tc - ar * ts], axis=1))
    b = jnp.concatenate(rows, axis=0).astype(BF16)
    z = jnp.dot(b, f2, preferred_element_type=F32).reshape(ns, FFT_N1, 2 * half)
    zr, zi = z[:, :, :half], z[:, :, half:]
    hr, hi = h[:, :, :half], h[:, :, half:]
    w = jnp.concatenate([zr * hr - zi * hi, zr * hi + zi * hr], axis=2)
    w = w.reshape(ns * FFT_N1, 2 * half).astype(BF16)
    c = jnp.dot(w, f2i, preferred_element_type=F32).reshape(ns, FFT_N1, 2 * half)
    cr, ci = c[:, :, :half], c[:, :, half:]
    dr = cr * tc - ci * ts
    di = cr * ts + ci * tc
    d = jnp.concatenate([jnp.concatenate([dr[s] for s in range(ns)], axis=1),
                         jnp.concatenate([di[s] for s in range(ns)], axis=1)], axis=0).astype(BF16)
    y = jnp.dot(f1i, d, preferred_element_type=F32)
    nrow = y.shape[0] // 2
    yr = jnp.stack([y[:nrow, s * half:(s + 1) * half] for s in range(ns)], axis=0)
    yi = jnp.stack([y[nrow:, s * half:(s + 1) * half] for s in range(ns)], axis=0)
    return yr, yi


def _hyena_kernel(x0_ref, x1_ref, v_ref, cw_ref, cb_ref, hb_ref, h_ref,
                  f1_ref, f1i_ref, f2_ref, f2i_ref, tc_ref, ts_ref, o_ref):
    consts = (f1_ref[...], f1i_ref[...], f2_ref[...], f2i_ref[...], tc_ref[...], ts_ref[...])
    g0, g1, v = [], [], []
    for half in range(2):
        g0.append(_short_conv_tile(x0_ref[:, half], cw_ref[0], cb_ref[0]))
        g1.append(_short_conv_tile(x1_ref[:, half], cw_ref[1], cb_ref[1]))
        v.append(_short_conv_tile(v_ref[:, half], cw_ref[2], cb_ref[2]))
    gates = (g0, g1)
    for o in range(HYENA_ORDER):
        yr, yi = _long_conv_pair(v[0], v[1], h_ref[o], *consts)
        v = [gates[o][0] * (yr + hb_ref[o] * v[0]), gates[o][1] * (yi + hb_ref[o] * v[1])]
    o_ref[:, 0] = v[0].astype(BF16)
    o_ref[:, 1] = v[1].astype(BF16)


def _hyena_latent(ut, cw, cb, hb, hspec, consts, *, nb, ct=32):
    seq = ut.shape[1] // nb
    nrow = seq // FFT_N2
    npair = nb // 2
    nct = HYENA_W // ct
    u5 = ut.reshape(3 * HYENA_W, 2, npair, nrow, FFT_N2)
    part = lambda p: pl.BlockSpec((ct, 2, None, nrow, FFT_N2), lambda c, j: (p * nct + c, 0, j, 0, 0))
    const2 = lambda c, j: (0, 0)
    cnames = ("f1", "f1i", "f2", "f2i", "tc", "ts")
    out = pl.pallas_call(
        _hyena_kernel,
        grid=(nct, npair),
        in_specs=[part(0), part(1), part(2),
                  pl.BlockSpec((3, 3, ct, 1, 1), lambda c, j: (0, 0, c, 0, 0)),
                  pl.BlockSpec((3, ct, 1, 1), lambda c, j: (0, c, 0, 0)),
                  pl.BlockSpec((HYENA_ORDER, ct, 1, 1), lambda c, j: (0, c, 0, 0)),
                  pl.BlockSpec((HYENA_ORDER, ct, FFT_N1, 2 * FFT_N2), lambda c, j: (0, c, 0, 0))]
                 + [pl.BlockSpec(consts[k].shape, const2) for k in cnames],
        out_specs=pl.BlockSpec((ct, 2, None, nrow, FFT_N2), lambda c, j: (c, 0, j, 0, 0)),
        out_shape=jax.ShapeDtypeStruct((HYENA_W, 2, npair, nrow, FFT_N2), BF16),
        compiler_params=_cparams("parallel", "arbitrary"),
        name="hyena",
    )(u5, u5, u5, cw, cb, hb, hspec, *[consts[k] for k in cnames])
    return out.reshape(HYENA_W, nb * seq)


def _hyena_ctx_kernel(x0_ref, x1_ref, v_ref, cw_ref, cb_ref, hb_ref, h_ref, cf_ref, ci_ref, o_ref, *, nb):
    seq = x0_ref.shape[1] // nb
    cc = x0_ref.shape[0]
    half = h_ref.shape[2] // 2

    def rows(ref, p):
        x = jnp.concatenate([ref[:, b * seq:(b + 1) * seq] for b in range(nb)], axis=0)
        lane = lax.broadcasted_iota(jnp.int32, x.shape, 1)
        prev = jnp.where(lane == 0, 0.0, pltpu.roll(x, 1, 1))
        nxt = jnp.where(lane == seq - 1, 0.0, pltpu.roll(x, seq - 1, 1))
        w = [jnp.concatenate([cw_ref[p, j]] * nb, axis=0) for j in range(3)]
        return prev * w[0] + x * w[1] + nxt * w[2] + jnp.concatenate([cb_ref[p]] * nb, axis=0)

    g = (rows(x0_ref, 0), rows(x1_ref, 1))
    v = rows(v_ref, 2)
    for o in range(HYENA_ORDER):
        spec = jnp.dot(v.astype(BF16), cf_ref[...], preferred_element_type=F32)
        xr, xi = spec[:, :half], spec[:, half:]
        hh = jnp.concatenate([h_ref[o]] * nb, axis=0)
        hr, hi = hh[:, :half], hh[:, half:]
        y = jnp.concatenate([xr * hr - xi * hi, xr * hi + xi * hr], axis=1).astype(BF16)
        conv = jnp.dot(y, ci_ref[...], preferred_element_type=F32)
        v = g[o] * (conv + jnp.concatenate([hb_ref[o]] * nb, axis=0) * v)
    for b in range(nb):
        o_ref[:, b * seq:(b + 1) * seq] = v[b * cc:(b + 1) * cc].astype(BF16)


def _hyena_ctx(ut, cw, cb, hb, hspec, cfwd, cinv, *, nb, cc=64):
    n = ut.shape[1]
    ncc = HYENA_W // cc
    part = lambda p: pl.BlockSpec((cc, n), lambda c: (p * ncc + c, 0))
    return pl.pallas_call(
        functools.partial(_hyena_ctx_kernel, nb=nb),
        grid=(ncc,),
        in_specs=[part(0), part(1), part(2),
                  pl.BlockSpec((3, 3, cc, 1), lambda c: (0, 0, c, 0)),
                  pl.BlockSpec((3, cc, 1), lambda c: (0, c, 0)),
                  pl.BlockSpec((HYENA_ORDER, cc, 1), lambda c: (0, c, 0)),
                  pl.BlockSpec((HYENA_ORDER, cc, hspec.shape[2]), lambda c: (0, c, 0)),
                  pl.BlockSpec(cfwd.shape, lambda c: (0, 0)),
                  pl.BlockSpec(cinv.shape, lambda c: (0, 0))],
        out_specs=pl.BlockSpec((cc, n), lambda c: (c, 0)),
        out_shape=jax.ShapeDtypeStruct((HYENA_W, n), BF16),
        compiler_params=_cparams("parallel"),
        name="hyena_ctx",
    )(ut, ut, ut, cw, cb, hb, hspec, cfwd, cinv)


FF_CHUNK = 256


def _mix_ffn_kernel(h_ref, ug_ref, gb_ref, yt_ref, att_ref, hp_ref, mp_ref, wo_ref, g2_ref, mods_ref,
                    wi_ref, wd_ref, fg_ref, o_ref, *, tpb, base, final):
    row = base + pl.program_id(0) // tpb
    tn = (((0,), (0,)), ((), ()))
    g = _sigmoid(ug_ref[...].astype(F32) + gb_ref[...])
    yh = lax.dot_general(yt_ref[...], hp_ref[...], tn, preferred_element_type=F32)
    ya = lax.dot_general(att_ref[...], mp_ref[...], tn, preferred_element_type=F32)
    mix = (g[:, :D_MODEL] * yh + g[:, D_MODEL:] * ya).astype(BF16)
    h = h_ref[...] + _mod_row(mods_ref, row, 2) * jnp.dot(mix, wo_ref[...], preferred_element_type=F32)
    f = (_rms(h, g2_ref[...]) * (1.0 + _mod_row(mods_ref, row, 4)) + _mod_row(mods_ref, row, 3)).astype(BF16)
    y = None
    for c in range(0, D_FF, FF_CHUNK):
        gate = jnp.dot(f, wi_ref[:, c:c + FF_CHUNK], preferred_element_type=F32)
        up = jnp.dot(f, wi_ref[:, D_FF + c:D_FF + c + FF_CHUNK], preferred_element_type=F32)
        act = (_silu(gate) * up).astype(BF16)
        d = jnp.dot(act, wd_ref[c:c + FF_CHUNK, :], preferred_element_type=F32)
        y = d if y is None else y + d
    out = h + _mod_row(mods_ref, row, 5) * y
    o_ref[...] = _rms(out, fg_ref[...]) if final else out


def _mix_ffn(h, ug, gate_b, yt, att, hy_proj, mla_proj, w_out, g2, mods, w_in, w_down, final_g, layer,
             *, tpb, base, final, tm=512):
    t = h.shape[0]
    const = lambda i: (0, 0)
    wmap = lambda i: (layer, 0, 0)
    wspec = lambda w: pl.BlockSpec((None,) + w.shape[1:], wmap, pipeline_mode=pl.Buffered(1))
    return pl.pallas_call(
        functools.partial(_mix_ffn_kernel, tpb=tpb, base=base, final=final),
        grid=(t // tm,),
        in_specs=[pl.BlockSpec((tm, D_MODEL), lambda i: (i, 0)),
                  pl.BlockSpec((tm, 2 * D_MODEL), lambda i: (i, 0)),
                  pl.BlockSpec((None, 1, 2 * D_MODEL), wmap),
                  pl.BlockSpec((HYENA_W, tm), lambda i: (0, i)),
                  pl.BlockSpec((MLA_HEADS * V_HEAD, tm), lambda i: (0, i)),
                  wspec(hy_proj), wspec(mla_proj), wspec(w_out),
                  pl.BlockSpec((None, 1, D_MODEL), wmap),
                  pl.BlockSpec((NMOD, 6 * D_MODEL), const),
                  wspec(w_in), wspec(w_down),
                  pl.BlockSpec((1, D_MODEL), const)],
        out_specs=pl.BlockSpec((tm, D_MODEL), lambda i: (i, 0)),
        out_shape=jax.ShapeDtypeStruct((t, D_MODEL), F32),
        compiler_params=_cparams("parallel"),
        name="mix_ffn",
    )(h, ug, gate_b, yt, att, hy_proj, mla_proj, w_out, g2, mods, w_in, w_down, final_g.reshape(1, -1))


def _dft_consts():
    n = FFT_N1 * FFT_N2
    k1 = np.arange(FFT_N1)
    th = 2 * np.pi * np.outer(k1, np.arange(FFT_N1)) / FFT_N1
    c, s = np.cos(th), np.sin(th)
    hn = FFT_N1 // 2
    f1 = np.block([[c[:, :hn], s[:, :hn]], [-s[:, :hn], c[:, :hn]]])
    f1h = np.concatenate([c, -s], axis=0)
    f1i = np.block([[c[:, :hn].T, -s[:, :hn].T], [s[:, :hn].T, c[:, :hn].T]]) / n
    ph = 2 * np.pi * np.outer(k1, np.arange(FFT_N2)) / n
    th2 = 2 * np.pi * np.outer(np.arange(FFT_N2), np.arange(FFT_N2)) / FFT_N2
    c2, s2 = np.cos(th2), np.sin(th2)
    f2 = np.block([[c2, -s2], [s2, c2]])
    f2i = np.block([[c2, s2], [-s2, c2]])
    return {"f1": jnp.asarray(f1, BF16), "f1h": jnp.asarray(f1h, BF16), "f1i": jnp.asarray(f1i, BF16),
            "f2": jnp.asarray(f2, BF16), "f2i": jnp.asarray(f2i, BF16),
            "tc": jnp.asarray(np.cos(ph), F32), "ts": jnp.asarray(np.sin(ph), F32)}


def _dense_dft(seq):
    n = 2 * seq
    th = 2 * np.pi * np.outer(np.arange(n), np.arange(n)) / n
    c, s = np.cos(th), np.sin(th)
    full = np.concatenate([c, -s], axis=1)
    inv = np.concatenate([c[:, :seq], -s[:, :seq]], axis=0) / n
    return jnp.asarray(full, F32), jnp.asarray(full[:seq], BF16), jnp.asarray(inv, BF16)


def _pos_features(seq):
    n = np.arange(2 * seq)
    lag = np.minimum(np.where(n < seq, n, 2 * seq - n), seq - 1)
    t = np.linspace(0.0, 1.0, seq, dtype=np.float32)[lag]
    bands = np.arange(1, N_BANDS + 1, dtype=np.float32)
    ang = (np.float32(2 * math.pi) * bands)[:, None] * t[None, :]
    zt = np.concatenate([t[None, :], np.cos(ang), np.sin(ang), np.zeros((7, 2 * seq), np.float32)], axis=0)
    return jnp.asarray(zt, F32), jnp.asarray(t.reshape(1, -1), F32)


def _rope_tables(seq):
    rows = seq // GRID_W
    row = np.repeat(np.arange(rows, dtype=np.float32), GRID_W)
    col = np.tile(np.arange(GRID_W, dtype=np.float32), rows)
    inv = (np.float32(ROPE_BASE) ** (-np.arange(0, AXIS_ROPE, 2, dtype=np.float32) / AXIS_ROPE)).astype(np.float32)
    ang = np.concatenate([row[:, None] * inv, col[:, None] * inv], axis=-1)
    cos, sin = np.cos(ang), np.sin(ang)
    ones = np.ones((seq, QK_NOPE), np.float32)
    pad = HEAD_PAD - QK_NOPE - QK_ROPE
    ctab = np.concatenate([ones, cos, cos, np.ones((seq, pad), np.float32)], axis=1)
    stab = np.concatenate([0 * ones, sin, sin, np.zeros((seq, pad), np.float32)], axis=1)
    return jnp.asarray(ctab, F32), jnp.asarray(stab, F32)


def _rot_cols(w):
    half = QK_ROPE // 2
    return jnp.concatenate([-w[..., half:], w[..., :half]], axis=-1)


def _pad_heads(nope, rope):
    lead = nope.shape[:-2]
    z = jnp.zeros(lead + (MLA_HEADS, HEAD_PAD - QK_NOPE - QK_ROPE), nope.dtype)
    return jnp.concatenate([nope, rope, z], axis=-1).reshape(lead + (MLA_HEADS * HEAD_PAD,))


def _prep_weights(w_in, w_uq, w_ukv):
    depth, d = w_in.shape[0], w_in.shape[1]
    w_hy_t = jnp.swapaxes(w_in[:, :, :OFF_Q], 1, 2).astype(BF16)
    kr_w = w_in[:, :, OFF_KR:OFF_G]
    zl = jnp.zeros((depth, d, ROPE_LANE), F32)
    zr = jnp.zeros((depth, d, HEAD_PAD - ROPE_LANE - QK_ROPE), F32)
    w_qkv = jnp.concatenate([w_in[:, :, OFF_Q:OFF_KR], zl, kr_w, zr, zl, _rot_cols(kr_w), zr], axis=2).astype(BF16)
    w_gate = w_in[:, :, OFF_G:].astype(BF16)
    uq = w_uq.reshape(depth, Q_LORA, MLA_HEADS, QK_NOPE + QK_ROPE)
    wq = jnp.concatenate([_pad_heads(uq[..., :QK_NOPE], uq[..., QK_NOPE:]),
                          _pad_heads(0 * uq[..., :QK_NOPE], _rot_cols(uq[..., QK_NOPE:]))], axis=2).astype(BF16)
    ukv = w_ukv.reshape(depth, KV_LORA, MLA_HEADS, QK_NOPE + V_HEAD)
    wk = _pad_heads(ukv[..., :QK_NOPE], jnp.zeros((depth, KV_LORA, MLA_HEADS, QK_ROPE), F32)).astype(BF16)
    wvt = jnp.swapaxes(ukv[..., QK_NOPE:].reshape(depth, KV_LORA, MLA_HEADS * V_HEAD), 1, 2).astype(BF16)
    return w_hy_t, w_qkv, w_gate, wq, wk, wvt


def kernel(x, c, ctx, c_ctx, ada_w, ada_b, norm1_g, norm2_g, w_in, gate_b, hy_conv_w, hy_conv_b,
           hy_filt_w1, hy_filt_b1, hy_filt_w2, hy_filt_b2, hy_filt_w3, hy_filt_freq, hy_bias, hy_proj,
           q_norm_g, kv_norm_g, w_uq, w_ukv, mla_proj, w_out, ffn_w_in, ffn_w_out, final_norm_g):
    nb, seq, d = x.shape
    lc = ctx.shape[1]
    depth = ada_w.shape[0]
    assert d == D_MODEL and seq == FFT_N1 * FFT_N2 // 2 and nb % 2 == 0 and nb < NMOD
    tm = 512
    tpb_lat = seq // tm
    tpb_ctx = 1 << 30

    cc = jnp.zeros((NMOD, d), F32).at[:nb].set(c).at[nb].set(c_ctx)
    mods_all = _ada(cc, ada_w, ada_b)

    consts = _dft_consts()
    cfull_c, cfwd_c, cinv_c = _dense_dft(lc)
    ctab, stab = _rope_tables(seq)
    one_tab = jnp.ones((tm, HEAD_PAD), F32)
    zero_tab = jnp.zeros((tm, HEAD_PAD), F32)
    dabs = jnp.asarray(np.abs(np.linspace(HYENA_MIN_DECAY, HYENA_MAX_DECAY, HYENA_W, dtype=np.float32)).reshape(HYENA_W, 1))
    zt_lat, tv_lat = _pos_features(seq)
    zt_ctx, tv_ctx = _pos_features(lc)

    w_hy_t, w_qkv, w_gate, wq, wk, wvt = _prep_weights(w_in, w_uq, w_ukv)
    hp, mp, wo = hy_proj.astype(BF16), mla_proj.astype(BF16), w_out.astype(BF16)
    w_ffi, w_ffo = ffn_w_in.astype(BF16), ffn_w_out.astype(BF16)
    qg = q_norm_g.reshape(depth, 1, Q_LORA)
    kvg = kv_norm_g.reshape(depth, 1, KV_LORA)
    gb = gate_b.reshape(depth, 1, 2 * D_MODEL)
    n1 = norm1_g
    n2 = norm2_g.reshape(depth, 1, D_MODEL)
    cw_all = jnp.transpose(hy_conv_w.reshape(depth, 3, 3, HYENA_W), (0, 2, 1, 3))
    cb_all = hy_conv_b.reshape(depth, 3, HYENA_W)

    h_lat = x.reshape(nb * seq, d)
    h_ctx = ctx.reshape(nb * lc, d)

    for i in range(depth):
        last = i == depth - 1
        mods = mods_all[i]
        cw, cb, hb = cw_all[i], cb_all[i], hy_bias[i]

        filt = (hy_filt_w1[i], hy_filt_b1[i], hy_filt_w2[i], hy_filt_b2[i], hy_filt_freq[i])
        hfull = _filt_taps(_filt_mlp(zt_lat, *filt), hy_filt_w3[i], dabs, tv_lat, seq)
        hspec = _filt_spec(hfull.reshape(HYENA_ORDER * HYENA_W, 2 * seq), consts)
        hspec = hspec.reshape(HYENA_ORDER, HYENA_W, FFT_N1, 2 * FFT_N2)

        uq_l, ug_l, ut_l = _in_proj(h_lat, n1[i], mods, w_qkv, w_gate, w_hy_t, i, tpb=tpb_lat, base=0)
        uq_c, ug_c, ut_c = _in_proj(h_ctx, n1[i], mods, w_qkv, w_gate, w_hy_t, i, tpb=tpb_ctx, base=nb)
        q_l, k_l, v_l = _qkv_up(uq_l, qg, kvg, wq, wk, wvt, ctab, stab, i, rope=True)
        q_c, k_c, v_c = _qkv_up(uq_c, qg, kvg, wq, wk, wvt, one_tab, zero_tab, i, rope=False)
        att_l = _attention(q_l, [(k_l, v_l, seq), (k_c, v_c, lc)], nb=nb)
        yt_l = _hyena_latent(ut_l, cw.reshape(3, 3, HYENA_W, 1, 1), cb.reshape(3, HYENA_W, 1, 1),
                             hb.reshape(HYENA_ORDER, HYENA_W, 1, 1), hspec, consts, nb=nb)
        h_lat = _mix_ffn(h_lat, ug_l, gb, yt_l, att_l, hp, mp, wo, n2, mods, w_ffi, w_ffo, final_norm_g, i,
                         tpb=tpb_lat, base=0, final=last)

        if not last:
            hfull_c = _filt_taps(_filt_mlp(zt_ctx, *filt), hy_filt_w3[i], dabs, tv_ctx, lc)
            hspec_c = _matmul_hi(hfull_c.reshape(HYENA_ORDER * HYENA_W, 2 * lc), cfull_c, tm=256)
            hspec_c = hspec_c.reshape(HYENA_ORDER, HYENA_W, 4 * lc)
            att_c = _attention(q_c, [(k_c, v_c, lc)], nb=nb)
            yt_c = _hyena_ctx(ut_c, cw.reshape(3, 3, HYENA_W, 1), cb.reshape(3, HYENA_W, 1),
                              hb.reshape(HYENA_ORDER, HYENA_W, 1), hspec_c, cfwd_c, cinv_c, nb=nb)
            h_ctx = _mix_ffn(h_ctx, ug_c, gb, yt_c, att_c, hp, mp, wo, n2, mods, w_ffi, w_ffo, final_norm_g, i,
                             tpb=tpb_ctx, base=nb, final=False)

    return h_lat.reshape(nb, seq, d)
```

```python
import functools
import math

import numpy as np
import jax
import jax.numpy as jnp
from jax import lax
from jax.experimental import pallas as pl
from jax.experimental.pallas import tpu as pltpu

F32 = jnp.float32
BF16 = jnp.bfloat16
HIGHEST = lax.Precision.HIGHEST

D_MODEL = 1024
GRID_W = 64
NORM_EPS = 1e-6

HYENA_W = 512
HYENA_ORDER = 2
N_BANDS = 16
FILTER_HIDDEN = 64
HYENA_TARGET = 1e-2
HYENA_MAX_DECAY = math.log(HYENA_TARGET) / 0.3
HYENA_MIN_DECAY = math.log(HYENA_TARGET) / 1.5

MLA_HEADS = 16
QK_NOPE = 64
QK_ROPE = 32
V_HEAD = 64
Q_LORA = 384
KV_LORA = 256
ROPE_BASE = 10000.0
AXIS_ROPE = QK_ROPE // 2
ATTN_SCALE = (QK_NOPE + QK_ROPE) ** -0.5
Q_SCALE = ATTN_SCALE * math.log2(math.e)
D_FF = 2816

HY_COLS = (HYENA_ORDER + 1) * HYENA_W
OFF_Q = HY_COLS
OFF_KV = OFF_Q + Q_LORA
OFF_KR = OFF_KV + KV_LORA
OFF_G = OFF_KR + QK_ROPE

HEAD_PAD = 128
ROPE_LANE = QK_NOPE
QKV_COLS = Q_LORA + KV_LORA + 2 * HEAD_PAD
HEADS_PER_STEP = 4
KEY_SPLIT = 1024
NMOD = 16

FFT_N1 = 64
FFT_N2 = 128
VMEM_LIMIT = 56 * 1024 * 1024


def _cparams(*sem):
    return pltpu.CompilerParams(dimension_semantics=sem, vmem_limit_bytes=VMEM_LIMIT)


def _rms(x, g):
    return x * lax.rsqrt(jnp.mean(x * x, axis=-1, keepdims=True) + NORM_EPS) * g


def _silu(x):
    return x * (1.0 / (1.0 + jnp.exp(-x)))


def _sigmoid(x):
    return 1.0 / (1.0 + jnp.exp(-x))


def _mod_row(mods_ref, row, k):
    return mods_ref[pl.ds(row, 1), k * D_MODEL:(k + 1) * D_MODEL]


def _ada_kernel(c_ref, w_ref, b_ref, o_ref):
    s = _silu(c_ref[...])
    o_ref[...] = jnp.dot(s, w_ref[...], preferred_element_type=F32, precision=HIGHEST) + b_ref[...]


def _ada(cc, ada_w, ada_b):
    depth = ada_w.shape[0]
    tn = D_MODEL
    return pl.pallas_call(
        _ada_kernel,
        grid=(depth, 6 * D_MODEL // tn),
        in_specs=[pl.BlockSpec((NMOD, D_MODEL), lambda l, j: (0, 0)),
                  pl.BlockSpec((None, D_MODEL, tn), lambda l, j: (l, 0, j)),
                  pl.BlockSpec((None, 1, tn), lambda l, j: (l, 0, j))],
        out_specs=pl.BlockSpec((None, NMOD, tn), lambda l, j: (l, 0, j)),
        out_shape=jax.ShapeDtypeStruct((depth, NMOD, 6 * D_MODEL), F32),
        compiler_params=_cparams("arbitrary", "arbitrary"),
        name="ada",
    )(cc, ada_w, ada_b.reshape(depth, 1, 6 * D_MODEL))


def _in_proj_kernel(h_ref, g_ref, mods_ref, wqkv_ref, wg_ref, wht_ref, uq_ref, ug_ref, ut_ref, *, tpb, base):
    row = base + pl.program_id(0) // tpb
    y = _rms(h_ref[...], g_ref[...])
    a = (y * (1.0 + _mod_row(mods_ref, row, 1)) + _mod_row(mods_ref, row, 0)).astype(BF16)
    uq_ref[...] = jnp.dot(a, wqkv_ref[...], preferred_element_type=F32).astype(BF16)
    ug_ref[...] = jnp.dot(a, wg_ref[...], preferred_element_type=F32).astype(BF16)
    ut_ref[...] = lax.dot_general(wht_ref[...], a, (((1,), (1,)), ((), ())), preferred_element_type=F32)


def _in_proj(h, g, mods, w_qkv, w_gate, w_hy_t, layer, *, tpb, base, tm=512):
    t = h.shape[0]
    const = lambda i: (0, 0)
    wmap = lambda i: (layer, 0, 0)
    return pl.pallas_call(
        functools.partial(_in_proj_kernel, tpb=tpb, base=base),
        grid=(t // tm,),
        in_specs=[pl.BlockSpec((tm, D_MODEL), lambda i: (i, 0)),
                  pl.BlockSpec((1, D_MODEL), const),
                  pl.BlockSpec((NMOD, 6 * D_MODEL), const),
                  pl.BlockSpec((None, D_MODEL, QKV_COLS), wmap),
                  pl.BlockSpec((None, D_MODEL, 2 * D_MODEL), wmap),
                  pl.BlockSpec((None, HY_COLS, D_MODEL), wmap)],
        out_specs=[pl.BlockSpec((tm, QKV_COLS), lambda i: (i, 0)),
                   pl.BlockSpec((tm, 2 * D_MODEL), lambda i: (i, 0)),
                   pl.BlockSpec((HY_COLS, tm), lambda i: (0, i))],
        out_shape=[jax.ShapeDtypeStruct((t, QKV_COLS), BF16),
                   jax.ShapeDtypeStruct((t, 2 * D_MODEL), BF16),
                   jax.ShapeDtypeStruct((HY_COLS, t), F32)],
        compiler_params=_cparams("parallel"),
        name="in_proj",
    )(h, g.reshape(1, D_MODEL), mods, w_qkv, w_gate, w_hy_t)


def _mm_hi_kernel(a_ref, w_ref, o_ref):
    o_ref[...] = jnp.dot(a_ref[...], w_ref[...], preferred_element_type=F32, precision=HIGHEST)


def _matmul_hi(a, w, tm):
    m, k = a.shape
    n = w.shape[1]
    return pl.pallas_call(
        _mm_hi_kernel,
        grid=(m // tm,),
        in_specs=[pl.BlockSpec((tm, k), lambda i: (i, 0)),
                  pl.BlockSpec((k, n), lambda i: (0, 0))],
        out_specs=pl.BlockSpec((tm, n), lambda i: (i, 0)),
        out_shape=jax.ShapeDtypeStruct((m, n), F32),
        compiler_params=_cparams("parallel"),
        name="matmul_hi",
    )(a, w)


def _qkv_kernel(u_ref, qg_ref, kvg_ref, wq_ref, wk_ref, wv_ref, ct_ref, st_ref, q_ref, k_ref, v_ref):
    u = u_ref[...].astype(F32)
    nq = _rms(u[:, :Q_LORA], qg_ref[...]).astype(BF16)
    nkv = _rms(u[:, Q_LORA:Q_LORA + KV_LORA], kvg_ref[...]).astype(BF16)
    ka = u[:, Q_LORA + KV_LORA:Q_LORA + KV_LORA + HEAD_PAD]
    kb = u[:, Q_LORA + KV_LORA + HEAD_PAD:]
    ct = ct_ref[...]
    st = st_ref[...]
    kr = ka * ct + kb * st
    qq = jnp.dot(nq, wq_ref[...], preferred_element_type=F32)
    kn = jnp.dot(nkv, wk_ref[...], preferred_element_type=F32)
    lane = lax.broadcasted_iota(jnp.int32, ct.shape, 1)
    half = ROPE_LANE + QK_ROPE // 2
    cq = ct * Q_SCALE
    sa = jnp.where(lane < half, -st, 0.0) * Q_SCALE
    sb = jnp.where(lane >= half, st, 0.0) * Q_SCALE
    for h in range(MLA_HEADS):
        lo, hi = h * HEAD_PAD, (h + 1) * HEAD_PAD
        qh = qq[:, lo:hi]
        down = pltpu.roll(qh, HEAD_PAD - QK_ROPE // 2, 1)
        up = pltpu.roll(qh, QK_ROPE // 2, 1)
        q_ref[:, lo:hi] = (qh * cq + down * sa + up * sb).astype(BF16)
        k_ref[:, lo:hi] = (kn[:, lo:hi] + kr).astype(BF16)
    v_ref[...] = lax.dot_general(wv_ref[...], nkv, (((1,), (1,)), ((), ())),
                                 preferred_element_type=F32).astype(BF16)


def _qkv_up(u, qg, kvg, wq, wk, wvt, ctab, stab, layer, *, rope, tm=512):
    t = u.shape[0]
    ntab = ctab.shape[0] // tm
    tab_map = (lambda i: (i % ntab, 0)) if rope else (lambda i: (0, 0))
    nh = MLA_HEADS * HEAD_PAD
    nv = MLA_HEADS * V_HEAD
    wmap = lambda i: (layer, 0, 0)
    return pl.pallas_call(
        _qkv_kernel,
        grid=(t // tm,),
        in_specs=[pl.BlockSpec((tm, QKV_COLS), lambda i: (i, 0)),
                  pl.BlockSpec((None, 1, Q_LORA), wmap),
                  pl.BlockSpec((None, 1, KV_LORA), wmap),
                  pl.BlockSpec((None, Q_LORA, nh), wmap),
                  pl.BlockSpec((None, KV_LORA, nh), wmap),
                  pl.BlockSpec((None, nv, KV_LORA), wmap),
                  pl.BlockSpec((tm, HEAD_PAD), tab_map),
                  pl.BlockSpec((tm, HEAD_PAD), tab_map)],
        out_specs=[pl.BlockSpec((tm, nh), lambda i: (i, 0)),
                   pl.BlockSpec((tm, nh), lambda i: (i, 0)),
                   pl.BlockSpec((nv, tm), lambda i: (0, i))],
        out_shape=[jax.ShapeDtypeStruct((t, nh), BF16),
                   jax.ShapeDtypeStruct((t, nh), BF16),
                   jax.ShapeDtypeStruct((nv, t), BF16)],
        compiler_params=_cparams("parallel"),
        name="qkv_up",
    )(u, qg, kvg, wq, wk, wvt, ctab, stab)


def _attn_kernel(*refs, lks, n_sub, tq):
    n_kv = len(lks)
    q_ref = refs[0]
    k_refs = refs[1:1 + n_kv]
    v_refs = refs[1 + n_kv:1 + 2 * n_kv]
    o_ref = refs[1 + 2 * n_kv]
    nt = (((1,), (1,)), ((), ()))
    units = [(j, h) for j in range(n_sub) for h in range(HEADS_PER_STEP)]

    def scores(j, h):
        q = q_ref[j * tq:(j + 1) * tq, h * HEAD_PAD:(h + 1) * HEAD_PAD]
        out = []
        for k_ref, lk in zip(k_refs, lks):
            step = min(lk, KEY_SPLIT)
            for c in range(0, lk, step):
                out.append(lax.dot_general(k_ref[c:c + step, h * HEAD_PAD:(h + 1) * HEAD_PAD], q, nt,
                                           preferred_element_type=F32))
        return out

    s_next = scores(*units[0])
    for idx, (j, h) in enumerate(units):
        s = s_next
        if idx + 1 < len(units):
            s_next = scores(*units[idx + 1])
        m = s[0].max(axis=0, keepdims=True)
        for si in s[1:]:
            m = jnp.maximum(m, si.max(axis=0, keepdims=True))
        p = [jnp.exp2(si - m) for si in s]
        l = p[0].sum(axis=0, keepdims=True)
        for pi in p[1:]:
            l = l + pi.sum(axis=0, keepdims=True)
        o, i = None, 0
        for v_ref, lk in zip(v_refs, lks):
            step = min(lk, KEY_SPLIT)
            for c in range(0, lk, step):
                d = jnp.dot(v_ref[h * V_HEAD:(h + 1) * V_HEAD, c:c + step], p[i].astype(BF16),
                            preferred_element_type=F32)
                o = d if o is None else o + d
                i += 1
        o_ref[h * V_HEAD:(h + 1) * V_HEAD, j * tq:(j + 1) * tq] = (o / l).astype(BF16)


def _attention(q, kv_sets, *, nb, tq=256, n_sub=4):
    t = q.shape[0]
    lq = t // nb
    n_sub = min(n_sub, lq // tq)
    tb = tq * n_sub
    nq = lq // tb
    groups = MLA_HEADS // HEADS_PER_STEP
    qw = HEADS_PER_STEP * HEAD_PAD
    vw = HEADS_PER_STEP * V_HEAD
    in_specs = [pl.BlockSpec((tb, qw), lambda b, g, i: (b * nq + i, g))]
    k_specs, v_specs, ks, vs = [], [], [], []
    for k, vt, lk in kv_sets:
        k_specs.append(pl.BlockSpec((lk, qw), lambda b, g, i: (b, g)))
        v_specs.append(pl.BlockSpec((vw, lk), lambda b, g, i: (g, b)))
        ks.append(k)
        vs.append(vt)
    lks = tuple(lk for _, _, lk in kv_sets)
    return pl.pallas_call(
        functools.partial(_attn_kernel, lks=lks, n_sub=n_sub, tq=tq),
        grid=(nb, groups, nq),
        in_specs=in_specs + k_specs + v_specs,
        out_specs=pl.BlockSpec((vw, tb), lambda b, g, i: (g, b * nq + i)),
        out_shape=jax.ShapeDtypeStruct((MLA_HEADS * V_HEAD, t), BF16),
        compiler_params=_cparams("parallel", "parallel", "arbitrary"),
        name="attention",
    )(q, *ks, *vs)


def _filt_mlp_kernel(z_ref, w1_ref, b1_ref, w2_ref, b2_ref, f0_ref, f1_ref, o_ref):
    h = jnp.dot(w1_ref[...], z_ref[...], preferred_element_type=F32, precision=HIGHEST) + b1_ref[...]
    h = jnp.sin(f0_ref[...] * h)
    h = jnp.dot(w2_ref[...], h, preferred_element_type=F32, precision=HIGHEST) + b2_ref[...]
    o_ref[...] = jnp.sin(f1_ref[...] * h)


def _filt_mlp(zt, w1, b1, w2, b2, freq):
    kz, n = zt.shape
    w1t = jnp.zeros((FILTER_HIDDEN, kz), F32).at[:, :w1.shape[0]].set(w1.T)
    col = lambda a: a.reshape(FILTER_HIDDEN, 1)
    args = (zt, w1t, col(b1), w2.T, col(b2), col(freq[0]), col(freq[1]))
    return pl.pallas_call(
        _filt_mlp_kernel,
        grid=(1,),
        in_specs=[pl.BlockSpec(a.shape, lambda i: (0, 0)) for a in args],
        out_specs=pl.BlockSpec((FILTER_HIDDEN, n), lambda i: (0, 0)),
        out_shape=jax.ShapeDtypeStruct((FILTER_HIDDEN, n), F32),
        compiler_params=_cparams("arbitrary"),
        name="filt_mlp",
    )(*args)


def _filt_taps_kernel(h_ref, w3_ref, d_ref, t_ref, o_ref, *, seq):
    n = h_ref.shape[1]
    pos = lax.broadcasted_iota(jnp.int32, (1, n), 1)
    win = jnp.exp(-t_ref[...] * d_ref[...])
    h = h_ref[...]
    for o in range(HYENA_ORDER):
        fwd = jnp.dot(w3_ref[0, o], h, preferred_element_type=F32, precision=HIGHEST)
        bwd = jnp.dot(w3_ref[1, o], h, preferred_element_type=F32, precision=HIGHEST)
        full = jnp.where(pos < seq, fwd, jnp.where(pos > seq, bwd, 0.0)) * win
        o_ref[o] = full / jnp.sum(jnp.abs(full), axis=1, keepdims=True)


def _filt_taps(h2t, w3, dabs, tvec, seq, cf=64):
    n = h2t.shape[1]
    w3t = w3.T.reshape(2, HYENA_ORDER, HYENA_W, FILTER_HIDDEN)
    return pl.pallas_call(
        functools.partial(_filt_taps_kernel, seq=seq),
        grid=(HYENA_W // cf,),
        in_specs=[pl.BlockSpec((FILTER_HIDDEN, n), lambda c: (0, 0)),
                  pl.BlockSpec((2, HYENA_ORDER, cf, FILTER_HIDDEN), lambda c: (0, 0, c, 0)),
                  pl.BlockSpec((cf, 1), lambda c: (c, 0)),
                  pl.BlockSpec((1, n), lambda c: (0, 0))],
        out_specs=pl.BlockSpec((HYENA_ORDER, cf, n), lambda c: (0, c, 0)),
        out_shape=jax.ShapeDtypeStruct((HYENA_ORDER, HYENA_W, n), F32),
        compiler_params=_cparams("parallel"),
        name="filt_taps",
    )(h2t, w3t, dabs, tvec)


def _filt_spec_kernel(x_ref, f1_ref, tc_ref, ts_ref, f2_ref, o_ref):
    ns = x_ref.shape[0]
    x = jnp.concatenate([x_ref[s:s + 1, :].reshape(FFT_N1, FFT_N2) for s in range(ns)], axis=1).astype(BF16)
    a = jnp.dot(f1_ref[...], x, preferred_element_type=F32)
    tc = tc_ref[...]
    ts = ts_ref[...]
    rows = []
    for s in range(ns):
        ar = a[:FFT_N1, s * FFT_N2:(s + 1) * FFT_N2]
        ai = a[FFT_N1:, s * FFT_N2:(s + 1) * FFT_N2]
        rows.append(jnp.concatenate([ar * tc + ai * ts, ai * tc - ar * ts], axis=1))
    b = jnp.concatenate(rows, axis=0).astype(BF16)
    z = jnp.dot(b, f2_ref[...], preferred_element_type=F32)
    o_ref[...] = z.reshape(ns, FFT_N1, 2 * FFT_N2)


def _filt_spec(hfull, consts, ns=16):
    r = hfull.shape[0]
    f1h, tc, ts, f2 = consts["f1h"], consts["tc"], consts["ts"], consts["f2"]
    const = lambda i: (0, 0)
    return pl.pallas_call(
        _filt_spec_kernel,
        grid=(r // ns,),
        in_specs=[pl.BlockSpec((ns, FFT_N1 * FFT_N2), lambda i: (i, 0)),
                  pl.BlockSpec(f1h.shape, const), pl.BlockSpec(tc.shape, const),
                  pl.BlockSpec(ts.shape, const), pl.BlockSpec(f2.shape, const)],
        out_specs=pl.BlockSpec((ns, FFT_N1, 2 * FFT_N2), lambda i: (i, 0, 0)),
        out_shape=jax.ShapeDtypeStruct((r, FFT_N1, 2 * FFT_N2), F32),
        compiler_params=_cparams("parallel"),
        name="filt_spec",
    )(hfull, f1h, tc, ts, f2)


def _short_conv_tile(x, w, b):
    nrow = x.shape[1]
    lane = lax.broadcasted_iota(jnp.int32, x.shape, 2)
    row = lax.broadcasted_iota(jnp.int32, x.shape, 1)
    r1 = pltpu.roll(x, 1, 2)
    r2 = pltpu.roll(r1, 1, 1)
    prev = jnp.where(lane == 0, jnp.where(row == 0, 0.0, r2), r1)
    l1 = pltpu.roll(x, FFT_N2 - 1, 2)
    l2 = pltpu.roll(l1, nrow - 1, 1)
    nxt = jnp.where(lane == FFT_N2 - 1, jnp.where(row == nrow - 1, 0.0, l2), l1)
    return prev * w[0] + x * w[1] + nxt * w[2] + b


def _long_conv_pair(vr, vi, h, f1, f1i, f2, f2i, tc, ts):
    ns = vr.shape[0]
    half = FFT_N2
    x = jnp.concatenate([jnp.concatenate([vr[s] for s in range(ns)], axis=1),
                         jnp.concatenate([vi[s] for s in range(ns)], axis=1)], axis=0).astype(BF16)
    a = jnp.dot(f1, x, preferred_element_type=F32)
    rows = []
    for s in range(ns):
        ar = a[:FFT_N1, s * half:(s + 1) * half]
        ai = a[FFT_N1:, s * half:(s + 1) * half]
        rows.append(jnp.concatenate([ar * tc + ai * ts, ai * tc - ar * ts], axis=1))
    b = jnp.concatenate(rows, axis=0).astype(BF16)
    z = jnp.dot(b, f2, preferred_element_type=F32).reshape(ns, FFT_N1, 2 * half)
    zr, zi = z[:, :, :half], z[:, :, half:]
    hr, hi = h[:, :, :half], h[:, :, half:]
    w = jnp.concatenate([zr * hr - zi * hi, zr * hi + zi * hr], axis=2)
    w = w.reshape(ns * FFT_N1, 2 * half).astype(BF16)
    c = jnp.dot(w, f2i, preferred_element_type=F32).reshape(ns, FFT_N1, 2 * half)
    cr, ci = c[:, :, :half], c[:, :, half:]
    dr = cr * tc - ci * ts
    di = cr * ts + ci * tc
    d = jnp.concatenate([jnp.concatenate([dr[s] for s in range(ns)], axis=1),
                         jnp.concatenate([di[s] for s in range(ns)], axis=1)], axis=0).astype(BF16)
    y = jnp.dot(f1i, d, preferred_element_type=F32)
    nrow = y.shape[0] // 2
    yr = jnp.stack([y[:nrow, s * half:(s + 1) * half] for s in range(ns)], axis=0)
    yi = jnp.stack([y[nrow:, s * half:(s + 1) * half] for s in range(ns)], axis=0)
    return yr, yi


def _hyena_kernel(x0_ref, x1_ref, v_ref, cw_ref, cb_ref, hb_ref, h_ref,
                  f1_ref, f1i_ref, f2_ref, f2i_ref, tc_ref, ts_ref, o_ref):
    consts = (f1_ref[...], f1i_ref[...], f2_ref[...], f2i_ref[...], tc_ref[...], ts_ref[...])
    g0, g1, v = [], [], []
    for half in range(2):
        g0.append(_short_conv_tile(x0_ref[:, half], cw_ref[0], cb_ref[0]))
        g1.append(_short_conv_tile(x1_ref[:, half], cw_ref[1], cb_ref[1]))
        v.append(_short_conv_tile(v_ref[:, half], cw_ref[2], cb_ref[2]))
    gates = (g0, g1)
    for o in range(HYENA_ORDER):
        yr, yi = _long_conv_pair(v[0], v[1], h_ref[o], *consts)
        v = [gates[o][0] * (yr + hb_ref[o] * v[0]), gates[o][1] * (yi + hb_ref[o] * v[1])]
    o_ref[:, 0] = v[0].astype(BF16)
    o_ref[:, 1] = v[1].astype(BF16)


def _hyena_latent(ut, cw, cb, hb, hspec, consts, *, nb, ct=32):
    seq = ut.shape[1] // nb
    nrow = seq // FFT_N2
    npair = nb // 2
    nct = HYENA_W // ct
    u5 = ut.reshape(3 * HYENA_W, 2, npair, nrow, FFT_N2)
    part = lambda p: pl.BlockSpec((ct, 2, None, nrow, FFT_N2), lambda c, j: (p * nct + c, 0, j, 0, 0))
    const2 = lambda c, j: (0, 0)
    cnames = ("f1", "f1i", "f2", "f2i", "tc", "ts")
    out = pl.pallas_call(
        _hyena_kernel,
        grid=(nct, npair),
        in_specs=[part(0), part(1), part(2),
                  pl.BlockSpec((3, 3, ct, 1, 1), lambda c, j: (0, 0, c, 0, 0)),
                  pl.BlockSpec((3, ct, 1, 1), lambda c, j: (0, c, 0, 0)),
                  pl.BlockSpec((HYENA_ORDER, ct, 1, 1), lambda c, j: (0, c, 0, 0)),
                  pl.BlockSpec((HYENA_ORDER, ct, FFT_N1, 2 * FFT_N2), lambda c, j: (0, c, 0, 0))]
                 + [pl.BlockSpec(consts[k].shape, const2) for k in cnames],
        out_specs=pl.BlockSpec((ct, 2, None, nrow, FFT_N2), lambda c, j: (c, 0, j, 0, 0)),
        out_shape=jax.ShapeDtypeStruct((HYENA_W, 2, npair, nrow, FFT_N2), BF16),
        compiler_params=_cparams("parallel", "arbitrary"),
        name="hyena",
    )(u5, u5, u5, cw, cb, hb, hspec, *[consts[k] for k in cnames])
    return out.reshape(HYENA_W, nb * seq)


def _hyena_ctx_kernel(x0_ref, x1_ref, v_ref, cw_ref, cb_ref, hb_ref, h_ref, cf_ref, ci_ref, o_ref, *, nb):
    seq = x0_ref.shape[1] // nb
    cc = x0_ref.shape[0]
    half = h_ref.shape[2] // 2

    def rows(ref, p):
        x = jnp.concatenate([ref[:, b * seq:(b + 1) * seq] for b in range(nb)], axis=0)
        lane = lax.broadcasted_iota(jnp.int32, x.shape, 1)
        prev = jnp.where(lane == 0, 0.0, pltpu.roll(x, 1, 1))
        nxt = jnp.where(lane == seq - 1, 0.0, pltpu.roll(x, seq - 1, 1))
        w = [jnp.concatenate([cw_ref[p, j]] * nb, axis=0) for j in range(3)]
        return prev * w[0] + x * w[1] + nxt * w[2] + jnp.concatenate([cb_ref[p]] * nb, axis=0)

    g = (rows(x0_ref, 0), rows(x1_ref, 1))
    v = rows(v_ref, 2)
    for o in range(HYENA_ORDER):
        spec = jnp.dot(v.astype(BF16), cf_ref[...], preferred_element_type=F32)
        xr, xi = spec[:, :half], spec[:, half:]
        hh = jnp.concatenate([h_ref[o]] * nb, axis=0)
        hr, hi = hh[:, :half], hh[:, half:]
        y = jnp.concatenate([xr * hr - xi * hi, xr * hi + xi * hr], axis=1).astype(BF16)
        conv = jnp.dot(y, ci_ref[...], preferred_element_type=F32)
        v = g[o] * (conv + jnp.concatenate([hb_ref[o]] * nb, axis=0) * v)
    for b in range(nb):
        o_ref[:, b * seq:(b + 1) * seq] = v[b * cc:(b + 1) * cc].astype(BF16)


def _hyena_ctx(ut, cw, cb, hb, hspec, cfwd, cinv, *, nb, cc=64):
    n = ut.shape[1]
    ncc = HYENA_W // cc
    part = lambda p: pl.BlockSpec((cc, n), lambda c: (p * ncc + c, 0))
    return pl.pallas_call(
        functools.partial(_hyena_ctx_kernel, nb=nb),
        grid=(ncc,),
        in_specs=[part(0), part(1), part(2),
                  pl.BlockSpec((3, 3, cc, 1), lambda c: (0, 0, c, 0)),
                  pl.BlockSpec((3, cc, 1), lambda c: (0, c, 0)),
                  pl.BlockSpec((HYENA_ORDER, cc, 1), lambda c: (0, c, 0)),
                  pl.BlockSpec((HYENA_ORDER, cc, hspec.shape[2]), lambda c: (0, c, 0)),
                  pl.BlockSpec(cfwd.shape, lambda c: (0, 0)),
                  pl.BlockSpec(cinv.shape, lambda c: (0, 0))],
        out_specs=pl.BlockSpec((cc, n), lambda c: (c, 0)),
        out_shape=jax.ShapeDtypeStruct((HYENA_W, n), BF16),
        compiler_params=_cparams("parallel"),
        name="hyena_ctx",
    )(ut, ut, ut, cw, cb, hb, hspec, cfwd, cinv)


FF_CHUNK = 256


def _mix_ffn_kernel(h_ref, ug_ref, gb_ref, yt_ref, att_ref, hp_ref, mp_ref, wo_ref, g2_ref, mods_ref,
                    wi_ref, wd_ref, fg_ref, o_ref, *, tpb, base, final):
    row = base + pl.program_id(0) // tpb
    tn = (((0,), (0,)), ((), ()))
    g = _sigmoid(ug_ref[...].astype(F32) + gb_ref[...])
    yh = lax.dot_general(yt_ref[...], hp_ref[...], tn, preferred_element_type=F32)
    ya = lax.dot_general(att_ref[...], mp_ref[...], tn, preferred_element_type=F32)
    mix = (g[:, :D_MODEL] * yh + g[:, D_MODEL:] * ya).astype(BF16)
    h = h_ref[...] + _mod_row(mods_ref, row, 2) * jnp.dot(mix, wo_ref[...], preferred_element_type=F32)
    f = (_rms(h, g2_ref[...]) * (1.0 + _mod_row(mods_ref, row, 4)) + _mod_row(mods_ref, row, 3)).astype(BF16)
    y = None
    for c in range(0, D_FF, FF_CHUNK):
        gate = jnp.dot(f, wi_ref[:, c:c + FF_CHUNK], preferred_element_type=F32)
        up = jnp.dot(f, wi_ref[:, D_FF + c:D_FF + c + FF_CHUNK], preferred_element_type=F32)
        act = (_silu(gate) * up).astype(BF16)
        d = jnp.dot(act, wd_ref[c:c + FF_CHUNK, :], preferred_element_type=F32)
        y = d if y is None else y + d
    out = h + _mod_row(mods_ref, row, 5) * y
    o_ref[...] = _rms(out, fg_ref[...]) if final else out


def _mix_ffn(h, ug, gate_b, yt, att, hy_proj, mla_proj, w_out, g2, mods, w_in, w_down, final_g, layer,
             *, tpb, base, final, tm=512):
    t = h.shape[0]
    const = lambda i: (0, 0)
    wmap = lambda i: (layer, 0, 0)
    wspec = lambda w: pl.BlockSpec((None,) + w.shape[1:], wmap, pipeline_mode=pl.Buffered(1))
    return pl.pallas_call(
        functools.partial(_mix_ffn_kernel, tpb=tpb, base=base, final=final),
        grid=(t // tm,),
        in_specs=[pl.BlockSpec((tm, D_MODEL), lambda i: (i, 0)),
                  pl.BlockSpec((tm, 2 * D_MODEL), lambda i: (i, 0)),
                  pl.BlockSpec((None, 1, 2 * D_MODEL), wmap),
                  pl.BlockSpec((HYENA_W, tm), lambda i: (0, i)),
                  pl.BlockSpec((MLA_HEADS * V_HEAD, tm), lambda i: (0, i)),
                  wspec(hy_proj), wspec(mla_proj), wspec(w_out),
                  pl.BlockSpec((None, 1, D_MODEL), wmap),
                  pl.BlockSpec((NMOD, 6 * D_MODEL), const),
                  wspec(w_in), wspec(w_down),
                  pl.BlockSpec((1, D_MODEL), const)],
        out_specs=pl.BlockSpec((tm, D_MODEL), lambda i: (i, 0)),
        out_shape=jax.ShapeDtypeStruct((t, D_MODEL), F32),
        compiler_params=_cparams("parallel"),
        name="mix_ffn",
    )(h, ug, gate_b, yt, att, hy_proj, mla_proj, w_out, g2, mods, w_in, w_down, final_g.reshape(1, -1))


def _dft_consts():
    n = FFT_N1 * FFT_N2
    k1 = np.arange(FFT_N1)
    th = 2 * np.pi * np.outer(k1, np.arange(FFT_N1)) / FFT_N1
    c, s = np.cos(th), np.sin(th)
    hn = FFT_N1 // 2
    f1 = np.block([[c[:, :hn], s[:, :hn]], [-s[:, :hn], c[:, :hn]]])
    f1h = np.concatenate([c, -s], axis=0)
    f1i = np.block([[c[:, :hn].T, -s[:, :hn].T], [s[:, :hn].T, c[:, :hn].T]]) / n
    ph = 2 * np.pi * np.outer(k1, np.arange(FFT_N2)) / n
    th2 = 2 * np.pi * np.outer(np.arange(FFT_N2), np.arange(FFT_N2)) / FFT_N2
    c2, s2 = np.cos(th2), np.sin(th2)
    f2 = np.block([[c2, -s2], [s2, c2]])
    f2i = np.block([[c2, s2], [-s2, c2]])
    return {"f1": jnp.asarray(f1, BF16), "f1h": jnp.asarray(f1h, BF16), "f1i": jnp.asarray(f1i, BF16),
            "f2": jnp.asarray(f2, BF16), "f2i": jnp.asarray(f2i, BF16),
            "tc": jnp.asarray(np.cos(ph), F32), "ts": jnp.asarray(np.sin(ph), F32)}


def _dense_dft(seq):
    n = 2 * seq
    th = 2 * np.pi * np.outer(np.arange(n), np.arange(n)) / n
    c, s = np.cos(th), np.sin(th)
    full = np.concatenate([c, -s], axis=1)
    inv = np.concatenate([c[:, :seq], -s[:, :seq]], axis=0) / n
    return jnp.asarray(full, F32), jnp.asarray(full[:seq], BF16), jnp.asarray(inv, BF16)


def _pos_features(seq):
    n = np.arange(2 * seq)
    lag = np.minimum(np.where(n < seq, n, 2 * seq - n), seq - 1)
    t = np.linspace(0.0, 1.0, seq, dtype=np.float32)[lag]
    bands = np.arange(1, N_BANDS + 1, dtype=np.float32)
    ang = (np.float32(2 * math.pi) * bands)[:, None] * t[None, :]
    zt = np.concatenate([t[None, :], np.cos(ang), np.sin(ang), np.zeros((7, 2 * seq), np.float32)], axis=0)
    return jnp.asarray(zt, F32), jnp.asarray(t.reshape(1, -1), F32)


def _rope_tables(seq):
    rows = seq // GRID_W
    row = np.repeat(np.arange(rows, dtype=np.float32), GRID_W)
    col = np.tile(np.arange(GRID_W, dtype=np.float32), rows)
    inv = (np.float32(ROPE_BASE) ** (-np.arange(0, AXIS_ROPE, 2, dtype=np.float32) / AXIS_ROPE)).astype(np.float32)
    ang = np.concatenate([row[:, None] * inv, col[:, None] * inv], axis=-1)
    cos, sin = np.cos(ang), np.sin(ang)
    ones = np.ones((seq, QK_NOPE), np.float32)
    pad = HEAD_PAD - QK_NOPE - QK_ROPE
    ctab = np.concatenate([ones, cos, cos, np.ones((seq, pad), np.float32)], axis=1)
    stab = np.concatenate([0 * ones, sin, sin, np.zeros((seq, pad), np.float32)], axis=1)
    return jnp.asarray(ctab, F32), jnp.asarray(stab, F32)


def _rot_cols(w):
    half = QK_ROPE // 2
    return jnp.concatenate([-w[..., half:], w[..., :half]], axis=-1)


def _pad_heads(nope, rope):
    lead = nope.shape[:-2]
    z = jnp.zeros(lead + (MLA_HEADS, HEAD_PAD - QK_NOPE - QK_ROPE), nope.dtype)
    return jnp.concatenate([nope, rope, z], axis=-1).reshape(lead + (MLA_HEADS * HEAD_PAD,))


def _prep_weights(w_in, w_uq, w_ukv):
    depth, d = w_in.shape[0], w_in.shape[1]
    w_hy_t = jnp.swapaxes(w_in[:, :, :OFF_Q], 1, 2).astype(BF16)
    kr_w = w_in[:, :, OFF_KR:OFF_G]
    zl = jnp.zeros((depth, d, ROPE_LANE), F32)
    zr = jnp.zeros((depth, d, HEAD_PAD - ROPE_LANE - QK_ROPE), F32)
    w_qkv = jnp.concatenate([w_in[:, :, OFF_Q:OFF_KR], zl, kr_w, zr, zl, _rot_cols(kr_w), zr], axis=2).astype(BF16)
    w_gate = w_in[:, :, OFF_G:].astype(BF16)
    uq = w_uq.reshape(depth, Q_LORA, MLA_HEADS, QK_NOPE + QK_ROPE)
    wq = _pad_heads(uq[..., :QK_NOPE], uq[..., QK_NOPE:]).astype(BF16)
    ukv = w_ukv.reshape(depth, KV_LORA, MLA_HEADS, QK_NOPE + V_HEAD)
    wk = _pad_heads(ukv[..., :QK_NOPE], jnp.zeros((depth, KV_LORA, MLA_HEADS, QK_ROPE), F32)).astype(BF16)
    wvt = jnp.swapaxes(ukv[..., QK_NOPE:].reshape(depth, KV_LORA, MLA_HEADS * V_HEAD), 1, 2).astype(BF16)
    return w_hy_t, w_qkv, w_gate, wq, wk, wvt


def kernel(x, c, ctx, c_ctx, ada_w, ada_b, norm1_g, norm2_g, w_in, gate_b, hy_conv_w, hy_conv_b,
           hy_filt_w1, hy_filt_b1, hy_filt_w2, hy_filt_b2, hy_filt_w3, hy_filt_freq, hy_bias, hy_proj,
           q_norm_g, kv_norm_g, w_uq, w_ukv, mla_proj, w_out, ffn_w_in, ffn_w_out, final_norm_g):
    nb, seq, d = x.shape
    lc = ctx.shape[1]
    depth = ada_w.shape[0]
    assert d == D_MODEL and seq == FFT_N1 * FFT_N2 // 2 and nb % 2 == 0 and nb < NMOD
    tm = 512
    tpb_lat = seq // tm
    tpb_ctx = 1 << 30

    cc = jnp.zeros((NMOD, d), F32).at[:nb].set(c).at[nb].set(c_ctx)
    mods_all = _ada(cc, ada_w, ada_b)

    consts = _dft_consts()
    cfull_c, cfwd_c, cinv_c = _dense_dft(lc)
    ctab, stab = _rope_tables(seq)
    one_tab = jnp.ones((tm, HEAD_PAD), F32)
    zero_tab = jnp.zeros((tm, HEAD_PAD), F32)
    dabs = jnp.asarray(np.abs(np.linspace(HYENA_MIN_DECAY, HYENA_MAX_DECAY, HYENA_W, dtype=np.float32)).reshape(HYENA_W, 1))
    zt_lat, tv_lat = _pos_features(seq)
    zt_ctx, tv_ctx = _pos_features(lc)

    w_hy_t, w_qkv, w_gate, wq, wk, wvt = _prep_weights(w_in, w_uq, w_ukv)
    hp, mp, wo = hy_proj.astype(BF16), mla_proj.astype(BF16), w_out.astype(BF16)
    w_ffi, w_ffo = ffn_w_in.astype(BF16), ffn_w_out.astype(BF16)
    qg = q_norm_g.reshape(depth, 1, Q_LORA)
    kvg = kv_norm_g.reshape(depth, 1, KV_LORA)
    gb = gate_b.reshape(depth, 1, 2 * D_MODEL)
    n1 = norm1_g
    n2 = norm2_g.reshape(depth, 1, D_MODEL)
    cw_all = jnp.transpose(hy_conv_w.reshape(depth, 3, 3, HYENA_W), (0, 2, 1, 3))
    cb_all = hy_conv_b.reshape(depth, 3, HYENA_W)

    h_lat = x.reshape(nb * seq, d)
    h_ctx = ctx.reshape(nb * lc, d)

    for i in range(depth):
        last = i == depth - 1
        mods = mods_all[i]
        cw, cb, hb = cw_all[i], cb_all[i], hy_bias[i]

        filt = (hy_filt_w1[i], hy_filt_b1[i], hy_filt_w2[i], hy_filt_b2[i], hy_filt_freq[i])
        hfull = _filt_taps(_filt_mlp(zt_lat, *filt), hy_filt_w3[i], dabs, tv_lat, seq)
        hspec = _filt_spec(hfull.reshape(HYENA_ORDER * HYENA_W, 2 * seq), consts)
        hspec = hspec.reshape(HYENA_ORDER, HYENA_W, FFT_N1, 2 * FFT_N2)

        uq_l, ug_l, ut_l = _in_proj(h_lat, n1[i], mods, w_qkv, w_gate, w_hy_t, i, tpb=tpb_lat, base=0)
        uq_c, ug_c, ut_c = _in_proj(h_ctx, n1[i], mods, w_qkv, w_gate, w_hy_t, i, tpb=tpb_ctx, base=nb)
        q_l, k_l, v_l = _qkv_up(uq_l, qg, kvg, wq, wk, wvt, ctab, stab, i, rope=True)
        q_c, k_c, v_c = _qkv_up(uq_c, qg, kvg, wq, wk, wvt, one_tab, zero_tab, i, rope=False)
        att_l = _attention(q_l, [(k_l, v_l, seq), (k_c, v_c, lc)], nb=nb)
        yt_l = _hyena_latent(ut_l, cw.reshape(3, 3, HYENA_W, 1, 1), cb.reshape(3, HYENA_W, 1, 1),
                             hb.reshape(HYENA_ORDER, HYENA_W, 1, 1), hspec, consts, nb=nb)
        h_lat = _mix_ffn(h_lat, ug_l, gb, yt_l, att_l, hp, mp, wo, n2, mods, w_ffi, w_ffo, final_norm_g, i,
                         tpb=tpb_lat, base=0, final=last)

        if not last:
            hfull_c = _filt_taps(_filt_mlp(zt_ctx, *filt), hy_filt_w3[i], dabs, tv_ctx, lc)
            hspec_c = _matmul_hi(hfull_c.reshape(HYENA_ORDER * HYENA_W, 2 * lc), cfull_c, tm=256)
            hspec_c = hspec_c.reshape(HYENA_ORDER, HYENA_W, 4 * lc)
            att_c = _attention(q_c, [(k_c, v_c, lc)], nb=nb)
            yt_c = _hyena_ctx(ut_c, cw.reshape(3, 3, HYENA_W, 1), cb.reshape(3, HYENA_W, 1),
                              hb.reshape(HYENA_ORDER, HYENA_W, 1), hspec_c, cfwd_c, cinv_c, nb=nb)
            h_ctx = _mix_ffn(h_ctx, ug_c, gb, yt_c, att_c, hp, mp, wo, n2, mods, w_ffi, w_ffo, final_norm_g, i,
                             tpb=tpb_ctx, base=nb, final=False)

    return h_lat.reshape(nb, seq, d)
```

```python
import functools
import math

import numpy as np
import jax
import jax.numpy as jnp
from jax import lax
from jax.experimental import pallas as pl
from jax.experimental.pallas import tpu as pltpu

F32 = jnp.float32
BF16 = jnp.bfloat16
HIGHEST = lax.Precision.HIGHEST

D_MODEL = 1024
GRID_W = 64
NORM_EPS = 1e-6

HYENA_W = 512
HYENA_ORDER = 2
N_BANDS = 16
FILTER_HIDDEN = 64
HYENA_TARGET = 1e-2
HYENA_MAX_DECAY = math.log(HYENA_TARGET) / 0.3
HYENA_MIN_DECAY = math.log(HYENA_TARGET) / 1.5

MLA_HEADS = 16
QK_NOPE = 64
QK_ROPE = 32
V_HEAD = 64
Q_LORA = 384
KV_LORA = 256
ROPE_BASE = 10000.0
AXIS_ROPE = QK_ROPE // 2
ATTN_SCALE = (QK_NOPE + QK_ROPE) ** -0.5
Q_SCALE = ATTN_SCALE * math.log2(math.e)
D_FF = 2816

HY_COLS = (HYENA_ORDER + 1) * HYENA_W
OFF_Q = HY_COLS
OFF_KV = OFF_Q + Q_LORA
OFF_KR = OFF_KV + KV_LORA
OFF_G = OFF_KR + QK_ROPE

HEAD_PAD = 128
ROPE_LANE = QK_NOPE
QKV_COLS = Q_LORA + KV_LORA + 2 * HEAD_PAD
HEADS_PER_STEP = 4
KEY_SPLIT = 1024
NMOD = 16

FFT_N1 = 64
FFT_N2 = 128
VMEM_LIMIT = 56 * 1024 * 1024


def _cparams(*sem):
    return pltpu.CompilerParams(dimension_semantics=sem, vmem_limit_bytes=VMEM_LIMIT)


def _rms(x, g):
    return x * lax.rsqrt(jnp.mean(x * x, axis=-1, keepdims=True) + NORM_EPS) * g


def _silu(x):
    return x * (1.0 / (1.0 + jnp.exp(-x)))


def _sigmoid(x):
    return 1.0 / (1.0 + jnp.exp(-x))


def _mod_row(mods_ref, row, k):
    return mods_ref[pl.ds(row, 1), k * D_MODEL:(k + 1) * D_MODEL]


def _ada_kernel(c_ref, w_ref, b_ref, o_ref):
    s = _silu(c_ref[...])
    o_ref[...] = jnp.dot(s, w_ref[...], preferred_element_type=F32, precision=HIGHEST) + b_ref[...]


def _ada(cc, ada_w, ada_b):
    depth = ada_w.shape[0]
    tn = D_MODEL
    return pl.pallas_call(
        _ada_kernel,
        grid=(depth, 6 * D_MODEL // tn),
        in_specs=[pl.BlockSpec((NMOD, D_MODEL), lambda l, j: (0, 0)),
                  pl.BlockSpec((None, D_MODEL, tn), lambda l, j: (l, 0, j)),
                  pl.BlockSpec((None, 1, tn), lambda l, j: (l, 0, j))],
        out_specs=pl.BlockSpec((None, NMOD, tn), lambda l, j: (l, 0, j)),
        out_shape=jax.ShapeDtypeStruct((depth, NMOD, 6 * D_MODEL), F32),
        compiler_params=_cparams("arbitrary", "arbitrary"),
        name="ada",
    )(cc, ada_w, ada_b.reshape(depth, 1, 6 * D_MODEL))


def _in_proj_kernel(h_ref, g_ref, mods_ref, wqkv_ref, wg_ref, wht_ref, uq_ref, ug_ref, ut_ref, *, tpb, base):
    row = base + pl.program_id(0) // tpb
    y = _rms(h_ref[...], g_ref[...])
    a = (y * (1.0 + _mod_row(mods_ref, row, 1)) + _mod_row(mods_ref, row, 0)).astype(BF16)
    uq_ref[...] = jnp.dot(a, wqkv_ref[...], preferred_element_type=F32).astype(BF16)
    ug_ref[...] = jnp.dot(a, wg_ref[...], preferred_element_type=F32).astype(BF16)
    ut_ref[...] = lax.dot_general(wht_ref[...], a, (((1,), (1,)), ((), ())), preferred_element_type=F32)


def _in_proj(h, g, mods, w_qkv, w_gate, w_hy_t, layer, *, tpb, base, tm=512):
    t = h.shape[0]
    const = lambda i: (0, 0)
    wmap = lambda i: (layer, 0, 0)
    return pl.pallas_call(
        functools.partial(_in_proj_kernel, tpb=tpb, base=base),
        grid=(t // tm,),
        in_specs=[pl.BlockSpec((tm, D_MODEL), lambda i: (i, 0)),
                  pl.BlockSpec((1, D_MODEL), const),
                  pl.BlockSpec((NMOD, 6 * D_MODEL), const),
                  pl.BlockSpec((None, D_MODEL, QKV_COLS), wmap),
                  pl.BlockSpec((None, D_MODEL, 2 * D_MODEL), wmap),
                  pl.BlockSpec((None, HY_COLS, D_MODEL), wmap)],
        out_specs=[pl.BlockSpec((tm, QKV_COLS), lambda i: (i, 0)),
                   pl.BlockSpec((tm, 2 * D_MODEL), lambda i: (i, 0)),
                   pl.BlockSpec((HY_COLS, tm), lambda i: (0, i))],
        out_shape=[jax.ShapeDtypeStruct((t, QKV_COLS), BF16),
                   jax.ShapeDtypeStruct((t, 2 * D_MODEL), BF16),
                   jax.ShapeDtypeStruct((HY_COLS, t), F32)],
        compiler_params=_cparams("parallel"),
        name="in_proj",
    )(h, g.reshape(1, D_MODEL), mods, w_qkv, w_gate, w_hy_t)


def _mm_hi_kernel(a_ref, w_ref, o_ref):
    o_ref[...] = jnp.dot(a_ref[...], w_ref[...], preferred_element_type=F32, precision=HIGHEST)


def _matmul_hi(a, w, tm):
    m, k = a.shape
    n = w.shape[1]
    return pl.pallas_call(
        _mm_hi_kernel,
        grid=(m // tm,),
        in_specs=[pl.BlockSpec((tm, k), lambda i: (i, 0)),
                  pl.BlockSpec((k, n), lambda i: (0, 0))],
        out_specs=pl.BlockSpec((tm, n), lambda i: (i, 0)),
        out_shape=jax.ShapeDtypeStruct((m, n), F32),
        compiler_params=_cparams("parallel"),
        name="matmul_hi",
    )(a, w)


def _qkv_kernel(u_ref, qg_ref, kvg_ref, wq_ref, wk_ref, wv_ref, ct_ref, st_ref, q_ref, k_ref, v_ref):
    u = u_ref[...].astype(F32)
    nq = _rms(u[:, :Q_LORA], qg_ref[...]).astype(BF16)
    nkv = _rms(u[:, Q_LORA:Q_LORA + KV_LORA], kvg_ref[...]).astype(BF16)
    ka = u[:, Q_LORA + KV_LORA:Q_LORA + KV_LORA + HEAD_PAD]
    kb = u[:, Q_LORA + KV_LORA + HEAD_PAD:]
    ct = ct_ref[...]
    st = st_ref[...]
    kr = ka * ct + kb * st
    qq = jnp.dot(nq, wq_ref[...], preferred_element_type=F32)
    kn = jnp.dot(nkv, wk_ref[...], preferred_element_type=F32)
    lane = lax.broadcasted_iota(jnp.int32, ct.shape, 1)
    half = ROPE_LANE + QK_ROPE // 2
    cq = ct * Q_SCALE
    sa = jnp.where(lane < half, -st, 0.0) * Q_SCALE
    sb = jnp.where(lane >= half, st, 0.0) * Q_SCALE
    for h in range(MLA_HEADS):
        lo, hi = h * HEAD_PAD, (h + 1) * HEAD_PAD
        qh = qq[:, lo:hi]
        down = pltpu.roll(qh, HEAD_PAD - QK_ROPE // 2, 1)
        up = pltpu.roll(qh, QK_ROPE // 2, 1)
        q_ref[:, lo:hi] = (qh * cq + down * sa + up * sb).astype(BF16)
        k_ref[:, lo:hi] = (kn[:, lo:hi] + kr).astype(BF16)
    v_ref[...] = lax.dot_general(wv_ref[...], nkv, (((1,), (1,)), ((), ())),
                                 preferred_element_type=F32).astype(BF16)


def _qkv_up(u, qg, kvg, wq, wk, wvt, ctab, stab, layer, *, rope, tm=512):
    t = u.shape[0]
    ntab = ctab.shape[0] // tm
    tab_map = (lambda i: (i % ntab, 0)) if rope else (lambda i: (0, 0))
    nh = MLA_HEADS * HEAD_PAD
    nv = MLA_HEADS * V_HEAD
    wmap = lambda i: (layer, 0, 0)
    return pl.pallas_call(
        _qkv_kernel,
        grid=(t // tm,),
        in_specs=[pl.BlockSpec((tm, QKV_COLS), lambda i: (i, 0)),
                  pl.BlockSpec((None, 1, Q_LORA), wmap),
                  pl.BlockSpec((None, 1, KV_LORA), wmap),
                  pl.BlockSpec((None, Q_LORA, nh), wmap),
                  pl.BlockSpec((None, KV_LORA, nh), wmap),
                  pl.BlockSpec((None, nv, KV_LORA), wmap),
                  pl.BlockSpec((tm, HEAD_PAD), tab_map),
                  pl.BlockSpec((tm, HEAD_PAD), tab_map)],
        out_specs=[pl.BlockSpec((tm, nh), lambda i: (i, 0)),
                   pl.BlockSpec((tm, nh), lambda i: (i, 0)),
                   pl.BlockSpec((nv, tm), lambda i: (0, i))],
        out_shape=[jax.ShapeDtypeStruct((t, nh), BF16),
                   jax.ShapeDtypeStruct((t, nh), BF16),
                   jax.ShapeDtypeStruct((nv, t), BF16)],
        compiler_params=_cparams("parallel"),
        name="qkv_up",
    )(u, qg, kvg, wq, wk, wvt, ctab, stab)


def _attn_kernel(*refs, lks, n_sub, tq, hps):
    n_kv = len(lks)
    q_ref = refs[0]
    k_refs = refs[1:1 + n_kv]
    v_refs = refs[1 + n_kv:1 + 2 * n_kv]
    o_ref = refs[1 + 2 * n_kv]
    nt = (((1,), (1,)), ((), ()))
    units = [(j, h) for j in range(n_sub) for h in range(hps)]

    def scores(j, h):
        q = q_ref[j * tq:(j + 1) * tq, h * HEAD_PAD:(h + 1) * HEAD_PAD]
        out = []
        for k_ref, lk in zip(k_refs, lks):
            step = min(lk, KEY_SPLIT)
            for c in range(0, lk, step):
                out.append(lax.dot_general(k_ref[c:c + step, h * HEAD_PAD:(h + 1) * HEAD_PAD], q, nt,
                                           preferred_element_type=F32))
        return out

    s_next = scores(*units[0])
    for idx, (j, h) in enumerate(units):
        s = s_next
        if idx + 1 < len(units):
            s_next = scores(*units[idx + 1])
        m = s[0].max(axis=0, keepdims=True)
        for si in s[1:]:
            m = jnp.maximum(m, si.max(axis=0, keepdims=True))
        p = [jnp.exp2(si - m) for si in s]
        l = p[0].sum(axis=0, keepdims=True)
        for pi in p[1:]:
            l = l + pi.sum(axis=0, keepdims=True)
        o, i = None, 0
        for v_ref, lk in zip(v_refs, lks):
            step = min(lk, KEY_SPLIT)
            for c in range(0, lk, step):
                d = jnp.dot(v_ref[h * V_HEAD:(h + 1) * V_HEAD, c:c + step], p[i].astype(BF16),
                            preferred_element_type=F32)
                o = d if o is None else o + d
                i += 1
        o_ref[h * V_HEAD:(h + 1) * V_HEAD, j * tq:(j + 1) * tq] = (o / l).astype(BF16)


def _attention(q, kv_sets, *, nb, tq=256, n_sub=4, hps=HEADS_PER_STEP):
    t = q.shape[0]
    lq = t // nb
    n_sub = min(n_sub, lq // tq)
    tb = tq * n_sub
    nq = lq // tb
    groups = MLA_HEADS // hps
    qw = hps * HEAD_PAD
    vw = hps * V_HEAD
    in_specs = [pl.BlockSpec((tb, qw), lambda b, g, i: (b * nq + i, g))]
    k_specs, v_specs, ks, vs = [], [], [], []
    for k, vt, lk in kv_sets:
        k_specs.append(pl.BlockSpec((lk, qw), lambda b, g, i: (b, g)))
        v_specs.append(pl.BlockSpec((vw, lk), lambda b, g, i: (g, b)))
        ks.append(k)
        vs.append(vt)
    lks = tuple(lk for _, _, lk in kv_sets)
    return pl.pallas_call(
        functools.partial(_attn_kernel, lks=lks, n_sub=n_sub, tq=tq, hps=hps),
        grid=(nb, groups, nq),
        in_specs=in_specs + k_specs + v_specs,
        out_specs=pl.BlockSpec((vw, tb), lambda b, g, i: (g, b * nq + i)),
        out_shape=jax.ShapeDtypeStruct((MLA_HEADS * V_HEAD, t), BF16),
        compiler_params=_cparams("parallel", "parallel", "arbitrary"),
        name="attention",
    )(q, *ks, *vs)


def _filt_mlp_kernel(z_ref, w1_ref, b1_ref, w2_ref, b2_ref, f0_ref, f1_ref, o_ref):
    h = jnp.dot(w1_ref[...], z_ref[...], preferred_element_type=F32, precision=HIGHEST) + b1_ref[...]
    h = jnp.sin(f0_ref[...] * h)
    h = jnp.dot(w2_ref[...], h, preferred_element_type=F32, precision=HIGHEST) + b2_ref[...]
    o_ref[...] = jnp.sin(f1_ref[...] * h)


def _filt_mlp(zt, w1, b1, w2, b2, freq):
    kz, n = zt.shape
    w1t = jnp.zeros((FILTER_HIDDEN, kz), F32).at[:, :w1.shape[0]].set(w1.T)
    col = lambda a: a.reshape(FILTER_HIDDEN, 1)
    args = (zt, w1t, col(b1), w2.T, col(b2), col(freq[0]), col(freq[1]))
    return pl.pallas_call(
        _filt_mlp_kernel,
        grid=(1,),
        in_specs=[pl.BlockSpec(a.shape, lambda i: (0, 0)) for a in args],
        out_specs=pl.BlockSpec((FILTER_HIDDEN, n), lambda i: (0, 0)),
        out_shape=jax.ShapeDtypeStruct((FILTER_HIDDEN, n), F32),
        compiler_params=_cparams("arbitrary"),
        name="filt_mlp",
    )(*args)


def _filt_taps_kernel(h_ref, w3_ref, d_ref, t_ref, o_ref, *, seq):
    n = h_ref.shape[1]
    pos = lax.broadcasted_iota(jnp.int32, (1, n), 1)
    win = jnp.exp(-t_ref[...] * d_ref[...])
    h = h_ref[...].astype(BF16)
    for o in range(HYENA_ORDER):
        fwd = jnp.dot(w3_ref[0, o].astype(BF16), h, preferred_element_type=F32)
        bwd = jnp.dot(w3_ref[1, o].astype(BF16), h, preferred_element_type=F32)
        full = jnp.where(pos < seq, fwd, jnp.where(pos > seq, bwd, 0.0)) * win
        o_ref[o] = full / jnp.sum(jnp.abs(full), axis=1, keepdims=True)


def _filt_taps(h2t, w3, dabs, tvec, seq, cf=64):
    n = h2t.shape[1]
    w3t = w3.T.reshape(2, HYENA_ORDER, HYENA_W, FILTER_HIDDEN)
    return pl.pallas_call(
        functools.partial(_filt_taps_kernel, seq=seq),
        grid=(HYENA_W // cf,),
        in_specs=[pl.BlockSpec((FILTER_HIDDEN, n), lambda c: (0, 0)),
                  pl.BlockSpec((2, HYENA_ORDER, cf, FILTER_HIDDEN), lambda c: (0, 0, c, 0)),
                  pl.BlockSpec((cf, 1), lambda c: (c, 0)),
                  pl.BlockSpec((1, n), lambda c: (0, 0))],
        out_specs=pl.BlockSpec((HYENA_ORDER, cf, n), lambda c: (0, c, 0)),
        out_shape=jax.ShapeDtypeStruct((HYENA_ORDER, HYENA_W, n), F32),
        compiler_params=_cparams("parallel"),
        name="filt_taps",
    )(h2t, w3t, dabs, tvec)


def _filt_spec_kernel(x_ref, f1_ref, tc_ref, ts_ref, f2_ref, o_ref):
    ns = x_ref.shape[0]
    x = jnp.concatenate([x_ref[s:s + 1, :].reshape(FFT_N1, FFT_N2) for s in range(ns)], axis=1).astype(BF16)
    a = jnp.dot(f1_ref[...], x, preferred_element_type=F32)
    tc = tc_ref[...]
    ts = ts_ref[...]
    rows = []
    for s in range(ns):
        ar = a[:FFT_N1, s * FFT_N2:(s + 1) * FFT_N2]
        ai = a[FFT_N1:, s * FFT_N2:(s + 1) * FFT_N2]
        rows.append(jnp.concatenate([ar * tc + ai * ts, ai * tc - ar * ts], axis=1))
    b = jnp.concatenate(rows, axis=0).astype(BF16)
    z = jnp.dot(b, f2_ref[...], preferred_element_type=F32)
    o_ref[...] = z.reshape(ns, FFT_N1, 2 * FFT_N2)


def _filt_spec(hfull, consts, ns=16):
    r = hfull.shape[0]
    f1h, tc, ts, f2 = consts["f1h"], consts["tc"], consts["ts"], consts["f2"]
    const = lambda i: (0, 0)
    return pl.pallas_call(
        _filt_spec_kernel,
        grid=(r // ns,),
        in_specs=[pl.BlockSpec((ns, FFT_N1 * FFT_N2), lambda i: (i, 0)),
                  pl.BlockSpec(f1h.shape, const), pl.BlockSpec(tc.shape, const),
                  pl.BlockSpec(ts.shape, const), pl.BlockSpec(f2.shape, const)],
        out_specs=pl.BlockSpec((ns, FFT_N1, 2 * FFT_N2), lambda i: (i, 0, 0)),
        out_shape=jax.ShapeDtypeStruct((r, FFT_N1, 2 * FFT_N2), F32),
        compiler_params=_cparams("parallel"),
        name="filt_spec",
    )(hfull, f1h, tc, ts, f2)


def _short_conv_tile(x, w, b):
    nrow = x.shape[1]
    lane = lax.broadcasted_iota(jnp.int32, x.shape, 2)
    row = lax.broadcasted_iota(jnp.int32, x.shape, 1)
    r1 = pltpu.roll(x, 1, 2)
    r2 = pltpu.roll(r1, 1, 1)
    prev = jnp.where(lane == 0, jnp.where(row == 0, 0.0, r2), r1)
    l1 = pltpu.roll(x, FFT_N2 - 1, 2)
    l2 = pltpu.roll(l1, nrow - 1, 1)
    nxt = jnp.where(lane == FFT_N2 - 1, jnp.where(row == nrow - 1, 0.0, l2), l1)
    return prev * w[0] + x * w[1] + nxt * w[2] + b


def _long_conv_pair(vr, vi, h, f1, f1i, f2, f2i, tc, ts):
    ns = vr.shape[0]
    half = FFT_N2
    x = jnp.concatenate([jnp.concatenate([vr[s] for s in range(ns)], axis=1),
                         jnp.concatenate([vi[s] for s in range(ns)], axis=1)], axis=0).astype(BF16)
    a = jnp.dot(f1, x, preferred_element_type=F32)
    rows = []
    for s in range(ns):
        ar = a[:FFT_N1, s * half:(s + 1) * half]
        ai = a[FFT_N1:, s * half:(s + 1) * half]
        rows.append(jnp.concatenate([ar * tc + ai * ts, ai * tc - ar * ts], axis=1))
    b = jnp.concatenate(rows, axis=0).astype(BF16)
    z = jnp.dot(b, f2, preferred_element_type=F32).reshape(ns, FFT_N1, 2 * half)
    zr, zi = z[:, :, :half], z[:, :, half:]
    hr, hi = h[:, :, :half], h[:, :, half:]
    w = jnp.concatenate([zr * hr - zi * hi, zr * hi + zi * hr], axis=2)
    w = w.reshape(ns * FFT_N1, 2 * half).astype(BF16)
    c = jnp.dot(w, f2i, preferred_element_type=F32).reshape(ns, FFT_N1, 2 * half)
    cr, ci = c[:, :, :half], c[:, :, half:]
    dr = cr * tc - ci * ts
    di = cr * ts + ci * tc
    d = jnp.concatenate([jnp.concatenate([dr[s] for s in range(ns)], axis=1),
                         jnp.concatenate([di[s] for s in range(ns)], axis=1)], axis=0).astype(BF16)
    y = jnp.dot(f1i, d, preferred_element_type=F32)
    nrow = y.shape[0] // 2
    yr = jnp.stack([y[:nrow, s * half:(s + 1) * half] for s in range(ns)], axis=0)
    yi = jnp.stack([y[nrow:, s * half:(s + 1) * half] for s in range(ns)], axis=0)
    return yr, yi


def _hyena_kernel(x0_ref, x1_ref, v_ref, cw_ref, cb_ref, hb_ref, h_ref,
                  f1_ref, f1i_ref, f2_ref, f2i_ref, tc_ref, ts_ref, o_ref):
    consts = (f1_ref[...], f1i_ref[...], f2_ref[...], f2i_ref[...], tc_ref[...], ts_ref[...])
    g0, g1, v = [], [], []
    for half in range(2):
        g0.append(_short_conv_tile(x0_ref[:, half], cw_ref[0], cb_ref[0]))
        g1.append(_short_conv_tile(x1_ref[:, half], cw_ref[1], cb_ref[1]))
        v.append(_short_conv_tile(v_ref[:, half], cw_ref[2], cb_ref[2]))
    gates = (g0, g1)
    for o in range(HYENA_ORDER):
        yr, yi = _long_conv_pair(v[0], v[1], h_ref[o], *consts)
        v = [gates[o][0] * (yr + hb_ref[o] * v[0]), gates[o][1] * (yi + hb_ref[o] * v[1])]
    o_ref[:, 0] = v[0].astype(BF16)
    o_ref[:, 1] = v[1].astype(BF16)


def _hyena_latent(ut, cw, cb, hb, hspec, consts, *, nb, ct=32):
    seq = ut.shape[1] // nb
    nrow = seq // FFT_N2
    npair = nb // 2
    nct = HYENA_W // ct
    u5 = ut.reshape(3 * HYENA_W, 2, npair, nrow, FFT_N2)
    part = lambda p: pl.BlockSpec((ct, 2, None, nrow, FFT_N2), lambda c, j: (p * nct + c, 0, j, 0, 0))
    const2 = lambda c, j: (0, 0)
    cnames = ("f1", "f1i", "f2", "f2i", "tc", "ts")
    out = pl.pallas_call(
        _hyena_kernel,
        grid=(nct, npair),
        in_specs=[part(0), part(1), part(2),
                  pl.BlockSpec((3, 3, ct, 1, 1), lambda c, j: (0, 0, c, 0, 0)),
                  pl.BlockSpec((3, ct, 1, 1), lambda c, j: (0, c, 0, 0)),
                  pl.BlockSpec((HYENA_ORDER, ct, 1, 1), lambda c, j: (0, c, 0, 0)),
                  pl.BlockSpec((HYENA_ORDER, ct, FFT_N1, 2 * FFT_N2), lambda c, j: (0, c, 0, 0))]
                 + [pl.BlockSpec(consts[k].shape, const2) for k in cnames],
        out_specs=pl.BlockSpec((ct, 2, None, nrow, FFT_N2), lambda c, j: (c, 0, j, 0, 0)),
        out_shape=jax.ShapeDtypeStruct((HYENA_W, 2, npair, nrow, FFT_N2), BF16),
        compiler_params=_cparams("parallel", "arbitrary"),
        name="hyena",
    )(u5, u5, u5, cw, cb, hb, hspec, *[consts[k] for k in cnames])
    return out.reshape(HYENA_W, nb * seq)


def _hyena_ctx_kernel(x0_ref, x1_ref, v_ref, cw_ref, cb_ref, hb_ref, h_ref, cf_ref, ci_ref, o_ref, *, nb):
    seq = x0_ref.shape[1] // nb
    cc = x0_ref.shape[0]
    half = h_ref.shape[2] // 2

    def rows(ref, p):
        x = jnp.concatenate([ref[:, b * seq:(b + 1) * seq] for b in range(nb)], axis=0)
        lane = lax.broadcasted_iota(jnp.int32, x.shape, 1)
        prev = jnp.where(lane == 0, 0.0, pltpu.roll(x, 1, 1))
        nxt = jnp.where(lane == seq - 1, 0.0, pltpu.roll(x, seq - 1, 1))
        w = [jnp.concatenate([cw_ref[p, j]] * nb, axis=0) for j in range(3)]
        return prev * w[0] + x * w[1] + nxt * w[2] + jnp.concatenate([cb_ref[p]] * nb, axis=0)

    g = (rows(x0_ref, 0), rows(x1_ref, 1))
    v = rows(v_ref, 2)
    for o in range(HYENA_ORDER):
        spec = jnp.dot(v.astype(BF16), cf_ref[...], preferred_element_type=F32)
        xr, xi = spec[:, :half], spec[:, half:]
        hh = jnp.concatenate([h_ref[o]] * nb, axis=0)
        hr, hi = hh[:, :half], hh[:, half:]
        y = jnp.concatenate([xr * hr - xi * hi, xr * hi + xi * hr], axis=1).astype(BF16)
        conv = jnp.dot(y, ci_ref[...], preferred_element_type=F32)
        v = g[o] * (conv + jnp.concatenate([hb_ref[o]] * nb, axis=0) * v)
    for b in range(nb):
        o_ref[:, b * seq:(b + 1) * seq] = v[b * cc:(b + 1) * cc].astype(BF16)


def _hyena_ctx(ut, cw, cb, hb, hspec, cfwd, cinv, *, nb, cc=64):
    n = ut.shape[1]
    ncc = HYENA_W // cc
    part = lambda p: pl.BlockSpec((cc, n), lambda c: (p * ncc + c, 0))
    return pl.pallas_call(
        functools.partial(_hyena_ctx_kernel, nb=nb),
        grid=(ncc,),
        in_specs=[part(0), part(1), part(2),
                  pl.BlockSpec((3, 3, cc, 1), lambda c: (0, 0, c, 0)),
                  pl.BlockSpec((3, cc, 1), lambda c: (0, c, 0)),
                  pl.BlockSpec((HYENA_ORDER, cc, 1), lambda c: (0, c, 0)),
                  pl.BlockSpec((HYENA_ORDER, cc, hspec.shape[2]), lambda c: (0, c, 0)),
                  pl.BlockSpec(cfwd.shape, lambda c: (0, 0)),
                  pl.BlockSpec(cinv.shape, lambda c: (0, 0))],
        out_specs=pl.BlockSpec((cc, n), lambda c: (c, 0)),
        out_shape=jax.ShapeDtypeStruct((HYENA_W, n), BF16),
        compiler_params=_cparams("parallel"),
        name="hyena_ctx",
    )(ut, ut, ut, cw, cb, hb, hspec, cfwd, cinv)


FF_CHUNK = 256


def _mix_ffn_kernel(h_ref, ug_ref, gb_ref, yt_ref, att_ref, hp_ref, mp_ref, wo_ref, g2_ref, mods_ref,
                    wi_ref, wd_ref, fg_ref, o_ref, *, tpb, base, final):
    row = base + pl.program_id(0) // tpb
    tn = (((0,), (0,)), ((), ()))
    g = _sigmoid(ug_ref[...].astype(F32) + gb_ref[...])
    yh = lax.dot_general(yt_ref[...], hp_ref[...], tn, preferred_element_type=F32)
    ya = lax.dot_general(att_ref[...], mp_ref[...], tn, preferred_element_type=F32)
    mix = (g[:, :D_MODEL] * yh + g[:, D_MODEL:] * ya).astype(BF16)
    h = h_ref[...] + _mod_row(mods_ref, row, 2) * jnp.dot(mix, wo_ref[...], preferred_element_type=F32)
    f = (_rms(h, g2_ref[...]) * (1.0 + _mod_row(mods_ref, row, 4)) + _mod_row(mods_ref, row, 3)).astype(BF16)
    y = None
    for c in range(0, D_FF, FF_CHUNK):
        gate = jnp.dot(f, wi_ref[:, c:c + FF_CHUNK], preferred_element_type=F32)
        up = jnp.dot(f, wi_ref[:, D_FF + c:D_FF + c + FF_CHUNK], preferred_element_type=F32)
        act = (_silu(gate) * up).astype(BF16)
        d = jnp.dot(act, wd_ref[c:c + FF_CHUNK, :], preferred_element_type=F32)
        y = d if y is None else y + d
    out = h + _mod_row(mods_ref, row, 5) * y
    o_ref[...] = _rms(out, fg_ref[...]) if final else out


def _mix_ffn(h, ug, gate_b, yt, att, hy_proj, mla_proj, w_out, g2, mods, w_in, w_down, final_g, layer,
             *, tpb, base, final, tm=512):
    t = h.shape[0]
    const = lambda i: (0, 0)
    wmap = lambda i: (layer, 0, 0)
    wspec = lambda w: pl.BlockSpec((None,) + w.shape[1:], wmap, pipeline_mode=pl.Buffered(1))
    return pl.pallas_call(
        functools.partial(_mix_ffn_kernel, tpb=tpb, base=base, final=final),
        grid=(t // tm,),
        in_specs=[pl.BlockSpec((tm, D_MODEL), lambda i: (i, 0)),
                  pl.BlockSpec((tm, 2 * D_MODEL), lambda i: (i, 0)),
                  pl.BlockSpec((None, 1, 2 * D_MODEL), wmap),
                  pl.BlockSpec((HYENA_W, tm), lambda i: (0, i)),
                  pl.BlockSpec((MLA_HEADS * V_HEAD, tm), lambda i: (0, i)),
                  wspec(hy_proj), wspec(mla_proj), wspec(w_out),
                  pl.BlockSpec((None, 1, D_MODEL), wmap),
                  pl.BlockSpec((NMOD, 6 * D_MODEL), const),
                  wspec(w_in), wspec(w_down),
                  pl.BlockSpec((1, D_MODEL), const)],
        out_specs=pl.BlockSpec((tm, D_MODEL), lambda i: (i, 0)),
        out_shape=jax.ShapeDtypeStruct((t, D_MODEL), F32),
        compiler_params=_cparams("parallel"),
        name="mix_ffn",
    )(h, ug, gate_b, yt, att, hy_proj, mla_proj, w_out, g2, mods, w_in, w_down, final_g.reshape(1, -1))


def _dft_consts():
    n = FFT_N1 * FFT_N2
    k1 = np.arange(FFT_N1)
    th = 2 * np.pi * np.outer(k1, np.arange(FFT_N1)) / FFT_N1
    c, s = np.cos(th), np.sin(th)
    hn = FFT_N1 // 2
    f1 = np.block([[c[:, :hn], s[:, :hn]], [-s[:, :hn], c[:, :hn]]])
    f1h = np.concatenate([c, -s], axis=0)
    f1i = np.block([[c[:, :hn].T, -s[:, :hn].T], [s[:, :hn].T, c[:, :hn].T]]) / n
    ph = 2 * np.pi * np.outer(k1, np.arange(FFT_N2)) / n
    th2 = 2 * np.pi * np.outer(np.arange(FFT_N2), np.arange(FFT_N2)) / FFT_N2
    c2, s2 = np.cos(th2), np.sin(th2)
    f2 = np.block([[c2, -s2], [s2, c2]])
    f2i = np.block([[c2, s2], [-s2, c2]])
    return {"f1": jnp.asarray(f1, BF16), "f1h": jnp.asarray(f1h, BF16), "f1i": jnp.asarray(f1i, BF16),
            "f2": jnp.asarray(f2, BF16), "f2i": jnp.asarray(f2i, BF16),
            "tc": jnp.asarray(np.cos(ph), F32), "ts": jnp.asarray(np.sin(ph), F32)}


def _dense_dft(seq):
    n = 2 * seq
    th = 2 * np.pi * np.outer(np.arange(n), np.arange(n)) / n
    c, s = np.cos(th), np.sin(th)
    full = np.concatenate([c, -s], axis=1)
    inv = np.concatenate([c[:, :seq], -s[:, :seq]], axis=0) / n
    return jnp.asarray(full, F32), jnp.asarray(full[:seq], BF16), jnp.asarray(inv, BF16)


def _pos_features(seq):
    n = np.arange(2 * seq)
    lag = np.minimum(np.where(n < seq, n, 2 * seq - n), seq - 1)
    t = np.linspace(0.0, 1.0, seq, dtype=np.float32)[lag]
    bands = np.arange(1, N_BANDS + 1, dtype=np.float32)
    ang = (np.float32(2 * math.pi) * bands)[:, None] * t[None, :]
    zt = np.concatenate([t[None, :], np.cos(ang), np.sin(ang), np.zeros((7, 2 * seq), np.float32)], axis=0)
    return jnp.asarray(zt, F32), jnp.asarray(t.reshape(1, -1), F32)


def _rope_tables(seq):
    rows = seq // GRID_W
    row = np.repeat(np.arange(rows, dtype=np.float32), GRID_W)
    col = np.tile(np.arange(GRID_W, dtype=np.float32), rows)
    inv = (np.float32(ROPE_BASE) ** (-np.arange(0, AXIS_ROPE, 2, dtype=np.float32) / AXIS_ROPE)).astype(np.float32)
    ang = np.concatenate([row[:, None] * inv, col[:, None] * inv], axis=-1)
    cos, sin = np.cos(ang), np.sin(ang)
    ones = np.ones((seq, QK_NOPE), np.float32)
    pad = HEAD_PAD - QK_NOPE - QK_ROPE
    ctab = np.concatenate([ones, cos, cos, np.ones((seq, pad), np.float32)], axis=1)
    stab = np.concatenate([0 * ones, sin, sin, np.zeros((seq, pad), np.float32)], axis=1)
    return jnp.asarray(ctab, F32), jnp.asarray(stab, F32)


def _rot_cols(w):
    half = QK_ROPE // 2
    return jnp.concatenate([-w[..., half:], w[..., :half]], axis=-1)


def _pad_heads(nope, rope):
    lead = nope.shape[:-2]
    z = jnp.zeros(lead + (MLA_HEADS, HEAD_PAD - QK_NOPE - QK_ROPE), nope.dtype)
    return jnp.concatenate([nope, rope, z], axis=-1).reshape(lead + (MLA_HEADS * HEAD_PAD,))


def _prep_weights(w_in, w_uq, w_ukv):
    depth, d = w_in.shape[0], w_in.shape[1]
    w_hy_t = jnp.swapaxes(w_in[:, :, :OFF_Q], 1, 2).astype(BF16)
    kr_w = w_in[:, :, OFF_KR:OFF_G]
    zl = jnp.zeros((depth, d, ROPE_LANE), F32)
    zr = jnp.zeros((depth, d, HEAD_PAD - ROPE_LANE - QK_ROPE), F32)
    w_qkv = jnp.concatenate([w_in[:, :, OFF_Q:OFF_KR], zl, kr_w, zr, zl, _rot_cols(kr_w), zr], axis=2).astype(BF16)
    w_gate = w_in[:, :, OFF_G:].astype(BF16)
    uq = w_uq.reshape(depth, Q_LORA, MLA_HEADS, QK_NOPE + QK_ROPE)
    wq = _pad_heads(uq[..., :QK_NOPE], uq[..., QK_NOPE:]).astype(BF16)
    ukv = w_ukv.reshape(depth, KV_LORA, MLA_HEADS, QK_NOPE + V_HEAD)
    wk = _pad_heads(ukv[..., :QK_NOPE], jnp.zeros((depth, KV_LORA, MLA_HEADS, QK_ROPE), F32)).astype(BF16)
    wvt = jnp.swapaxes(ukv[..., QK_NOPE:].reshape(depth, KV_LORA, MLA_HEADS * V_HEAD), 1, 2).astype(BF16)
    return w_hy_t, w_qkv, w_gate, wq, wk, wvt


def kernel(x, c, ctx, c_ctx, ada_w, ada_b, norm1_g, norm2_g, w_in, gate_b, hy_conv_w, hy_conv_b,
           hy_filt_w1, hy_filt_b1, hy_filt_w2, hy_filt_b2, hy_filt_w3, hy_filt_freq, hy_bias, hy_proj,
           q_norm_g, kv_norm_g, w_uq, w_ukv, mla_proj, w_out, ffn_w_in, ffn_w_out, final_norm_g):
    nb, seq, d = x.shape
    lc = ctx.shape[1]
    depth = ada_w.shape[0]
    assert d == D_MODEL and seq == FFT_N1 * FFT_N2 // 2 and nb % 2 == 0 and nb < NMOD
    tm = 512
    tpb_lat = seq // tm
    tpb_ctx = 1 << 30

    cc = jnp.zeros((NMOD, d), F32).at[:nb].set(c).at[nb].set(c_ctx)
    mods_all = _ada(cc, ada_w, ada_b)

    consts = _dft_consts()
    cfull_c, cfwd_c, cinv_c = _dense_dft(lc)
    ctab, stab = _rope_tables(seq)
    one_tab = jnp.ones((tm, HEAD_PAD), F32)
    zero_tab = jnp.zeros((tm, HEAD_PAD), F32)
    dabs = jnp.asarray(np.abs(np.linspace(HYENA_MIN_DECAY, HYENA_MAX_DECAY, HYENA_W, dtype=np.float32)).reshape(HYENA_W, 1))
    zt_lat, tv_lat = _pos_features(seq)
    zt_ctx, tv_ctx = _pos_features(lc)

    w_hy_t, w_qkv, w_gate, wq, wk, wvt = _prep_weights(w_in, w_uq, w_ukv)
    hp, mp, wo = hy_proj.astype(BF16), mla_proj.astype(BF16), w_out.astype(BF16)
    w_ffi, w_ffo = ffn_w_in.astype(BF16), ffn_w_out.astype(BF16)
    qg = q_norm_g.reshape(depth, 1, Q_LORA)
    kvg = kv_norm_g.reshape(depth, 1, KV_LORA)
    gb = gate_b.reshape(depth, 1, 2 * D_MODEL)
    n1 = norm1_g
    n2 = norm2_g.reshape(depth, 1, D_MODEL)
    cw_all = jnp.transpose(hy_conv_w.reshape(depth, 3, 3, HYENA_W), (0, 2, 1, 3))
    cb_all = hy_conv_b.reshape(depth, 3, HYENA_W)

    h_lat = x.reshape(nb * seq, d)
    h_ctx = ctx.reshape(nb * lc, d)

    for i in range(depth):
        last = i == depth - 1
        mods = mods_all[i]
        cw, cb, hb = cw_all[i], cb_all[i], hy_bias[i]

        filt = (hy_filt_w1[i], hy_filt_b1[i], hy_filt_w2[i], hy_filt_b2[i], hy_filt_freq[i])
        hfull = _filt_taps(_filt_mlp(zt_lat, *filt), hy_filt_w3[i], dabs, tv_lat, seq)
        hspec = _filt_spec(hfull.reshape(HYENA_ORDER * HYENA_W, 2 * seq), consts)
        hspec = hspec.reshape(HYENA_ORDER, HYENA_W, FFT_N1, 2 * FFT_N2)

        uq_l, ug_l, ut_l = _in_proj(h_lat, n1[i], mods, w_qkv, w_gate, w_hy_t, i, tpb=tpb_lat, base=0)
        uq_c, ug_c, ut_c = _in_proj(h_ctx, n1[i], mods, w_qkv, w_gate, w_hy_t, i, tpb=tpb_ctx, base=nb)
        q_l, k_l, v_l = _qkv_up(uq_l, qg, kvg, wq, wk, wvt, ctab, stab, i, rope=True)
        q_c, k_c, v_c = _qkv_up(uq_c, qg, kvg, wq, wk, wvt, one_tab, zero_tab, i, rope=False)
        att_l = _attention(q_l, [(k_l, v_l, seq), (k_c, v_c, lc)], nb=nb)
        yt_l = _hyena_latent(ut_l, cw.reshape(3, 3, HYENA_W, 1, 1), cb.reshape(3, HYENA_W, 1, 1),
                             hb.reshape(HYENA_ORDER, HYENA_W, 1, 1), hspec, consts, nb=nb)
        h_lat = _mix_ffn(h_lat, ug_l, gb, yt_l, att_l, hp, mp, wo, n2, mods, w_ffi, w_ffo, final_norm_g, i,
                         tpb=tpb_lat, base=0, final=last)

        if not last:
            hfull_c = _filt_taps(_filt_mlp(zt_ctx, *filt), hy_filt_w3[i], dabs, tv_ctx, lc)
            hspec_c = _matmul_hi(hfull_c.reshape(HYENA_ORDER * HYENA_W, 2 * lc), cfull_c, tm=256)
            hspec_c = hspec_c.reshape(HYENA_ORDER, HYENA_W, 4 * lc)
            att_c = _attention(q_c, [(k_c, v_c, lc)], nb=nb, hps=MLA_HEADS)
            yt_c = _hyena_ctx(ut_c, cw.reshape(3, 3, HYENA_W, 1), cb.reshape(3, HYENA_W, 1),
                              hb.reshape(HYENA_ORDER, HYENA_W, 1), hspec_c, cfwd_c, cinv_c, nb=nb)
            h_ctx = _mix_ffn(h_ctx, ug_c, gb, yt_c, att_c, hp, mp, wo, n2, mods, w_ffi, w_ffo, final_norm_g, i,
                             tpb=tpb_ctx, base=nb, final=False)

    return h_lat.reshape(nb, seq, d)
```

```python
import functools
import math

import numpy as np
import jax
import jax.numpy as jnp
from jax import lax
from jax.experimental import pallas as pl
from jax.experimental.pallas import tpu as pltpu

F32 = jnp.float32
BF16 = jnp.bfloat16
HIGHEST = lax.Precision.HIGHEST

D_MODEL = 1024
GRID_W = 64
NORM_EPS = 1e-6

HYENA_W = 512
HYENA_ORDER = 2
N_BANDS = 16
FILTER_HIDDEN = 64
HYENA_TARGET = 1e-2
HYENA_MAX_DECAY = math.log(HYENA_TARGET) / 0.3
HYENA_MIN_DECAY = math.log(HYENA_TARGET) / 1.5

MLA_HEADS = 16
QK_NOPE = 64
QK_ROPE = 32
V_HEAD = 64
Q_LORA = 384
KV_LORA = 256
ROPE_BASE = 10000.0
AXIS_ROPE = QK_ROPE // 2
ATTN_SCALE = (QK_NOPE + QK_ROPE) ** -0.5
Q_SCALE = ATTN_SCALE * math.log2(math.e)
D_FF = 2816

HY_COLS = (HYENA_ORDER + 1) * HYENA_W
OFF_Q = HY_COLS
OFF_KV = OFF_Q + Q_LORA
OFF_KR = OFF_KV + KV_LORA
OFF_G = OFF_KR + QK_ROPE

HEAD_PAD = 128
ROPE_LANE = QK_NOPE
QKV_COLS = Q_LORA + KV_LORA + 2 * HEAD_PAD
HEADS_PER_STEP = 4
KEY_SPLIT = 1024
NMOD = 16

FFT_N1 = 64
FFT_N2 = 128
VMEM_LIMIT = 56 * 1024 * 1024


def _cparams(*sem):
    return pltpu.CompilerParams(dimension_semantics=sem, vmem_limit_bytes=VMEM_LIMIT)


def _rms(x, g):
    return x * lax.rsqrt(jnp.mean(x * x, axis=-1, keepdims=True) + NORM_EPS) * g


def _silu(x):
    return x * (1.0 / (1.0 + jnp.exp(-x)))


def _sigmoid(x):
    return 1.0 / (1.0 + jnp.exp(-x))


def _mod_row(mods_ref, row, k):
    return mods_ref[pl.ds(row, 1), k * D_MODEL:(k + 1) * D_MODEL]


def _ada_kernel(c_ref, w_ref, b_ref, o_ref):
    s = _silu(c_ref[...])
    o_ref[...] = jnp.dot(s, w_ref[...], preferred_element_type=F32, precision=HIGHEST) + b_ref[...]


def _ada(cc, ada_w, ada_b):
    depth = ada_w.shape[0]
    tn = D_MODEL
    return pl.pallas_call(
        _ada_kernel,
        grid=(depth, 6 * D_MODEL // tn),
        in_specs=[pl.BlockSpec((NMOD, D_MODEL), lambda l, j: (0, 0)),
                  pl.BlockSpec((None, D_MODEL, tn), lambda l, j: (l, 0, j)),
                  pl.BlockSpec((None, 1, tn), lambda l, j: (l, 0, j))],
        out_specs=pl.BlockSpec((None, NMOD, tn), lambda l, j: (l, 0, j)),
        out_shape=jax.ShapeDtypeStruct((depth, NMOD, 6 * D_MODEL), F32),
        compiler_params=_cparams("arbitrary", "arbitrary"),
        name="ada",
    )(cc, ada_w, ada_b.reshape(depth, 1, 6 * D_MODEL))


def _in_proj_kernel(h_ref, g_ref, mods_ref, wqkv_ref, wg_ref, wht_ref, uq_ref, ug_ref, ut_ref, *, tpb, base):
    row = base + pl.program_id(0) // tpb
    y = _rms(h_ref[...], g_ref[...])
    a = (y * (1.0 + _mod_row(mods_ref, row, 1)) + _mod_row(mods_ref, row, 0)).astype(BF16)
    uq_ref[...] = jnp.dot(a, wqkv_ref[...], preferred_element_type=F32).astype(BF16)
    ug_ref[...] = jnp.dot(a, wg_ref[...], preferred_element_type=F32).astype(BF16)
    ut_ref[...] = lax.dot_general(wht_ref[...], a, (((1,), (1,)), ((), ())), preferred_element_type=F32)


def _in_proj(h, g, mods, w_qkv, w_gate, w_hy_t, layer, *, tpb, base, tm=512):
    t = h.shape[0]
    const = lambda i: (0, 0)
    wmap = lambda i: (layer, 0, 0)
    return pl.pallas_call(
        functools.partial(_in_proj_kernel, tpb=tpb, base=base),
        grid=(t // tm,),
        in_specs=[pl.BlockSpec((tm, D_MODEL), lambda i: (i, 0)),
                  pl.BlockSpec((1, D_MODEL), const),
                  pl.BlockSpec((NMOD, 6 * D_MODEL), const),
                  pl.BlockSpec((None, D_MODEL, QKV_COLS), wmap),
                  pl.BlockSpec((None, D_MODEL, 2 * D_MODEL), wmap),
                  pl.BlockSpec((None, HY_COLS, D_MODEL), wmap)],
        out_specs=[pl.BlockSpec((tm, QKV_COLS), lambda i: (i, 0)),
                   pl.BlockSpec((tm, 2 * D_MODEL), lambda i: (i, 0)),
                   pl.BlockSpec((HY_COLS, tm), lambda i: (0, i))],
        out_shape=[jax.ShapeDtypeStruct((t, QKV_COLS), BF16),
                   jax.ShapeDtypeStruct((t, 2 * D_MODEL), BF16),
                   jax.ShapeDtypeStruct((HY_COLS, t), F32)],
        compiler_params=_cparams("parallel"),
        name="in_proj",
    )(h, g.reshape(1, D_MODEL), mods, w_qkv, w_gate, w_hy_t)


def _mm_hi_kernel(a_ref, w_ref, o_ref):
    o_ref[...] = jnp.dot(a_ref[...], w_ref[...], preferred_element_type=F32, precision=HIGHEST)


def _matmul_hi(a, w, tm):
    m, k = a.shape
    n = w.shape[1]
    return pl.pallas_call(
        _mm_hi_kernel,
        grid=(m // tm,),
        in_specs=[pl.BlockSpec((tm, k), lambda i: (i, 0)),
                  pl.BlockSpec((k, n), lambda i: (0, 0))],
        out_specs=pl.BlockSpec((tm, n), lambda i: (i, 0)),
        out_shape=jax.ShapeDtypeStruct((m, n), F32),
        compiler_params=_cparams("parallel"),
        name="matmul_hi",
    )(a, w)


def _qkv_kernel(u_ref, qg_ref, kvg_ref, wq_ref, wk_ref, wv_ref, ct_ref, st_ref, q_ref, k_ref, v_ref):
    u = u_ref[...].astype(F32)
    nq = _rms(u[:, :Q_LORA], qg_ref[...]).astype(BF16)
    nkv = _rms(u[:, Q_LORA:Q_LORA + KV_LORA], kvg_ref[...]).astype(BF16)
    ka = u[:, Q_LORA + KV_LORA:Q_LORA + KV_LORA + HEAD_PAD]
    kb = u[:, Q_LORA + KV_LORA + HEAD_PAD:]
    ct = ct_ref[...]
    st = st_ref[...]
    kr = ka * ct + kb * st
    qq = jnp.dot(nq, wq_ref[...], preferred_element_type=F32)
    kn = jnp.dot(nkv, wk_ref[...], preferred_element_type=F32)
    lane = lax.broadcasted_iota(jnp.int32, ct.shape, 1)
    half = ROPE_LANE + QK_ROPE // 2
    cq = ct * Q_SCALE
    sa = jnp.where(lane < half, -st, 0.0) * Q_SCALE
    sb = jnp.where(lane >= half, st, 0.0) * Q_SCALE
    for h in range(MLA_HEADS):
        lo, hi = h * HEAD_PAD, (h + 1) * HEAD_PAD
        qh = qq[:, lo:hi]
        down = pltpu.roll(qh, HEAD_PAD - QK_ROPE // 2, 1)
        up = pltpu.roll(qh, QK_ROPE // 2, 1)
        q_ref[:, lo:hi] = (qh * cq + down * sa + up * sb).astype(BF16)
        k_ref[:, lo:hi] = (kn[:, lo:hi] + kr).astype(BF16)
    v_ref[...] = lax.dot_general(wv_ref[...], nkv, (((1,), (1,)), ((), ())),
                                 preferred_element_type=F32).astype(BF16)


def _qkv_up(u, qg, kvg, wq, wk, wvt, ctab, stab, layer, *, rope, tm=512):
    t = u.shape[0]
    ntab = ctab.shape[0] // tm
    tab_map = (lambda i: (i % ntab, 0)) if rope else (lambda i: (0, 0))
    nh = MLA_HEADS * HEAD_PAD
    nv = MLA_HEADS * V_HEAD
    wmap = lambda i: (layer, 0, 0)
    return pl.pallas_call(
        _qkv_kernel,
        grid=(t // tm,),
        in_specs=[pl.BlockSpec((tm, QKV_COLS), lambda i: (i, 0)),
                  pl.BlockSpec((None, 1, Q_LORA), wmap),
                  pl.BlockSpec((None, 1, KV_LORA), wmap),
                  pl.BlockSpec((None, Q_LORA, nh), wmap),
                  pl.BlockSpec((None, KV_LORA, nh), wmap),
                  pl.BlockSpec((None, nv, KV_LORA), wmap),
                  pl.BlockSpec((tm, HEAD_PAD), tab_map),
                  pl.BlockSpec((tm, HEAD_PAD), tab_map)],
        out_specs=[pl.BlockSpec((tm, nh), lambda i: (i, 0)),
                   pl.BlockSpec((tm, nh), lambda i: (i, 0)),
                   pl.BlockSpec((nv, tm), lambda i: (0, i))],
        out_shape=[jax.ShapeDtypeStruct((t, nh), BF16),
                   jax.ShapeDtypeStruct((t, nh), BF16),
                   jax.ShapeDtypeStruct((nv, t), BF16)],
        compiler_params=_cparams("parallel"),
        name="qkv_up",
    )(u, qg, kvg, wq, wk, wvt, ctab, stab)


def _attn_kernel(*refs, lks, n_sub, tq, hps):
    n_kv = len(lks)
    q_ref = refs[0]
    k_refs = refs[1:1 + n_kv]
    v_refs = refs[1 + n_kv:1 + 2 * n_kv]
    o_ref = refs[1 + 2 * n_kv]
    nt = (((1,), (1,)), ((), ()))
    units = [(j, h) for j in range(n_sub) for h in range(hps)]

    def scores(j, h):
        q = q_ref[j * tq:(j + 1) * tq, h * HEAD_PAD:(h + 1) * HEAD_PAD]
        out = []
        for k_ref, lk in zip(k_refs, lks):
            step = min(lk, KEY_SPLIT)
            for c in range(0, lk, step):
                out.append(lax.dot_general(k_ref[c:c + step, h * HEAD_PAD:(h + 1) * HEAD_PAD], q, nt,
                                           preferred_element_type=F32))
        return out

    s_next = scores(*units[0])
    for idx, (j, h) in enumerate(units):
        s = s_next
        if idx + 1 < len(units):
            s_next = scores(*units[idx + 1])
        m = s[0].max(axis=0, keepdims=True)
        for si in s[1:]:
            m = jnp.maximum(m, si.max(axis=0, keepdims=True))
        p = [jnp.exp2(si - m) for si in s]
        l = p[0].sum(axis=0, keepdims=True)
        for pi in p[1:]:
            l = l + pi.sum(axis=0, keepdims=True)
        o, i = None, 0
        for v_ref, lk in zip(v_refs, lks):
            step = min(lk, KEY_SPLIT)
            for c in range(0, lk, step):
                d = jnp.dot(v_ref[h * V_HEAD:(h + 1) * V_HEAD, c:c + step], p[i].astype(BF16),
                            preferred_element_type=F32)
                o = d if o is None else o + d
                i += 1
        o_ref[h * V_HEAD:(h + 1) * V_HEAD, j * tq:(j + 1) * tq] = (o / l).astype(BF16)


def _attention(q, kv_sets, *, nb, tq=256, n_sub=8, hps=HEADS_PER_STEP):
    t = q.shape[0]
    lq = t // nb
    n_sub = min(n_sub, lq // tq)
    tb = tq * n_sub
    nq = lq // tb
    groups = MLA_HEADS // hps
    qw = hps * HEAD_PAD
    vw = hps * V_HEAD
    in_specs = [pl.BlockSpec((tb, qw), lambda b, g, i: (b * nq + i, g))]
    k_specs, v_specs, ks, vs = [], [], [], []
    for k, vt, lk in kv_sets:
        k_specs.append(pl.BlockSpec((lk, qw), lambda b, g, i: (b, g)))
        v_specs.append(pl.BlockSpec((vw, lk), lambda b, g, i: (g, b)))
        ks.append(k)
        vs.append(vt)
    lks = tuple(lk for _, _, lk in kv_sets)
    return pl.pallas_call(
        functools.partial(_attn_kernel, lks=lks, n_sub=n_sub, tq=tq, hps=hps),
        grid=(nb, groups, nq),
        in_specs=in_specs + k_specs + v_specs,
        out_specs=pl.BlockSpec((vw, tb), lambda b, g, i: (g, b * nq + i)),
        out_shape=jax.ShapeDtypeStruct((MLA_HEADS * V_HEAD, t), BF16),
        compiler_params=_cparams("parallel", "parallel", "arbitrary"),
        name="attention",
    )(q, *ks, *vs)


def _filt_mlp_kernel(z_ref, w1_ref, b1_ref, w2_ref, b2_ref, f0_ref, f1_ref, o_ref):
    h = jnp.dot(w1_ref[...], z_ref[...], preferred_element_type=F32, precision=HIGHEST) + b1_ref[...]
    h = jnp.sin(f0_ref[...] * h)
    h = jnp.dot(w2_ref[...], h, preferred_element_type=F32, precision=HIGHEST) + b2_ref[...]
    o_ref[...] = jnp.sin(f1_ref[...] * h)


def _filt_mlp(zt, w1, b1, w2, b2, freq):
    kz, n = zt.shape
    w1t = jnp.zeros((FILTER_HIDDEN, kz), F32).at[:, :w1.shape[0]].set(w1.T)
    col = lambda a: a.reshape(FILTER_HIDDEN, 1)
    args = (zt, w1t, col(b1), w2.T, col(b2), col(freq[0]), col(freq[1]))
    return pl.pallas_call(
        _filt_mlp_kernel,
        grid=(1,),
        in_specs=[pl.BlockSpec(a.shape, lambda i: (0, 0)) for a in args],
        out_specs=pl.BlockSpec((FILTER_HIDDEN, n), lambda i: (0, 0)),
        out_shape=jax.ShapeDtypeStruct((FILTER_HIDDEN, n), F32),
        compiler_params=_cparams("arbitrary"),
        name="filt_mlp",
    )(*args)


def _filt_taps_kernel(h_ref, w3_ref, d_ref, t_ref, o_ref, *, seq):
    n = h_ref.shape[1]
    pos = lax.broadcasted_iota(jnp.int32, (1, n), 1)
    win = jnp.exp(-t_ref[...] * d_ref[...])
    h = h_ref[...].astype(BF16)
    for o in range(HYENA_ORDER):
        fwd = jnp.dot(w3_ref[0, o].astype(BF16), h, preferred_element_type=F32)
        bwd = jnp.dot(w3_ref[1, o].astype(BF16), h, preferred_element_type=F32)
        full = jnp.where(pos < seq, fwd, jnp.where(pos > seq, bwd, 0.0)) * win
        o_ref[o] = full / jnp.sum(jnp.abs(full), axis=1, keepdims=True)


def _filt_taps(h2t, w3, dabs, tvec, seq, cf=64):
    n = h2t.shape[1]
    w3t = w3.T.reshape(2, HYENA_ORDER, HYENA_W, FILTER_HIDDEN)
    return pl.pallas_call(
        functools.partial(_filt_taps_kernel, seq=seq),
        grid=(HYENA_W // cf,),
        in_specs=[pl.BlockSpec((FILTER_HIDDEN, n), lambda c: (0, 0)),
                  pl.BlockSpec((2, HYENA_ORDER, cf, FILTER_HIDDEN), lambda c: (0, 0, c, 0)),
                  pl.BlockSpec((cf, 1), lambda c: (c, 0)),
                  pl.BlockSpec((1, n), lambda c: (0, 0))],
        out_specs=pl.BlockSpec((HYENA_ORDER, cf, n), lambda c: (0, c, 0)),
        out_shape=jax.ShapeDtypeStruct((HYENA_ORDER, HYENA_W, n), F32),
        compiler_params=_cparams("parallel"),
        name="filt_taps",
    )(h2t, w3t, dabs, tvec)


def _filt_spec_kernel(x_ref, f1_ref, tc_ref, ts_ref, f2_ref, o_ref):
    ns = x_ref.shape[0]
    x = jnp.concatenate([x_ref[s:s + 1, :].reshape(FFT_N1, FFT_N2) for s in range(ns)], axis=1).astype(BF16)
    a = jnp.dot(f1_ref[...], x, preferred_element_type=F32)
    tc = tc_ref[...]
    ts = ts_ref[...]
    rows = []
    for s in range(ns):
        ar = a[:FFT_N1, s * FFT_N2:(s + 1) * FFT_N2]
        ai = a[FFT_N1:, s * FFT_N2:(s + 1) * FFT_N2]
        rows.append(jnp.concatenate([ar * tc + ai * ts, ai * tc - ar * ts], axis=1))
    b = jnp.concatenate(rows, axis=0).astype(BF16)
    z = jnp.dot(b, f2_ref[...], preferred_element_type=F32)
    o_ref[...] = z.reshape(ns, FFT_N1, 2 * FFT_N2)


def _filt_spec(hfull, consts, ns=16):
    r = hfull.shape[0]
    f1h, tc, ts, f2 = consts["f1h"], consts["tc"], consts["ts"], consts["f2"]
    const = lambda i: (0, 0)
    return pl.pallas_call(
        _filt_spec_kernel,
        grid=(r // ns,),
        in_specs=[pl.BlockSpec((ns, FFT_N1 * FFT_N2), lambda i: (i, 0)),
                  pl.BlockSpec(f1h.shape, const), pl.BlockSpec(tc.shape, const),
                  pl.BlockSpec(ts.shape, const), pl.BlockSpec(f2.shape, const)],
        out_specs=pl.BlockSpec((ns, FFT_N1, 2 * FFT_N2), lambda i: (i, 0, 0)),
        out_shape=jax.ShapeDtypeStruct((r, FFT_N1, 2 * FFT_N2), F32),
        compiler_params=_cparams("parallel"),
        name="filt_spec",
    )(hfull, f1h, tc, ts, f2)


def _short_conv_tile(x, w, b):
    nrow = x.shape[1]
    lane = lax.broadcasted_iota(jnp.int32, x.shape, 2)
    row = lax.broadcasted_iota(jnp.int32, x.shape, 1)
    r1 = pltpu.roll(x, 1, 2)
    r2 = pltpu.roll(r1, 1, 1)
    prev = jnp.where(lane == 0, jnp.where(row == 0, 0.0, r2), r1)
    l1 = pltpu.roll(x, FFT_N2 - 1, 2)
    l2 = pltpu.roll(l1, nrow - 1, 1)
    nxt = jnp.where(lane == FFT_N2 - 1, jnp.where(row == nrow - 1, 0.0, l2), l1)
    return prev * w[0] + x * w[1] + nxt * w[2] + b


def _long_conv_pair(vr, vi, h, f1, f1i, f2, f2i, tc, ts):
    ns = vr.shape[0]
    half = FFT_N2
    x = jnp.concatenate([jnp.concatenate([vr[s] for s in range(ns)], axis=1),
                         jnp.concatenate([vi[s] for s in range(ns)], axis=1)], axis=0).astype(BF16)
    a = jnp.dot(f1, x, preferred_element_type=F32)
    rows = []
    for s in range(ns):
        ar = a[:FFT_N1, s * half:(s + 1) * half]
        ai = a[FFT_N1:, s * half:(s + 1) * half]
        rows.append(jnp.concatenate([ar * tc + ai * ts, ai * tc - ar * ts], axis=1))
    b = jnp.concatenate(rows, axis=0).astype(BF16)
    z = jnp.dot(b, f2, preferred_element_type=F32).reshape(ns, FFT_N1, 2 * half)
    zr, zi = z[:, :, :half], z[:, :, half:]
    hr, hi = h[:, :, :half], h[:, :, half:]
    w = jnp.concatenate([zr * hr - zi * hi, zr * hi + zi * hr], axis=2)
    w = w.reshape(ns * FFT_N1, 2 * half).astype(BF16)
    c = jnp.dot(w, f2i, preferred_element_type=F32).reshape(ns, FFT_N1, 2 * half)
    cr, ci = c[:, :, :half], c[:, :, half:]
    dr = cr * tc - ci * ts
    di = cr * ts + ci * tc
    d = jnp.concatenate([jnp.concatenate([dr[s] for s in range(ns)], axis=1),
                         jnp.concatenate([di[s] for s in range(ns)], axis=1)], axis=0).astype(BF16)
    y = jnp.dot(f1i, d, preferred_element_type=F32)
    nrow = y.shape[0] // 2
    yr = jnp.stack([y[:nrow, s * half:(s + 1) * half] for s in range(ns)], axis=0)
    yi = jnp.stack([y[nrow:, s * half:(s + 1) * half] for s in range(ns)], axis=0)
    return yr, yi


def _hyena_kernel(x0_ref, x1_ref, v_ref, cw_ref, cb_ref, hb_ref, h_ref,
                  f1_ref, f1i_ref, f2_ref, f2i_ref, tc_ref, ts_ref, o_ref):
    consts = (f1_ref[...], f1i_ref[...], f2_ref[...], f2i_ref[...], tc_ref[...], ts_ref[...])
    g0, g1, v = [], [], []
    for half in range(2):
        g0.append(_short_conv_tile(x0_ref[:, half], cw_ref[0], cb_ref[0]))
        g1.append(_short_conv_tile(x1_ref[:, half], cw_ref[1], cb_ref[1]))
        v.append(_short_conv_tile(v_ref[:, half], cw_ref[2], cb_ref[2]))
    gates = (g0, g1)
    for o in range(HYENA_ORDER):
        yr, yi = _long_conv_pair(v[0], v[1], h_ref[o], *consts)
        v = [gates[o][0] * (yr + hb_ref[o] * v[0]), gates[o][1] * (yi + hb_ref[o] * v[1])]
    o_ref[:, 0] = v[0].astype(BF16)
    o_ref[:, 1] = v[1].astype(BF16)


def _hyena_latent(ut, cw, cb, hb, hspec, consts, *, nb, ct=32):
    seq = ut.shape[1] // nb
    nrow = seq // FFT_N2
    npair = nb // 2
    nct = HYENA_W // ct
    u5 = ut.reshape(3 * HYENA_W, 2, npair, nrow, FFT_N2)
    part = lambda p: pl.BlockSpec((ct, 2, None, nrow, FFT_N2), lambda c, j: (p * nct + c, 0, j, 0, 0))
    const2 = lambda c, j: (0, 0)
    cnames = ("f1", "f1i", "f2", "f2i", "tc", "ts")
    out = pl.pallas_call(
        _hyena_kernel,
        grid=(nct, npair),
        in_specs=[part(0), part(1), part(2),
                  pl.BlockSpec((3, 3, ct, 1, 1), lambda c, j: (0, 0, c, 0, 0)),
                  pl.BlockSpec((3, ct, 1, 1), lambda c, j: (0, c, 0, 0)),
                  pl.BlockSpec((HYENA_ORDER, ct, 1, 1), lambda c, j: (0, c, 0, 0)),
                  pl.BlockSpec((HYENA_ORDER, ct, FFT_N1, 2 * FFT_N2), lambda c, j: (0, c, 0, 0))]
                 + [pl.BlockSpec(consts[k].shape, const2) for k in cnames],
        out_specs=pl.BlockSpec((ct, 2, None, nrow, FFT_N2), lambda c, j: (c, 0, j, 0, 0)),
        out_shape=jax.ShapeDtypeStruct((HYENA_W, 2, npair, nrow, FFT_N2), BF16),
        compiler_params=_cparams("parallel", "arbitrary"),
        name="hyena",
    )(u5, u5, u5, cw, cb, hb, hspec, *[consts[k] for k in cnames])
    return out.reshape(HYENA_W, nb * seq)


def _hyena_ctx_kernel(x0_ref, x1_ref, v_ref, cw_ref, cb_ref, hb_ref, h_ref, cf_ref, ci_ref, o_ref, *, nb):
    seq = x0_ref.shape[1] // nb
    cc = x0_ref.shape[0]
    half = h_ref.shape[2] // 2

    def rows(ref, p):
        x = jnp.concatenate([ref[:, b * seq:(b + 1) * seq] for b in range(nb)], axis=0)
        lane = lax.broadcasted_iota(jnp.int32, x.shape, 1)
        prev = jnp.where(lane == 0, 0.0, pltpu.roll(x, 1, 1))
        nxt = jnp.where(lane == seq - 1, 0.0, pltpu.roll(x, seq - 1, 1))
        w = [jnp.concatenate([cw_ref[p, j]] * nb, axis=0) for j in range(3)]
        return prev * w[0] + x * w[1] + nxt * w[2] + jnp.concatenate([cb_ref[p]] * nb, axis=0)

    g = (rows(x0_ref, 0), rows(x1_ref, 1))
    v = rows(v_ref, 2)
    for o in range(HYENA_ORDER):
        spec = jnp.dot(v.astype(BF16), cf_ref[...], preferred_element_type=F32)
        xr, xi = spec[:, :half], spec[:, half:]
        hh = jnp.concatenate([h_ref[o]] * nb, axis=0)
        hr, hi = hh[:, :half], hh[:, half:]
        y = jnp.concatenate([xr * hr - xi * hi, xr * hi + xi * hr], axis=1).astype(BF16)
        conv = jnp.dot(y, ci_ref[...], preferred_element_type=F32)
        v = g[o] * (conv + jnp.concatenate([hb_ref[o]] * nb, axis=0) * v)
    for b in range(nb):
        o_ref[:, b * seq:(b + 1) * seq] = v[b * cc:(b + 1) * cc].astype(BF16)


def _hyena_ctx(ut, cw, cb, hb, hspec, cfwd, cinv, *, nb, cc=64):
    n = ut.shape[1]
    ncc = HYENA_W // cc
    part = lambda p: pl.BlockSpec((cc, n), lambda c: (p * ncc + c, 0))
    return pl.pallas_call(
        functools.partial(_hyena_ctx_kernel, nb=nb),
        grid=(ncc,),
        in_specs=[part(0), part(1), part(2),
                  pl.BlockSpec((3, 3, cc, 1), lambda c: (0, 0, c, 0)),
                  pl.BlockSpec((3, cc, 1), lambda c: (0, c, 0)),
                  pl.BlockSpec((HYENA_ORDER, cc, 1), lambda c: (0, c, 0)),
                  pl.BlockSpec((HYENA_ORDER, cc, hspec.shape[2]), lambda c: (0, c, 0)),
                  pl.BlockSpec(cfwd.shape, lambda c: (0, 0)),
                  pl.BlockSpec(cinv.shape, lambda c: (0, 0))],
        out_specs=pl.BlockSpec((cc, n), lambda c: (c, 0)),
        out_shape=jax.ShapeDtypeStruct((HYENA_W, n), BF16),
        compiler_params=_cparams("parallel"),
        name="hyena_ctx",
    )(ut, ut, ut, cw, cb, hb, hspec, cfwd, cinv)


FF_CHUNK = 256


def _mix_ffn_kernel(h_ref, ug_ref, gb_ref, yt_ref, att_ref, hp_ref, mp_ref, wo_ref, g2_ref, mods_ref,
                    wi_ref, wd_ref, fg_ref, o_ref, *, tpb, base, final):
    row = base + pl.program_id(0) // tpb
    tn = (((0,), (0,)), ((), ()))
    g = _sigmoid(ug_ref[...].astype(F32) + gb_ref[...])
    yh = lax.dot_general(yt_ref[...], hp_ref[...], tn, preferred_element_type=F32)
    ya = lax.dot_general(att_ref[...], mp_ref[...], tn, preferred_element_type=F32)
    mix = (g[:, :D_MODEL] * yh + g[:, D_MODEL:] * ya).astype(BF16)
    h = h_ref[...] + _mod_row(mods_ref, row, 2) * jnp.dot(mix, wo_ref[...], preferred_element_type=F32)
    f = (_rms(h, g2_ref[...]) * (1.0 + _mod_row(mods_ref, row, 4)) + _mod_row(mods_ref, row, 3)).astype(BF16)
    y = None
    for c in range(0, D_FF, FF_CHUNK):
        gate = jnp.dot(f, wi_ref[:, c:c + FF_CHUNK], preferred_element_type=F32)
        up = jnp.dot(f, wi_ref[:, D_FF + c:D_FF + c + FF_CHUNK], preferred_element_type=F32)
        act = (_silu(gate) * up).astype(BF16)
        d = jnp.dot(act, wd_ref[c:c + FF_CHUNK, :], preferred_element_type=F32)
        y = d if y is None else y + d
    out = h + _mod_row(mods_ref, row, 5) * y
    o_ref[...] = _rms(out, fg_ref[...]) if final else out


def _mix_ffn(h, ug, gate_b, yt, att, hy_proj, mla_proj, w_out, g2, mods, w_in, w_down, final_g, layer,
             *, tpb, base, final, tm=512):
    t = h.shape[0]
    const = lambda i: (0, 0)
    wmap = lambda i: (layer, 0, 0)
    wspec = lambda w: pl.BlockSpec((None,) + w.shape[1:], wmap, pipeline_mode=pl.Buffered(1))
    return pl.pallas_call(
        functools.partial(_mix_ffn_kernel, tpb=tpb, base=base, final=final),
        grid=(t // tm,),
        in_specs=[pl.BlockSpec((tm, D_MODEL), lambda i: (i, 0)),
                  pl.BlockSpec((tm, 2 * D_MODEL), lambda i: (i, 0)),
                  pl.BlockSpec((None, 1, 2 * D_MODEL), wmap),
                  pl.BlockSpec((HYENA_W, tm), lambda i: (0, i)),
                  pl.BlockSpec((MLA_HEADS * V_HEAD, tm), lambda i: (0, i)),
                  wspec(hy_proj), wspec(mla_proj), wspec(w_out),
                  pl.BlockSpec((None, 1, D_MODEL), wmap),
                  pl.BlockSpec((NMOD, 6 * D_MODEL), const),
                  wspec(w_in), wspec(w_down),
                  pl.BlockSpec((1, D_MODEL), const)],
        out_specs=pl.BlockSpec((tm, D_MODEL), lambda i: (i, 0)),
        out_shape=jax.ShapeDtypeStruct((t, D_MODEL), F32),
        compiler_params=_cparams("parallel"),
        name="mix_ffn",
    )(h, ug, gate_b, yt, att, hy_proj, mla_proj, w_out, g2, mods, w_in, w_down, final_g.reshape(1, -1))


def _dft_consts():
    n = FFT_N1 * FFT_N2
    k1 = np.arange(FFT_N1)
    th = 2 * np.pi * np.outer(k1, np.arange(FFT_N1)) / FFT_N1
    c, s = np.cos(th), np.sin(th)
    hn = FFT_N1 // 2
    f1 = np.block([[c[:, :hn], s[:, :hn]], [-s[:, :hn], c[:, :hn]]])
    f1h = np.concatenate([c, -s], axis=0)
    f1i = np.block([[c[:, :hn].T, -s[:, :hn].T], [s[:, :hn].T, c[:, :hn].T]]) / n
    ph = 2 * np.pi * np.outer(k1, np.arange(FFT_N2)) / n
    th2 = 2 * np.pi * np.outer(np.arange(FFT_N2), np.arange(FFT_N2)) / FFT_N2
    c2, s2 = np.cos(th2), np.sin(th2)
    f2 = np.block([[c2, -s2], [s2, c2]])
    f2i = np.block([[c2, s2], [-s2, c2]])
    return {"f1": jnp.asarray(f1, BF16), "f1h": jnp.asarray(f1h, BF16), "f1i": jnp.asarray(f1i, BF16),
            "f2": jnp.asarray(f2, BF16), "f2i": jnp.asarray(f2i, BF16),
            "tc": jnp.asarray(np.cos(ph), F32), "ts": jnp.asarray(np.sin(ph), F32)}


def _dense_dft(seq):
    n = 2 * seq
    th = 2 * np.pi * np.outer(np.arange(n), np.arange(n)) / n
    c, s = np.cos(th), np.sin(th)
    full = np.concatenate([c, -s], axis=1)
    inv = np.concatenate([c[:, :seq], -s[:, :seq]], axis=0) / n
    return jnp.asarray(full, F32), jnp.asarray(full[:seq], BF16), jnp.asarray(inv, BF16)


def _pos_features(seq):
    n = np.arange(2 * seq)
    lag = np.minimum(np.where(n < seq, n, 2 * seq - n), seq - 1)
    t = np.linspace(0.0, 1.0, seq, dtype=np.float32)[lag]
    bands = np.arange(1, N_BANDS + 1, dtype=np.float32)
    ang = (np.float32(2 * math.pi) * bands)[:, None] * t[None, :]
    zt = np.concatenate([t[None, :], np.cos(ang), np.sin(ang), np.zeros((7, 2 * seq), np.float32)], axis=0)
    return jnp.asarray(zt, F32), jnp.asarray(t.reshape(1, -1), F32)


def _rope_tables(seq):
    rows = seq // GRID_W
    row = np.repeat(np.arange(rows, dtype=np.float32), GRID_W)
    col = np.tile(np.arange(GRID_W, dtype=np.float32), rows)
    inv = (np.float32(ROPE_BASE) ** (-np.arange(0, AXIS_ROPE, 2, dtype=np.float32) / AXIS_ROPE)).astype(np.float32)
    ang = np.concatenate([row[:, None] * inv, col[:, None] * inv], axis=-1)
    cos, sin = np.cos(ang), np.sin(ang)
    ones = np.ones((seq, QK_NOPE), np.float32)
    pad = HEAD_PAD - QK_NOPE - QK_ROPE
    ctab = np.concatenate([ones, cos, cos, np.ones((seq, pad), np.float32)], axis=1)
    stab = np.concatenate([0 * ones, sin, sin, np.zeros((seq, pad), np.float32)], axis=1)
    return jnp.asarray(ctab, F32), jnp.asarray(stab, F32)


def _rot_cols(w):
    half = QK_ROPE // 2
    return jnp.concatenate([-w[..., half:], w[..., :half]], axis=-1)


def _pad_heads(nope, rope):
    lead = nope.shape[:-2]
    z = jnp.zeros(lead + (MLA_HEADS, HEAD_PAD - QK_NOPE - QK_ROPE), nope.dtype)
    return jnp.concatenate([nope, rope, z], axis=-1).reshape(lead + (MLA_HEADS * HEAD_PAD,))


def _prep_weights(w_in, w_uq, w_ukv):
    depth, d = w_in.shape[0], w_in.shape[1]
    w_hy_t = jnp.swapaxes(w_in[:, :, :OFF_Q], 1, 2).astype(BF16)
    kr_w = w_in[:, :, OFF_KR:OFF_G]
    zl = jnp.zeros((depth, d, ROPE_LANE), F32)
    zr = jnp.zeros((depth, d, HEAD_PAD - ROPE_LANE - QK_ROPE), F32)
    w_qkv = jnp.concatenate([w_in[:, :, OFF_Q:OFF_KR], zl, kr_w, zr, zl, _rot_cols(kr_w), zr], axis=2).astype(BF16)
    w_gate = w_in[:, :, OFF_G:].astype(BF16)
    uq = w_uq.reshape(depth, Q_LORA, MLA_HEADS, QK_NOPE + QK_ROPE)
    wq = _pad_heads(uq[..., :QK_NOPE], uq[..., QK_NOPE:]).astype(BF16)
    ukv = w_ukv.reshape(depth, KV_LORA, MLA_HEADS, QK_NOPE + V_HEAD)
    wk = _pad_heads(ukv[..., :QK_NOPE], jnp.zeros((depth, KV_LORA, MLA_HEADS, QK_ROPE), F32)).astype(BF16)
    wvt = jnp.swapaxes(ukv[..., QK_NOPE:].reshape(depth, KV_LORA, MLA_HEADS * V_HEAD), 1, 2).astype(BF16)
    return w_hy_t, w_qkv, w_gate, wq, wk, wvt


def kernel(x, c, ctx, c_ctx, ada_w, ada_b, norm1_g, norm2_g, w_in, gate_b, hy_conv_w, hy_conv_b,
           hy_filt_w1, hy_filt_b1, hy_filt_w2, hy_filt_b2, hy_filt_w3, hy_filt_freq, hy_bias, hy_proj,
           q_norm_g, kv_norm_g, w_uq, w_ukv, mla_proj, w_out, ffn_w_in, ffn_w_out, final_norm_g):
    nb, seq, d = x.shape
    lc = ctx.shape[1]
    depth = ada_w.shape[0]
    assert d == D_MODEL and seq == FFT_N1 * FFT_N2 // 2 and nb % 2 == 0 and nb < NMOD
    tm = 512
    tpb_lat = seq // tm
    tpb_ctx = 1 << 30

    cc = jnp.zeros((NMOD, d), F32).at[:nb].set(c).at[nb].set(c_ctx)
    mods_all = _ada(cc, ada_w, ada_b)

    consts = _dft_consts()
    cfull_c, cfwd_c, cinv_c = _dense_dft(lc)
    ctab, stab = _rope_tables(seq)
    one_tab = jnp.ones((tm, HEAD_PAD), F32)
    zero_tab = jnp.zeros((tm, HEAD_PAD), F32)
    dabs = jnp.asarray(np.abs(np.linspace(HYENA_MIN_DECAY, HYENA_MAX_DECAY, HYENA_W, dtype=np.float32)).reshape(HYENA_W, 1))
    zt_lat, tv_lat = _pos_features(seq)
    zt_ctx, tv_ctx = _pos_features(lc)

    w_hy_t, w_qkv, w_gate, wq, wk, wvt = _prep_weights(w_in, w_uq, w_ukv)
    hp, mp, wo = hy_proj.astype(BF16), mla_proj.astype(BF16), w_out.astype(BF16)
    w_ffi, w_ffo = ffn_w_in.astype(BF16), ffn_w_out.astype(BF16)
    qg = q_norm_g.reshape(depth, 1, Q_LORA)
    kvg = kv_norm_g.reshape(depth, 1, KV_LORA)
    gb = gate_b.reshape(depth, 1, 2 * D_MODEL)
    n1 = norm1_g
    n2 = norm2_g.reshape(depth, 1, D_MODEL)
    cw_all = jnp.transpose(hy_conv_w.reshape(depth, 3, 3, HYENA_W), (0, 2, 1, 3))
    cb_all = hy_conv_b.reshape(depth, 3, HYENA_W)

    h_lat = x.reshape(nb * seq, d)
    h_ctx = ctx.reshape(nb * lc, d)

    for i in range(depth):
        last = i == depth - 1
        mods = mods_all[i]
        cw, cb, hb = cw_all[i], cb_all[i], hy_bias[i]

        filt = (hy_filt_w1[i], hy_filt_b1[i], hy_filt_w2[i], hy_filt_b2[i], hy_filt_freq[i])
        hfull = _filt_taps(_filt_mlp(zt_lat, *filt), hy_filt_w3[i], dabs, tv_lat, seq)
        hspec = _filt_spec(hfull.reshape(HYENA_ORDER * HYENA_W, 2 * seq), consts)
        hspec = hspec.reshape(HYENA_ORDER, HYENA_W, FFT_N1, 2 * FFT_N2)

        uq_l, ug_l, ut_l = _in_proj(h_lat, n1[i], mods, w_qkv, w_gate, w_hy_t, i, tpb=tpb_lat, base=0)
        uq_c, ug_c, ut_c = _in_proj(h_ctx, n1[i], mods, w_qkv, w_gate, w_hy_t, i, tpb=tpb_ctx, base=nb)
        q_l, k_l, v_l = _qkv_up(uq_l, qg, kvg, wq, wk, wvt, ctab, stab, i, rope=True)
        q_c, k_c, v_c = _qkv_up(uq_c, qg, kvg, wq, wk, wvt, one_tab, zero_tab, i, rope=False)
        att_l = _attention(q_l, [(k_l, v_l, seq), (k_c, v_c, lc)], nb=nb)
        yt_l = _hyena_latent(ut_l, cw.reshape(3, 3, HYENA_W, 1, 1), cb.reshape(3, HYENA_W, 1, 1),
                             hb.reshape(HYENA_ORDER, HYENA_W, 1, 1), hspec, consts, nb=nb)
        h_lat = _mix_ffn(h_lat, ug_l, gb, yt_l, att_l, hp, mp, wo, n2, mods, w_ffi, w_ffo, final_norm_g, i,
                         tpb=tpb_lat, base=0, final=last)

        if not last:
            hfull_c = _filt_taps(_filt_mlp(zt_ctx, *filt), hy_filt_w3[i], dabs, tv_ctx, lc)
            hspec_c = _matmul_hi(hfull_c.reshape(HYENA_ORDER * HYENA_W, 2 * lc), cfull_c, tm=256)
            hspec_c = hspec_c.reshape(HYENA_ORDER, HYENA_W, 4 * lc)
            att_c = _attention(q_c, [(k_c, v_c, lc)], nb=nb, hps=MLA_HEADS)
            yt_c = _hyena_ctx(ut_c, cw.reshape(3, 3, HYENA_W, 1), cb.reshape(3, HYENA_W, 1),
                              hb.reshape(HYENA_ORDER, HYENA_W, 1), hspec_c, cfwd_c, cinv_c, nb=nb)
            h_ctx = _mix_ffn(h_ctx, ug_c, gb, yt_c, att_c, hp, mp, wo, n2, mods, w_ffi, w_ffo, final_norm_g, i,
                             tpb=tpb_ctx, base=nb, final=False)

    return h_lat.reshape(nb, seq, d)
```
